```python
import math
import jax, jax.numpy as jnp
from jax import lax
import numpy as np

D_MODEL = 1024
BATCH = 2
SEQ = 8192
DEPTH = 4

N_MIXERS = 3
PLE_DIM = 256
EPS = 1e-6
N_NORMS = 5

RET_HEADS = 4
RET_QK_DIM = D_MODEL // RET_HEADS
RET_V_DIM = 2 * RET_QK_DIM
RET_CHUNK = 128
ROPE_BASE = 10000.0
RET_IN_WIDTH = 2 * RET_HEADS * RET_QK_DIM + 2 * RET_HEADS * RET_V_DIM

ATT_HEADS = 16
ATT_HEAD_DIM = D_MODEL // ATT_HEADS
DILATION_PAIRS = ((128, 1), (512, 4), (2048, 16))
N_DIL_GROUPS = len(DILATION_PAIRS)
ATT_IN_WIDTH = N_DIL_GROUPS * 3 * ATT_HEADS * ATT_HEAD_DIM

REL_BUCKETS = 32
REL_MAX_EXACT = REL_BUCKETS // 2
REL_MAX_DIST = 2048

GM_CHUNK = 128
GM_WIDTH = 2 * D_MODEL
GM_GROUPS = 8
GM_GROUP_DIM = GM_WIDTH // GM_GROUPS

FFN_HIDDEN = -(-(8 * D_MODEL) // (3 * 256)) * 256

N_A = (DEPTH + 2) // 3
N_B = (DEPTH + 1) // 3
N_C = DEPTH // 3

kernel_name = "hybrid_retention_dilated_gmlp_trunk"


def _rmsnorm(x, g):
    xf = x.astype(jnp.float32)
    y = xf * lax.rsqrt(jnp.mean(xf * xf, axis=-1, keepdims=True) + EPS)
    return (y * g.astype(jnp.float32)).astype(x.dtype)


def _rope(t, seq_len):
    d = t.shape[-1]
    inv_freq = ROPE_BASE ** (-jnp.arange(0, d, 2, dtype=jnp.float32) / d)
    ang = jnp.arange(seq_len, dtype=jnp.float32)[:, None] * inv_freq[None, :]
    cos = jnp.cos(ang)[None, :, None, :]
    sin = jnp.sin(ang)[None, :, None, :]
    t1, t2 = jnp.split(t, 2, axis=-1)
    return jnp.concatenate([t1 * cos - t2 * sin, t2 * cos + t1 * sin], axis=-1)


def _retention_mixer(h, w_in, w_out):
    B, S, _ = h.shape
    H, dk, dv, C = RET_HEADS, RET_QK_DIM, RET_V_DIM, RET_CHUNK
    proj = h @ w_in
    q, k, v, g = jnp.split(proj, [H * dk, 2 * H * dk, 2 * H * dk + H * dv], axis=-1)
    q = _rope(q.reshape(B, S, H, dk).astype(jnp.float32), S)
    k = _rope(k.reshape(B, S, H, dk).astype(jnp.float32), S) * (dk ** -0.5)
    v = v.reshape(B, S, H, dv).astype(jnp.float32)

    log_gamma = jnp.log1p(-jnp.exp2(-5.0 - jnp.arange(H, dtype=jnp.float32)))
    idx = jnp.arange(C, dtype=jnp.float32)
    diff = idx[:, None] - idx[None, :]
    decay_mask = jnp.where(diff >= 0, jnp.exp(log_gamma[:, None, None] * jnp.maximum(diff, 0.0)), 0.0)
    q_decay = jnp.exp(log_gamma[None, :] * (idx[:, None] + 1.0))
    k_decay = jnp.exp(log_gamma[None, :] * (C - 1.0 - idx[:, None]))
    chunk_decay = jnp.exp(log_gamma * C)

    n_chunks = S // C

    def to_chunks(t):
        return t.reshape(B, n_chunks, C, H, t.shape[-1]).transpose(1, 0, 2, 3, 4)

    def step(state, qkv):
        qc, kc, vc = qkv
        scores = jnp.einsum('bnhk,bmhk->bhnm', qc, kc) * decay_mask[None]
        inner = jnp.einsum('bhnm,bmhv->bnhv', scores, vc)
        cross = jnp.einsum('bnhk,bhkv->bnhv', qc, state) * q_decay[None, :, :, None]
        new_state = state * chunk_decay[None, :, None, None] + jnp.einsum(
            'bmhk,bmhv->bhkv', kc * k_decay[None, :, :, None], vc)
        return new_state, inner + cross

    state0 = jnp.zeros((B, H, dk, dv), jnp.float32)
    _, ys = lax.scan(step, state0, (to_chunks(q), to_chunks(k), to_chunks(v)))
    y = ys.transpose(1, 0, 2, 3, 4).reshape(B, S, H, dv)
    y = y * lax.rsqrt(jnp.mean(y * y, axis=-1, keepdims=True) + EPS)
    y = y.reshape(B, S, H * dv).astype(h.dtype)
    return (jax.nn.silu(g) * y) @ w_out


def _t5_bucket(dist):
    d = jnp.maximum(dist, 1).astype(jnp.float32)
    large = REL_MAX_EXACT + (jnp.log(d / REL_MAX_EXACT) / math.log(REL_MAX_DIST / REL_MAX_EXACT)
                             * (REL_BUCKETS - REL_MAX_EXACT)).astype(jnp.int32)
    large = jnp.minimum(large, REL_BUCKETS - 1)
    return jnp.where(dist < REL_MAX_EXACT, dist, large)


def _dilated_group(q, k, v, bias_table, window, dilation):
    B, S, H, Dh = q.shape
    r = dilation
    blk = window // dilation
    L = -(-S // (r * blk)) * r * blk
    pad = ((0, 0), (0, L - S), (0, 0), (0, 0))
    nb = L // r // blk

    def to_sub(t):
        t = jnp.pad(t, pad).reshape(B, L // r, r, H, Dh).transpose(0, 2, 1, 3, 4)
        return t.reshape(B, r, nb, blk, H, Dh)

    qb, kb, vb = to_sub(q), to_sub(k), to_sub(v)
    shift = ((0, 0), (0, 0), (1, 0), (0, 0), (0, 0), (0, 0))
    kk = jnp.concatenate([jnp.pad(kb, shift)[:, :, :-1], kb], axis=3)
    vv = jnp.concatenate([jnp.pad(vb, shift)[:, :, :-1], vb], axis=3)

    qi = jnp.arange(blk)[:, None]
    kj = jnp.arange(2 * blk)[None, :]
    dist_sub = blk + qi - kj
    band = (dist_sub >= 0) & (dist_sub <= blk)
    mask = band[None] & ((jnp.arange(nb)[:, None, None] > 0) | (kj >= blk)[None])
    bias = bias_table[_t5_bucket(jnp.maximum(dist_sub, 0) * r)].astype(jnp.float32)
    bias = bias.transpose(2, 0, 1)

    logits = jnp.einsum('brnqhd,brnkhd->brnhqk', qb, kk) * (Dh ** -0.5) + bias
    logits = jnp.where(mask[:, None], logits, -jnp.inf)
    m = jnp.max(logits, axis=-1, keepdims=True)
    e = jnp.exp(logits - m)
    s = jnp.sum(e, axis=-1)
    o = jnp.einsum('brnhqk,brnkhd->brnqhd', e, vv) / s.transpose(0, 1, 2, 4, 3)[..., None]
    lse = (m[..., 0] + jnp.log(s)).transpose(0, 1, 2, 4, 3)

    o = o.reshape(B, r, L // r, H, Dh).transpose(0, 2, 1, 3, 4).reshape(B, L, H, Dh)[:, :S]
    lse = lse.reshape(B, r, L // r, H).transpose(0, 2, 1, 3).reshape(B, L, H)[:, :S]
    return o, lse


def _dilated_attention_mixer(h, w_in, w_out, rel_bias):
    B, S, _ = h.shape
    H = ATT_HEADS
    qkv = (h @ w_in).reshape(B, S, N_DIL_GROUPS, 3, H, ATT_HEAD_DIM).astype(jnp.float32)
    outs, lses = [], []
    for gi, (window, dil) in enumerate(DILATION_PAIRS):
        o, lse = _dilated_group(qkv[:, :, gi, 0], qkv[:, :, gi, 1], qkv[:, :, gi, 2],
                                rel_bias[:, gi * H:(gi + 1) * H], window, dil)
        outs.append(o)
        lses.append(lse)
    wts = jax.nn.softmax(jnp.stack(lses, axis=0), axis=0)
    o = jnp.sum(wts[..., None] * jnp.stack(outs, axis=0), axis=0)
    return o.reshape(B, S, H * ATT_HEAD_DIM).astype(h.dtype) @ w_out


def _gmlp_mixer(h, w_in, ln_g, ln_b, w_s, b_s, w_out):
    B, S, _ = h.shape
    C, G = GM_CHUNK, GM_GROUPS
    z = jax.nn.gelu(h @ w_in, approximate=False)
    u, v = jnp.split(z, 2, axis=-1)
    vf = v.astype(jnp.float32)
    mu = jnp.mean(vf, axis=-1, keepdims=True)
    var = jnp.mean(jnp.square(vf - mu), axis=-1, keepdims=True)
    vn = (vf - mu) * lax.rsqrt(var + EPS) * ln_g.astype(jnp.float32) + ln_b.astype(jnp.float32)
    vn = vn.reshape(B, S // C, C, G, GM_GROUP_DIM)
    causal = jnp.tril(jnp.ones((C, C), jnp.float32))
    s = jnp.einsum('gnm,bcmgd->bcngd', w_s.astype(jnp.float32) * causal[None], vn)
    s = s + b_s.astype(jnp.float32).T[None, None, :, :, None]
    s = s.reshape(B, S, GM_WIDTH).astype(h.dtype)
    return (u * s) @ w_out


def _swiglu(h, w_in, w_out):
    gate, up = jnp.split(h @ w_in, 2, axis=-1)
    return (jax.nn.silu(gate) * up) @ w_out


def setup_inputs(seed: int = 0) -> dict:
    key = jax.random.key(seed)
    ks = jax.random.split(key, 18)

    def nrm(k, shape, scale):
        return jax.random.normal(k, shape, jnp.float32) * scale

    D = D_MODEL
    return {
        "x": nrm(ks[0], (BATCH, SEQ, D), 1.0),
        "p": nrm(ks[1], (DEPTH, BATCH, SEQ, PLE_DIM), 1.0),
        "norm_g": 1.0 + nrm(ks[2], (DEPTH, N_NORMS, D), 0.1),
        "ret_w_in": nrm(ks[3], (N_A, D, RET_IN_WIDTH), D ** -0.5),
        "ret_w_out": nrm(ks[4], (N_A, RET_HEADS * RET_V_DIM, D), (RET_HEADS * RET_V_DIM) ** -0.5),
        "attn_w_in": nrm(ks[5], (N_B, D, ATT_IN_WIDTH), D ** -0.5),
        "attn_w_out": nrm(ks[6], (N_B, ATT_HEADS * ATT_HEAD_DIM, D), (ATT_HEADS * ATT_HEAD_DIM) ** -0.5),
        "rel_bias": nrm(ks[7], (REL_BUCKETS, N_DIL_GROUPS * ATT_HEADS), 0.3),
        "gm_w_in": nrm(ks[8], (N_C, D, 2 * GM_WIDTH), D ** -0.5),
        "gm_ln_g": 1.0 + nrm(ks[9], (N_C, GM_WIDTH), 0.1),
        "gm_ln_b": nrm(ks[10], (N_C, GM_WIDTH), 0.02),
        "gm_w_s": nrm(ks[11], (N_C, GM_GROUPS, GM_CHUNK, GM_CHUNK), GM_CHUNK ** -0.5),
        "gm_b_s": 1.0 + nrm(ks[12], (N_C, GM_GROUPS, GM_CHUNK), 0.1),
        "gm_w_out": nrm(ks[13], (N_C, GM_WIDTH, D), GM_WIDTH ** -0.5),
        "ffn_w_in": nrm(ks[14], (DEPTH, D, 2 * FFN_HIDDEN), D ** -0.5),
        "ffn_w_out": nrm(ks[15], (DEPTH, FFN_HIDDEN, D), FFN_HIDDEN ** -0.5),
        "ple_w_proj": nrm(ks[16], (DEPTH, PLE_DIM, D), PLE_DIM ** -0.5),
        "ple_w_gate": nrm(ks[17], (DEPTH, D, D), D ** -0.5),
    }


def reference(x, p, norm_g, ret_w_in, ret_w_out, attn_w_in, attn_w_out, rel_bias,
              gm_w_in, gm_ln_g, gm_ln_b, gm_w_s, gm_b_s, gm_w_out,
              ffn_w_in, ffn_w_out, ple_w_proj, ple_w_gate):
    h = x
    for i in range(DEPTH):
        kind, j = i % N_MIXERS, i // N_MIXERS
        a = _rmsnorm(h, norm_g[i, 0])
        if kind == 0:
            mix = _retention_mixer(a, ret_w_in[j], ret_w_out[j])
        elif kind == 1:
            mix = _dilated_attention_mixer(a, attn_w_in[j], attn_w_out[j], rel_bias)
        else:
            mix = _gmlp_mixer(a, gm_w_in[j], gm_ln_g[j], gm_ln_b[j], gm_w_s[j], gm_b_s[j], gm_w_out[j])
        h = h + _rmsnorm(mix.astype(h.dtype), norm_g[i, 1])
        f = _swiglu(_rmsnorm(h, norm_g[i, 2]), ffn_w_in[i], ffn_w_out[i])
        h = h + _rmsnorm(f.astype(h.dtype), norm_g[i, 3])
        emb = p[i].astype(h.dtype) @ ple_w_proj[i]
        gate = jax.nn.sigmoid(_rmsnorm(h, norm_g[i, 4]) @ ple_w_gate[i])
        h = h + gate * emb
    return h
```

```python
import functools
import math

import jax
import jax.numpy as jnp
import numpy as np
from jax import lax
from jax.experimental import pallas as pl
from jax.experimental.pallas import tpu as pltpu

D_MODEL = 1024
DEPTH = 4
N_MIXERS = 3
PLE_DIM = 256
EPS = 1e-6

RET_HEADS = 4
RET_QK_DIM = 256
RET_V_DIM = 512
RET_CHUNK = 128
ROPE_BASE = 10000.0

ATT_HEADS = 16
ATT_HEAD_DIM = 64
DILATION_PAIRS = ((128, 1), (512, 4), (2048, 16))
ATT_BLK = 128
REL_BUCKETS = 32
REL_MAX_EXACT = 16
REL_MAX_DIST = 2048

GM_CHUNK = 128
GM_WIDTH = 2 * D_MODEL
GM_GROUPS = 8
GM_GROUP_DIM = GM_WIDTH // GM_GROUPS

FFN_HIDDEN = 2816

V7X_LANES = 128
V7X_VMEM_BYTES = 64 * 1024 * 1024
VMEM_LIMIT_BYTES = V7X_VMEM_BYTES - 8 * 1024 * 1024

MASK_VALUE = -1e30

BF16 = jnp.bfloat16
F32 = jnp.float32


def _resident(shape):
    return pl.BlockSpec(shape, lambda *_: (0,) * len(shape), pipeline_mode=pl.Buffered(1))


def _rms(x, g):
    return x * lax.rsqrt(jnp.mean(x * x, axis=-1, keepdims=True) + EPS) * g


def _dot(a, b):
    return jnp.dot(a, b, preferred_element_type=F32)


def _dot_nt(a, b):
    return lax.dot_general(a, b, (((1,), (1,)), ((), ())), preferred_element_type=F32)


def _dot_tn(a, b):
    return lax.dot_general(a, b, (((0,), (0,)), ((), ())), preferred_element_type=F32)


def _params(*semantics):
    return pltpu.CompilerParams(dimension_semantics=semantics, vmem_limit_bytes=VMEM_LIMIT_BYTES)


FFN_TM = 512
FFN_CHUNK = FFN_HIDDEN // 2


def _ffn_ple_kernel(h_ref, p_ref, g2_ref, g3_ref, g4_ref, win_ref, wout_ref, wp_ref, wg_ref, o_ref):
    x = h_ref[...]
    a = _rms(x, g2_ref[...]).astype(BF16)
    f = jnp.zeros(x.shape, F32)
    for c in range(FFN_HIDDEN // FFN_CHUNK):
        lo = c * FFN_CHUNK
        gate = _dot(a, win_ref[:, lo:lo + FFN_CHUNK])
        up = _dot(a, win_ref[:, FFN_HIDDEN + lo:FFN_HIDDEN + lo + FFN_CHUNK])
        hid = (gate * jax.nn.sigmoid(gate) * up).astype(BF16)
        f = f + _dot(hid, wout_ref[lo:lo + FFN_CHUNK, :])
    h1 = x + _rms(f, g3_ref[...])
    emb = _dot(p_ref[...].astype(BF16), wp_ref[...])
    gate = jax.nn.sigmoid(_dot(_rms(h1, g4_ref[...]).astype(BF16), wg_ref[...]))
    o_ref[...] = h1 + gate * emb


def _ffn_ple(h, p, g2, g3, g4, w_in, w_out, w_proj, w_gate):
    n, d = h.shape
    row = lambda i: (i, 0)
    return pl.pallas_call(
        _ffn_ple_kernel,
        grid=(n // FFN_TM,),
        in_specs=[
            pl.BlockSpec((FFN_TM, d), row),
            pl.BlockSpec((FFN_TM, PLE_DIM), row),
            _resident((1, d)), _resident((1, d)), _resident((1, d)),
            _resident(w_in.shape), _resident(w_out.shape), _resident(w_proj.shape), _resident(w_gate.shape),
        ],
        out_specs=pl.BlockSpec((FFN_TM, d), row),
        out_shape=jax.ShapeDtypeStruct((n, d), F32),
        compiler_params=_params("arbitrary"),
        name="ffn_ple",
    )(h, p, g2, g3, g4, w_in, w_out, w_proj, w_gate)


RET_TM = 256
RET_Q_OFF = 0
RET_K_OFF = RET_HEADS * RET_QK_DIM
RET_V_OFF = 2 * RET_HEADS * RET_QK_DIM
RET_G_OFF = RET_V_OFF + RET_HEADS * RET_V_DIM


def _rope(t, cos, sin):
    half = RET_QK_DIM // 2
    t1, t2 = t[:, :half], t[:, half:]
    return jnp.concatenate([t1 * cos - t2 * sin, t2 * cos + t1 * sin], axis=-1)


def _retention_kernel(h_ref, cos_ref, sin_ref, g0_ref, g1_ref, win_ref, wout_ref,
                      dmask_ref, qdec_ref, kdec_ref, cdec_ref, o_ref, state_ref, y_ref):
    @pl.when(pl.program_id(1) == 0)
    def _():
        state_ref[...] = jnp.zeros(state_ref.shape, F32)

    x = h_ref[...]
    a = _rms(x, g0_ref[...]).astype(BF16)
    cos, sin = cos_ref[...], sin_ref[...]
    for hd in range(RET_HEADS):
        qo, ko, vo = RET_Q_OFF + hd * RET_QK_DIM, RET_K_OFF + hd * RET_QK_DIM, RET_V_OFF + hd * RET_V_DIM
        q = _rope(_dot(a, win_ref[:, qo:qo + RET_QK_DIM]), cos, sin).astype(BF16)
        k = _rope(_dot(a, win_ref[:, ko:ko + RET_QK_DIM]), cos, sin) * (RET_QK_DIM ** -0.5)
        v = _dot(a, win_ref[:, vo:vo + RET_V_DIM]).astype(BF16)
        for c in range(RET_TM // RET_CHUNK):
            rows = slice(c * RET_CHUNK, (c + 1) * RET_CHUNK)
            qc, kc, vc = q[rows], k[rows], v[rows]
            st = state_ref[hd]
            scores = _dot_nt(qc, kc.astype(BF16)) * dmask_ref[hd]
            inner = _dot(scores.astype(BF16), vc)
            cross = _dot(qc, st.astype(BF16)) * qdec_ref[hd]
            state_ref[hd] = st * cdec_ref[hd] + _dot_tn((kc * kdec_ref[hd]).astype(BF16), vc)
            y = inner + cross
            y_ref[rows, hd * RET_V_DIM:(hd + 1) * RET_V_DIM] = y * lax.rsqrt(
                jnp.mean(y * y, axis=-1, keepdims=True) + EPS)
    gate = _dot(a, win_ref[:, RET_G_OFF:RET_G_OFF + RET_HEADS * RET_V_DIM])
    z = (gate * jax.nn.sigmoid(gate) * y_ref[...]).astype(BF16)
    o_ref[...] = x + _rms(_dot(z, wout_ref[...]), g1_ref[...])


def _retention_consts(seq_len):
    hh, c, dk = RET_HEADS, RET_CHUNK, RET_QK_DIM
    inv_freq = ROPE_BASE ** (-jnp.arange(0, dk, 2, dtype=F32) / dk)
    ang = jnp.arange(seq_len, dtype=F32)[:, None] * inv_freq[None, :]
    log_gamma = jnp.log1p(-jnp.exp2(-5.0 - jnp.arange(hh, dtype=F32)))
    idx = jnp.arange(c, dtype=F32)
    diff = idx[:, None] - idx[None, :]
    dmask = jnp.where(diff >= 0, jnp.exp(log_gamma[:, None, None] * jnp.maximum(diff, 0.0)), 0.0)
    qdec = jnp.exp(log_gamma[:, None] * (idx[None, :] + 1.0))[:, :, None]
    kdec = jnp.exp(log_gamma[:, None] * (c - 1.0 - idx[None, :]))[:, :, None]
    cdec = jnp.exp(log_gamma * c)[:, None, None]
    return jnp.cos(ang), jnp.sin(ang), dmask, qdec, kdec, cdec


def _retention_mixer(h, g0, g1, w_in, w_out, consts):
    b, s, d = h.shape
    cos, sin, dmask, qdec, kdec, cdec = consts
    tile = lambda bi, t: (bi, t, 0)
    pos = lambda bi, t: (t, 0)
    return pl.pallas_call(
        _retention_kernel,
        grid=(b, s // RET_TM),
        in_specs=[
            pl.BlockSpec((None, RET_TM, d), tile),
            pl.BlockSpec((RET_TM, RET_QK_DIM // 2), pos),
            pl.BlockSpec((RET_TM, RET_QK_DIM // 2), pos),
            _resident((1, d)), _resident((1, d)),
            _resident(w_in.shape), _resident(w_out.shape),
            _resident(dmask.shape), _resident(qdec.shape), _resident(kdec.shape), _resident(cdec.shape),
        ],
        out_specs=pl.BlockSpec((None, RET_TM, d), tile),
        out_shape=jax.ShapeDtypeStruct((b, s, d), F32),
        scratch_shapes=[
            pltpu.VMEM((RET_HEADS, RET_QK_DIM, RET_V_DIM), F32),
            pltpu.VMEM((RET_TM, RET_HEADS * RET_V_DIM), F32),
        ],
        compiler_params=_params("arbitrary", "arbitrary"),
        name="retention_mixer",
    )(h, cos, sin, g0, g1, w_in, w_out, dmask, qdec, kdec, cdec)


GM_TM = 256


def _gmlp_kernel(h_ref, g0_ref, g1_ref, win_ref, lng_ref, lnb_ref, ws_ref, bs_ref, wout_ref, o_ref, s_ref):
    x = h_ref[...]
    a = _rms(x, g0_ref[...]).astype(BF16)

    def gelu(t):
        return 0.5 * t * (1.0 + lax.erf(t * (2.0 ** -0.5)))

    v = gelu(_dot(a, win_ref[:, GM_WIDTH:]))
    mu = jnp.mean(v, axis=-1, keepdims=True)
    vc = v - mu
    var = jnp.mean(vc * vc, axis=-1, keepdims=True)
    vn = (vc * lax.rsqrt(var + EPS) * lng_ref[...] + lnb_ref[...]).astype(BF16)
    row = lax.broadcasted_iota(jnp.int32, (GM_CHUNK, GM_CHUNK), 0)
    col = lax.broadcasted_iota(jnp.int32, (GM_CHUNK, GM_CHUNK), 1)
    causal = row >= col
    for g in range(GM_GROUPS):
        w = jnp.where(causal, ws_ref[g], 0.0).astype(BF16)
        cols = slice(g * GM_GROUP_DIM, (g + 1) * GM_GROUP_DIM)
        for c in range(GM_TM // GM_CHUNK):
            rows = slice(c * GM_CHUNK, (c + 1) * GM_CHUNK)
            s_ref[rows, cols] = _dot(w, vn[rows, cols]) + bs_ref[g]
    u = gelu(_dot(a, win_ref[:, :GM_WIDTH]))
    z = (u * s_ref[...]).astype(BF16)
    o_ref[...] = x + _rms(_dot(z, wout_ref[...]), g1_ref[...])


def _gmlp_mixer(h, g0, g1, w_in, ln_g, ln_b, w_s, b_s, w_out):
    n, d = h.shape
    row = lambda i: (i, 0)
    return pl.pallas_call(
        _gmlp_kernel,
        grid=(n // GM_TM,),
        in_specs=[
            pl.BlockSpec((GM_TM, d), row),
            _resident((1, d)), _resident((1, d)),
            _resident(w_in.shape), _resident(ln_g.shape), _resident(ln_b.shape),
            _resident(w_s.shape), _resident(b_s.shape), _resident(w_out.shape),
        ],
        out_specs=pl.BlockSpec((GM_TM, d), row),
        out_shape=jax.ShapeDtypeStruct((n, d), F32),
        scratch_shapes=[pltpu.VMEM((GM_TM, GM_WIDTH), F32)],
        compiler_params=_params("arbitrary"),
        name="gmlp_mixer",
    )(h, g0, g1, w_in, ln_g, ln_b, w_s, b_s, w_out)


ATT_TJ = 512
ATT_D = ATT_HEADS * ATT_HEAD_DIM


def _t5_bucket(dist):
    d = jnp.maximum(dist, 1).astype(F32)
    large = REL_MAX_EXACT + (jnp.log(d / REL_MAX_EXACT) / math.log(REL_MAX_DIST / REL_MAX_EXACT)
                             * (REL_BUCKETS - REL_MAX_EXACT)).astype(jnp.int32)
    large = jnp.minimum(large, REL_BUCKETS - 1)
    return jnp.where(dist < REL_MAX_EXACT, dist, large)


def _bias_kernel(table_ref, bucket_ref, o_ref):
    g, hd = pl.program_id(0), pl.program_id(1)
    bucket = bucket_ref[...]
    acc = jnp.zeros(bucket.shape, F32)
    for b in range(REL_BUCKETS):
        acc = jnp.where(bucket == b, table_ref[b, g * ATT_HEADS + hd], acc)
    qi = lax.broadcasted_iota(jnp.int32, bucket.shape, 0)
    kj = lax.broadcasted_iota(jnp.int32, bucket.shape, 1)
    dist = ATT_BLK + qi - kj
    o_ref[...] = jnp.where((dist >= 0) & (dist <= ATT_BLK), acc, MASK_VALUE)


def _attention_bias(rel_bias):
    n_groups = len(DILATION_PAIRS)
    qi = jnp.arange(ATT_BLK)[:, None]
    kj = jnp.arange(2 * ATT_BLK)[None, :]
    dist_sub = jnp.maximum(ATT_BLK + qi - kj, 0)
    buckets = jnp.stack([_t5_bucket(dist_sub * r) for _, r in DILATION_PAIRS]).astype(jnp.int32)
    return pl.pallas_call(
        _bias_kernel,
        grid=(n_groups, ATT_HEADS),
        in_specs=[
            pl.BlockSpec(memory_space=pltpu.SMEM),
            pl.BlockSpec((None, ATT_BLK, 2 * ATT_BLK), lambda g, hd: (g, 0, 0)),
        ],
        out_specs=pl.BlockSpec((None, None, ATT_BLK, 2 * ATT_BLK), lambda g, hd: (g, hd, 0, 0)),
        out_shape=jax.ShapeDtypeStruct((n_groups, ATT_HEADS, ATT_BLK, 2 * ATT_BLK), F32),
        compiler_params=_params("arbitrary", "arbitrary"),
        name="attention_bias",
    )(rel_bias, buckets)


def _attn_group_kernel(h_ref, g0_ref, w_ref, bias_ref, o_ref, lse_ref, k_ref, v_ref, q_ref):
    t = pl.program_id(2)

    @pl.when(t == 0)
    def _():
        k_ref[:ATT_BLK, :] = jnp.zeros((ATT_BLK, ATT_D), BF16)
        v_ref[:ATT_BLK, :] = jnp.zeros((ATT_BLK, ATT_D), BF16)

    a = _rms(h_ref[...], g0_ref[...]).astype(BF16)
    q_ref[...] = (_dot(a, w_ref[:, :ATT_D]) * (ATT_HEAD_DIM ** -0.5)).astype(BF16)
    k_ref[ATT_BLK:, :] = _dot(a, w_ref[:, ATT_D:2 * ATT_D]).astype(BF16)
    v_ref[ATT_BLK:, :] = _dot(a, w_ref[:, 2 * ATT_D:]).astype(BF16)

    lane = lax.broadcasted_iota(jnp.int32, (ATT_BLK, V7X_LANES), 1)
    low = lane < ATT_HEAD_DIM
    kcol = lax.broadcasted_iota(jnp.int32, (ATT_BLK, 2 * ATT_BLK), 1)
    first_pen = jnp.where((kcol < ATT_BLK) & (t == 0), MASK_VALUE, 0.0)
    for blk in range(ATT_TJ // ATT_BLK):
        rows = slice(blk * ATT_BLK, (blk + 1) * ATT_BLK)
        krows = slice(blk * ATT_BLK, (blk + 2) * ATT_BLK)
        for hp in range(ATT_HEADS // 2):
            cols = slice(hp * V7X_LANES, (hp + 1) * V7X_LANES)
            qp = q_ref[rows, cols]
            kk = k_ref[krows, cols]
            vv = v_ref[krows, cols]
            outs, lses = [], []
            for sub in range(2):
                qm = jnp.where(low if sub == 0 else ~low, qp, jnp.zeros_like(qp))
                s = _dot_nt(qm, kk) + bias_ref[2 * hp + sub]
                if blk == 0:
                    s = s + first_pen
                m = jnp.max(s, axis=-1, keepdims=True)
                e = jnp.exp(s - m)
                den = jnp.sum(e, axis=-1, keepdims=True)
                outs.append(_dot(e.astype(BF16), vv) / den)
                lses.append(m + jnp.log(den))
            o_ref[rows, cols] = jnp.where(low, outs[0], outs[1])
            lse_ref[rows, cols] = jnp.where(low, lses[0], lses[1])
    k_ref[:ATT_BLK, :] = k_ref[ATT_TJ:, :]
    v_ref[:ATT_BLK, :] = v_ref[ATT_TJ:, :]


def _attention_group(h, g0, w_in, bias, group, dilation):
    b, s, d = h.shape
    sub = s // dilation
    hv = h.reshape(b, sub, dilation * d)
    tile = lambda bi, c, t: (bi, t, c)
    out_sds = jax.ShapeDtypeStruct((b, sub, dilation * ATT_D), F32)
    o, lse = pl.pallas_call(
        _attn_group_kernel,
        grid=(b, dilation, sub // ATT_TJ),
        in_specs=[
            pl.BlockSpec((None, ATT_TJ, d), tile),
            _resident((1, d)),
            pl.BlockSpec((d, 3 * ATT_D), lambda bi, c, t: (0, group), pipeline_mode=pl.Buffered(1)),
            pl.BlockSpec((None, ATT_HEADS, ATT_BLK, 2 * ATT_BLK), lambda bi, c, t: (group, 0, 0, 0),
                         pipeline_mode=pl.Buffered(1)),
        ],
        out_specs=[pl.BlockSpec((None, ATT_TJ, ATT_D), tile), pl.BlockSpec((None, ATT_TJ, ATT_D), tile)],
        out_shape=[out_sds, out_sds],
        scratch_shapes=[
            pltpu.VMEM((ATT_TJ + ATT_BLK, ATT_D), BF16),
            pltpu.VMEM((ATT_TJ + ATT_BLK, ATT_D), BF16),
            pltpu.VMEM((ATT_TJ, ATT_D), BF16),
        ],
        compiler_params=_params("arbitrary", "arbitrary", "arbitrary"),
        name=f"attention_group{group}",
    )(hv, g0, w_in, bias)
    return o.reshape(b * s, ATT_D), lse.reshape(b * s, ATT_D)


ATT_MERGE_TM = 512


def _attn_merge_kernel(h_ref, o0_ref, o1_ref, o2_ref, l0_ref, l1_ref, l2_ref, g1_ref, wout_ref, out_ref):
    l0, l1, l2 = l0_ref[...], l1_ref[...], l2_ref[...]
    m = jnp.maximum(jnp.maximum(l0, l1), l2)
    e0, e1, e2 = jnp.exp(l0 - m), jnp.exp(l1 - m), jnp.exp(l2 - m)
    o = (e0 * o0_ref[...] + e1 * o1_ref[...] + e2 * o2_ref[...]) / (e0 + e1 + e2)
    out_ref[...] = h_ref[...] + _rms(_dot(o.astype(BF16), wout_ref[...]), g1_ref[...])


def _attention_merge(h, outs, lses, g1, w_out):
    n, d = h.shape
    row = lambda i: (i, 0)
    tile = pl.BlockSpec((ATT_MERGE_TM, d), row)
    return pl.pallas_call(
        _attn_merge_kernel,
        grid=(n // ATT_MERGE_TM,),
        in_specs=[tile] * 7 + [_resident((1, d)), _resident(w_out.shape)],
        out_specs=tile,
        out_shape=jax.ShapeDtypeStruct((n, d), F32),
        compiler_params=_params("arbitrary"),
        name="attention_merge",
    )(h, *outs, *lses, g1, w_out)


def kernel(x, p, norm_g, ret_w_in, ret_w_out, attn_w_in, attn_w_out, rel_bias, gm_w_in, gm_ln_g, gm_ln_b,
           gm_w_s, gm_b_s, gm_w_out, ffn_w_in, ffn_w_out, ple_w_proj, ple_w_gate):
    b, s, d = x.shape
    n = b * s
    bf = lambda w: w.astype(BF16)
    gain = lambda i, k: norm_g[i, k].reshape(1, d)
    ret_consts = _retention_consts(s)

    h = x
    for i in range(DEPTH):
        kind, j = i % N_MIXERS, i // N_MIXERS
        if kind == 0:
            h = _retention_mixer(h.reshape(b, s, d), gain(i, 0), gain(i, 1), bf(ret_w_in[j]), bf(ret_w_out[j]),
                                 ret_consts)
        elif kind == 1:
            bias = _attention_bias(rel_bias)
            w_in = bf(attn_w_in[j])
            outs, lses = [], []
            for gi, (_, dil) in enumerate(DILATION_PAIRS):
                o, lse = _attention_group(h.reshape(b, s, d), gain(i, 0), w_in, bias, gi, dil)
                outs.append(o)
                lses.append(lse)
            h = _attention_merge(h.reshape(n, d), outs, lses, gain(i, 1), bf(attn_w_out[j]))
        else:
            h = _gmlp_mixer(h.reshape(n, d), gain(i, 0), gain(i, 1), bf(gm_w_in[j]),
                            gm_ln_g[j].reshape(1, GM_WIDTH), gm_ln_b[j].reshape(1, GM_WIDTH),
                            gm_w_s[j], gm_b_s[j].reshape(GM_GROUPS, GM_CHUNK, 1), bf(gm_w_out[j]))
        h = _ffn_ple(h.reshape(n, d), p[i].reshape(n, PLE_DIM), gain(i, 2), gain(i, 3), gain(i, 4),
                     bf(ffn_w_in[i]), bf(ffn_w_out[i]), bf(ple_w_proj[i]), bf(ple_w_gate[i]))
    return h.reshape(b, s, d)
```

```python
import functools
import math
from typing import NamedTuple

import jax
import jax.numpy as jnp
import numpy as np
from jax import lax
from jax.experimental import pallas as pl
from jax.experimental.pallas import tpu as pltpu

D_MODEL = 1024
DEPTH = 4
N_MIXERS = 3
N_NORMS = 5
PLE_DIM = 256
EPS = 1e-6

RET_HEADS = 4
RET_QK_DIM = 256
RET_V_DIM = 512
RET_CHUNK = 128
ROPE_BASE = 10000.0

ATT_HEADS = 16
ATT_HEAD_DIM = 64
DILATION_PAIRS = ((128, 1), (512, 4), (2048, 16))
ATT_BLK = 128
REL_BUCKETS = 32
REL_MAX_EXACT = 16
REL_MAX_DIST = 2048

GM_CHUNK = 128
GM_WIDTH = 2 * D_MODEL
GM_GROUPS = 8
GM_GROUP_DIM = GM_WIDTH // GM_GROUPS

FFN_HIDDEN = 2816

V7X_LANES = 128
V7X_SUBLANES = 8
V7X_VMEM_BYTES = 64 * 1024 * 1024
VMEM_LIMIT_BYTES = V7X_VMEM_BYTES - 8 * 1024 * 1024

MASK_VALUE = -1e30

BF16 = jnp.bfloat16
F32 = jnp.float32


def _resident(shape):
    return pl.BlockSpec(shape, lambda *_: (0,) * len(shape), pipeline_mode=pl.Buffered(1))


def _layer_resident(shape, layer):
    rest = tuple(shape[1:])
    return pl.BlockSpec((None,) + rest, lambda *_: (layer,) + (0,) * len(rest), pipeline_mode=pl.Buffered(1))


def _rms(x, g):
    return x * lax.rsqrt(jnp.mean(x * x, axis=-1, keepdims=True) + EPS) * g


def _dot(a, b):
    return jnp.dot(a, b, preferred_element_type=F32)


def _dot_nt(a, b):
    return lax.dot_general(a, b, (((1,), (1,)), ((), ())), preferred_element_type=F32)


def _dot_tn(a, b):
    return lax.dot_general(a, b, (((0,), (0,)), ((), ())), preferred_element_type=F32)


def _params(*semantics):
    return pltpu.CompilerParams(dimension_semantics=semantics, vmem_limit_bytes=VMEM_LIMIT_BYTES)


FFN_TM = 512


def _ffn_ple_kernel(h_ref, p_ref, g2_ref, g3_ref, g4_ref, win_ref, wout_ref, wp_ref, wg_ref, o_ref):
    x = h_ref[...]
    a = _rms(x, g2_ref[...]).astype(BF16)
    gate = _dot(a, win_ref[:, :FFN_HIDDEN])
    up = _dot(a, win_ref[:, FFN_HIDDEN:])
    hid = (gate * jax.nn.sigmoid(gate) * up).astype(BF16)
    h1 = x + _rms(_dot(hid, wout_ref[...]), g3_ref[...])
    emb = _dot(p_ref[...].astype(BF16), wp_ref[...])
    gate = jax.nn.sigmoid(_dot(_rms(h1, g4_ref[...]).astype(BF16), wg_ref[...]))
    o_ref[...] = h1 + gate * emb


def _ffn_ple(h, p, gains, w_in, w_out, w_proj, w_gate, layer):
    n, d = h.shape
    row = lambda i: (i, 0)
    gain = lambda k: _layer_resident(gains.shape, layer * N_NORMS + k)
    return pl.pallas_call(
        _ffn_ple_kernel,
        grid=(n // FFN_TM,),
        in_specs=[
            pl.BlockSpec((FFN_TM, d), row),
            pl.BlockSpec((None, FFN_TM, PLE_DIM), lambda i: (layer, i, 0)),
            gain(2), gain(3), gain(4),
            _layer_resident(w_in.shape, layer), _layer_resident(w_out.shape, layer),
            _layer_resident(w_proj.shape, layer), _layer_resident(w_gate.shape, layer),
        ],
        out_specs=pl.BlockSpec((FFN_TM, d), row),
        out_shape=jax.ShapeDtypeStruct((n, d), F32),
        compiler_params=_params("arbitrary"),
        name="ffn_ple",
    )(h, p, gains, gains, gains, w_in, w_out, w_proj, w_gate)


RET_TM = 256
RET_Q_OFF = 0
RET_K_OFF = RET_HEADS * RET_QK_DIM
RET_V_OFF = 2 * RET_HEADS * RET_QK_DIM
RET_G_OFF = RET_V_OFF + RET_HEADS * RET_V_DIM


def _rope(t, cos, sin):
    half = RET_QK_DIM // 2
    t1, t2 = t[:, :half], t[:, half:]
    return jnp.concatenate([t1 * cos - t2 * sin, t2 * cos + t1 * sin], axis=-1)


def _retention_kernel(h_ref, cos_ref, sin_ref, g0_ref, g1_ref, win_ref, wout_ref,
                      dmask_ref, qdec_ref, kdec_ref, cdec_ref, o_ref, state_ref, y_ref):
    @pl.when(pl.program_id(1) == 0)
    def _():
        state_ref[...] = jnp.zeros(state_ref.shape, F32)

    x = h_ref[...]
    a = _rms(x, g0_ref[...]).astype(BF16)
    cos, sin = cos_ref[...], sin_ref[...]
    for hd in range(RET_HEADS):
        qo, ko, vo = RET_Q_OFF + hd * RET_QK_DIM, RET_K_OFF + hd * RET_QK_DIM, RET_V_OFF + hd * RET_V_DIM
        q = _rope(_dot(a, win_ref[:, qo:qo + RET_QK_DIM]), cos, sin).astype(BF16)
        k = _rope(_dot(a, win_ref[:, ko:ko + RET_QK_DIM]), cos, sin) * (RET_QK_DIM ** -0.5)
        v = _dot(a, win_ref[:, vo:vo + RET_V_DIM]).astype(BF16)
        for c in range(RET_TM // RET_CHUNK):
            rows = slice(c * RET_CHUNK, (c + 1) * RET_CHUNK)
            qc, kc, vc = q[rows], k[rows], v[rows]
            st = state_ref[hd]
            scores = _dot_nt(qc, kc.astype(BF16)) * dmask_ref[hd]
            inner = _dot(scores.astype(BF16), vc)
            cross = _dot(qc, st.astype(BF16)) * qdec_ref[hd]
            state_ref[hd] = st * cdec_ref[hd] + _dot_tn((kc * kdec_ref[hd]).astype(BF16), vc)
            y = inner + cross
            y_ref[rows, hd * RET_V_DIM:(hd + 1) * RET_V_DIM] = y * lax.rsqrt(
                jnp.mean(y * y, axis=-1, keepdims=True) + EPS)
    gate = _dot(a, win_ref[:, RET_G_OFF:RET_G_OFF + RET_HEADS * RET_V_DIM])
    z = (gate * jax.nn.sigmoid(gate) * y_ref[...]).astype(BF16)
    o_ref[...] = x + _rms(_dot(z, wout_ref[...]), g1_ref[...])


def _retention_consts(seq_len):
    hh, c, dk = RET_HEADS, RET_CHUNK, RET_QK_DIM
    inv_freq = np.float32(ROPE_BASE) ** (-np.arange(0, dk, 2, dtype=np.float32) / np.float32(dk))
    ang = np.arange(seq_len, dtype=np.float32)[:, None] * inv_freq[None, :]
    cos = np.cos(ang.astype(np.float64)).astype(np.float32)
    sin = np.sin(ang.astype(np.float64)).astype(np.float32)
    log_gamma = jnp.log1p(-jnp.exp2(-5.0 - jnp.arange(hh, dtype=F32)))
    idx = jnp.arange(c, dtype=F32)
    diff = idx[:, None] - idx[None, :]
    dmask = jnp.where(diff >= 0, jnp.exp(log_gamma[:, None, None] * jnp.maximum(diff, 0.0)), 0.0)
    qdec = jnp.exp(log_gamma[:, None] * (idx[None, :] + 1.0))[:, :, None]
    kdec = jnp.exp(log_gamma[:, None] * (c - 1.0 - idx[None, :]))[:, :, None]
    cdec = jnp.exp(log_gamma * c)[:, None, None]
    return jnp.asarray(cos), jnp.asarray(sin), dmask, qdec, kdec, cdec


def _retention_mixer(h, gains, w_in, w_out, consts, layer, j):
    b, s, d = h.shape
    cos, sin, dmask, qdec, kdec, cdec = consts
    tile = lambda bi, t: (bi, t, 0)
    pos = lambda bi, t: (t, 0)
    gain = lambda k: _layer_resident(gains.shape, layer * N_NORMS + k)
    return pl.pallas_call(
        _retention_kernel,
        grid=(b, s // RET_TM),
        in_specs=[
            pl.BlockSpec((None, RET_TM, d), tile),
            pl.BlockSpec((RET_TM, RET_QK_DIM // 2), pos),
            pl.BlockSpec((RET_TM, RET_QK_DIM // 2), pos),
            gain(0), gain(1),
            _layer_resident(w_in.shape, j), _layer_resident(w_out.shape, j),
            _resident(dmask.shape), _resident(qdec.shape), _resident(kdec.shape), _resident(cdec.shape),
        ],
        out_specs=pl.BlockSpec((None, RET_TM, d), tile),
        out_shape=jax.ShapeDtypeStruct((b, s, d), F32),
        scratch_shapes=[
            pltpu.VMEM((RET_HEADS, RET_QK_DIM, RET_V_DIM), F32),
            pltpu.VMEM((RET_TM, RET_HEADS * RET_V_DIM), F32),
        ],
        compiler_params=_params("arbitrary", "arbitrary"),
        name="retention_mixer",
    )(h, cos, sin, gains, gains, w_in, w_out, dmask, qdec, kdec, cdec)


GM_TM = 256


def _gmlp_kernel(h_ref, g0_ref, g1_ref, win_ref, lng_ref, lnb_ref, ws_ref, bs_ref, wout_ref, o_ref, s_ref):
    x = h_ref[...]
    a = _rms(x, g0_ref[...]).astype(BF16)

    def gelu(t):
        return 0.5 * t * (1.0 + lax.erf(t * (2.0 ** -0.5)))

    v = gelu(_dot(a, win_ref[:, GM_WIDTH:]))
    mu = jnp.mean(v, axis=-1, keepdims=True)
    vc = v - mu
    var = jnp.mean(vc * vc, axis=-1, keepdims=True)
    vn = (vc * lax.rsqrt(var + EPS) * lng_ref[...] + lnb_ref[...]).astype(BF16)
    row = lax.broadcasted_iota(jnp.int32, (GM_CHUNK, GM_CHUNK), 0)
    col = lax.broadcasted_iota(jnp.int32, (GM_CHUNK, GM_CHUNK), 1)
    causal = row >= col
    for g in range(GM_GROUPS):
        w = jnp.where(causal, ws_ref[g], 0.0).astype(BF16)
        cols = slice(g * GM_GROUP_DIM, (g + 1) * GM_GROUP_DIM)
        for c in range(GM_TM // GM_CHUNK):
            rows = slice(c * GM_CHUNK, (c + 1) * GM_CHUNK)
            s_ref[rows, cols] = _dot(w, vn[rows, cols]) + bs_ref[g]
    u = gelu(_dot(a, win_ref[:, :GM_WIDTH]))
    z = (u * s_ref[...]).astype(BF16)
    o_ref[...] = x + _rms(_dot(z, wout_ref[...]), g1_ref[...])


def _gmlp_mixer(h, gains, w_in, ln_g, ln_b, w_s, b_s, w_out, layer, j):
    n, d = h.shape
    row = lambda i: (i, 0)
    gain = lambda k: _layer_resident(gains.shape, layer * N_NORMS + k)
    return pl.pallas_call(
        _gmlp_kernel,
        grid=(n // GM_TM,),
        in_specs=[
            pl.BlockSpec((GM_TM, d), row),
            gain(0), gain(1),
            _layer_resident(w_in.shape, j), _layer_resident(ln_g.shape, j), _layer_resident(ln_b.shape, j),
            _layer_resident(w_s.shape, j), _layer_resident(b_s.shape, j), _layer_resident(w_out.shape, j),
        ],
        out_specs=pl.BlockSpec((GM_TM, d), row),
        out_shape=jax.ShapeDtypeStruct((n, d), F32),
        scratch_shapes=[pltpu.VMEM((GM_TM, GM_WIDTH), F32)],
        compiler_params=_params("arbitrary"),
        name="gmlp_mixer",
    )(h, gains, gains, w_in, ln_g, ln_b, w_s, b_s, w_out)


ATT_D = ATT_HEADS * ATT_HEAD_DIM


class _GroupPlan(NamedTuple):
    dilation: int
    view_minor: int
    block_rows: int
    halves: int
    n_units: int
    n_slots: int
    pieces: tuple
    first_units: tuple
    positions: np.ndarray

    @property
    def tokens_per_step(self):
        return self.n_units * ATT_BLK


def _group_plan(dilation):
    if dilation == 1:
        return _GroupPlan(1, 0, 4 * ATT_BLK, 1, 4, 1,
                          tuple(((None, u * ATT_BLK, ATT_BLK),) for u in range(4)), (0,),
                          np.arange(ATT_BLK))
    if dilation == 4:
        half = ATT_BLK // 2
        return _GroupPlan(4, 8, half, 1, 4, 4,
                          tuple(((c, 0, half), (c + 4, 0, half)) for c in range(4)), (0, 1, 2, 3),
                          np.concatenate([2 * np.arange(half), 2 * np.arange(half) + 1]))
    assert dilation == 16
    return _GroupPlan(16, 16, ATT_BLK, 2, 8, 8,
                      tuple(((c, 0, ATT_BLK),) for c in range(8)), tuple(range(8)),
                      np.arange(ATT_BLK))


def _t5_bucket(dist):
    d = jnp.maximum(dist, 1).astype(F32)
    large = REL_MAX_EXACT + (jnp.log(d / REL_MAX_EXACT) / math.log(REL_MAX_DIST / REL_MAX_EXACT)
                             * (REL_BUCKETS - REL_MAX_EXACT)).astype(jnp.int32)
    large = jnp.minimum(large, REL_BUCKETS - 1)
    return jnp.where(dist < REL_MAX_EXACT, dist, large)


def _bias_kernel(table_ref, bucket_ref, dist_ref, o_ref):
    g, hd = pl.program_id(0), pl.program_id(1)
    bucket = bucket_ref[...]
    acc = jnp.zeros(bucket.shape, F32)
    for b in range(REL_BUCKETS):
        acc = jnp.where(bucket == b, table_ref[b, g * ATT_HEADS + hd], acc)
    dist = dist_ref[...]
    o_ref[...] = jnp.where((dist >= 0) & (dist <= ATT_BLK), acc, MASK_VALUE)


def _attention_bias(rel_bias, plans):
    dists = []
    for plan in plans:
        pos_k = np.concatenate([plan.positions - ATT_BLK, plan.positions])
        dists.append(plan.positions[:, None] - pos_k[None, :])
    dist = jnp.asarray(np.stack(dists), jnp.int32)
    scale = jnp.asarray([plan.dilation for plan in plans], jnp.int32)[:, None, None]
    buckets = _t5_bucket(jnp.maximum(dist, 0) * scale).astype(jnp.int32)
    tile = pl.BlockSpec((None, ATT_BLK, 2 * ATT_BLK), lambda g, hd: (g, 0, 0))
    return pl.pallas_call(
        _bias_kernel,
        grid=(len(plans), ATT_HEADS),
        in_specs=[pl.BlockSpec(memory_space=pltpu.SMEM), tile, tile],
        out_specs=pl.BlockSpec((None, None, ATT_BLK, 2 * ATT_BLK), lambda g, hd: (g, hd, 0, 0)),
        out_shape=jax.ShapeDtypeStruct((len(plans), ATT_HEADS, ATT_BLK, 2 * ATT_BLK), F32),
        compiler_params=_params("arbitrary", "arbitrary"),
        name="attention_bias",
    )(rel_bias, buckets, dist)


def _unit_load(ref, pieces):
    parts = [ref[r0:r0 + n, :] if sub is None else ref[r0:r0 + n, sub, :] for sub, r0, n in pieces]
    return parts[0] if len(parts) == 1 else jnp.concatenate(parts, axis=0)


def _unit_store(ref, pieces, cols, tile):
    at = 0
    for sub, r0, n in pieces:
        if sub is None:
            ref[r0:r0 + n, cols] = tile[at:at + n]
        else:
            ref[r0:r0 + n, sub, cols] = tile[at:at + n]
        at += n


def _attn_group_kernel(x_ref, g0_ref, w_ref, bias_ref, o_ref, lse_ref, k_ref, v_ref, q_ref, *, plan):
    t = pl.program_id(2)

    @pl.when(t == 0)
    def _():
        k_ref[:, :ATT_BLK, :] = jnp.zeros((plan.n_slots, ATT_BLK, ATT_D), BF16)
        v_ref[:, :ATT_BLK, :] = jnp.zeros((plan.n_slots, ATT_BLK, ATT_D), BF16)

    xa = jnp.concatenate([_unit_load(x_ref, pcs) for pcs in plan.pieces], axis=0)
    a = _rms(xa, g0_ref[...]).astype(BF16)
    q_ref[...] = (_dot(a, w_ref[:, :ATT_D]) * (ATT_HEAD_DIM ** -0.5)).astype(BF16)
    kc = _dot(a, w_ref[:, ATT_D:2 * ATT_D]).astype(BF16)
    vc = _dot(a, w_ref[:, 2 * ATT_D:]).astype(BF16)

    lane = lax.broadcasted_iota(jnp.int32, (ATT_BLK, V7X_LANES), 1)
    low = lane < ATT_HEAD_DIM
    kcol = lax.broadcasted_iota(jnp.int32, (ATT_BLK, 2 * ATT_BLK), 1)
    first_pen = jnp.where((kcol < ATT_BLK) & (t == 0), MASK_VALUE, 0.0)
    for u in range(plan.n_units):
        slot = u % plan.n_slots
        rows = slice(u * ATT_BLK, (u + 1) * ATT_BLK)
        k_ref[slot, ATT_BLK:, :] = kc[rows]
        v_ref[slot, ATT_BLK:, :] = vc[rows]
        lse_tile = jnp.zeros((ATT_BLK, V7X_LANES), F32)
        for hp in range(ATT_HEADS // 2):
            cols = slice(hp * V7X_LANES, (hp + 1) * V7X_LANES)
            qp = q_ref[rows, cols]
            kk = k_ref[slot, :, cols]
            vv = v_ref[slot, :, cols]
            outs, lses = [], []
            for sub in range(2):
                qm = jnp.where(low if sub == 0 else ~low, qp, jnp.zeros_like(qp))
                s = _dot_nt(qm, kk) + bias_ref[2 * hp + sub]
                if u in plan.first_units:
                    s = s + first_pen
                m = jnp.max(s, axis=-1, keepdims=True)
                e = jnp.exp(s - m)
                den = jnp.sum(e, axis=-1, keepdims=True)
                outs.append(_dot(e.astype(BF16), vv) / den)
                lses.append(m + jnp.log(den))
            _unit_store(o_ref, plan.pieces[u], cols, jnp.where(low, outs[0], outs[1]))
            lse_tile = jnp.where(lane == 2 * hp, lses[0], jnp.where(lane == 2 * hp + 1, lses[1], lse_tile))
        _unit_store(lse_ref, plan.pieces[u], slice(0, V7X_LANES), lse_tile)
        k_ref[slot, :ATT_BLK, :] = k_ref[slot, ATT_BLK:, :]
        v_ref[slot, :ATT_BLK, :] = v_ref[slot, ATT_BLK:, :]


def _attention_group(h, gains, w_in, bias, plan, group, layer, j):
    b, s, d = h.shape
    if plan.view_minor:
        lead = s // plan.view_minor
        view = lambda width: (b, lead, plan.view_minor, width)
        block = lambda width: pl.BlockSpec((None, plan.block_rows, V7X_SUBLANES, width),
                                           lambda bi, hf, t: (bi, t, hf, 0))
        steps = lead // plan.block_rows
    else:
        view = lambda width: (b, s, width)
        block = lambda width: pl.BlockSpec((None, plan.block_rows, width), lambda bi, hf, t: (bi, t, 0))
        steps = s // plan.block_rows
    o, lse = pl.pallas_call(
        functools.partial(_attn_group_kernel, plan=plan),
        grid=(b, plan.halves, steps),
        in_specs=[
            block(d),
            _layer_resident(gains.shape, layer * N_NORMS),
            pl.BlockSpec((None, d, 3 * ATT_D), lambda bi, hf, t: (j, 0, group), pipeline_mode=pl.Buffered(1)),
            pl.BlockSpec((None, ATT_HEADS, ATT_BLK, 2 * ATT_BLK), lambda bi, hf, t: (group, 0, 0, 0),
                         pipeline_mode=pl.Buffered(1)),
        ],
        out_specs=[block(ATT_D), block(V7X_LANES)],
        out_shape=[jax.ShapeDtypeStruct(view(ATT_D), F32), jax.ShapeDtypeStruct(view(V7X_LANES), F32)],
        scratch_shapes=[
            pltpu.VMEM((plan.n_slots, 2 * ATT_BLK, ATT_D), BF16),
            pltpu.VMEM((plan.n_slots, 2 * ATT_BLK, ATT_D), BF16),
            pltpu.VMEM((plan.tokens_per_step, ATT_D), BF16),
        ],
        compiler_params=_params("arbitrary", "arbitrary", "arbitrary"),
        name=f"attention_group{group}",
    )(h.reshape(view(d)), gains, w_in, bias)
    return o.reshape(b * s, ATT_D), lse.reshape(b * s, V7X_LANES)


ATT_MERGE_TM = 512


def _attn_merge_kernel(h_ref, o0_ref, o1_ref, o2_ref, l0_ref, l1_ref, l2_ref, g1_ref, wout_ref, expand_ref,
                       out_ref):
    l0, l1, l2 = l0_ref[...], l1_ref[...], l2_ref[...]
    m = jnp.maximum(jnp.maximum(l0, l1), l2)
    e0, e1, e2 = jnp.exp(l0 - m), jnp.exp(l1 - m), jnp.exp(l2 - m)
    inv = 1.0 / (e0 + e1 + e2)
    expand = expand_ref[...]

    def per_channel(w):
        hi = w.astype(BF16)
        lo = (w - hi.astype(F32)).astype(BF16)
        return _dot(hi, expand) + _dot(lo, expand)

    o = (per_channel(e0 * inv) * o0_ref[...] + per_channel(e1 * inv) * o1_ref[...]
         + per_channel(e2 * inv) * o2_ref[...])
    out_ref[...] = h_ref[...] + _rms(_dot(o.astype(BF16), wout_ref[...]), g1_ref[...])


def _attention_merge(h, outs, lses, gains, w_out, layer, j):
    n, d = h.shape
    row = lambda i: (i, 0)
    tile = pl.BlockSpec((ATT_MERGE_TM, d), row)
    lse_tile = pl.BlockSpec((ATT_MERGE_TM, V7X_LANES), row)
    expand = np.zeros((V7X_LANES, ATT_D), np.float32)
    for hd in range(ATT_HEADS):
        expand[hd, hd * ATT_HEAD_DIM:(hd + 1) * ATT_HEAD_DIM] = 1.0
    expand = jnp.asarray(expand, BF16)
    return pl.pallas_call(
        _attn_merge_kernel,
        grid=(n // ATT_MERGE_TM,),
        in_specs=[tile] * 4 + [lse_tile] * 3 + [
            _layer_resident(gains.shape, layer * N_NORMS + 1), _layer_resident(w_out.shape, j),
            _resident(expand.shape)],
        out_specs=tile,
        out_shape=jax.ShapeDtypeStruct((n, d), F32),
        compiler_params=_params("arbitrary"),
        name="attention_merge",
    )(h, *outs, *lses, gains, w_out, expand)


def kernel(x, p, norm_g, ret_w_in, ret_w_out, attn_w_in, attn_w_out, rel_bias, gm_w_in, gm_ln_g, gm_ln_b,
           gm_w_s, gm_b_s, gm_w_out, ffn_w_in, ffn_w_out, ple_w_proj, ple_w_gate):
    b, s, d = x.shape
    n = b * s
    bf = lambda w: w.astype(BF16)
    gains = norm_g.reshape(DEPTH * N_NORMS, 1, d)
    p = p.reshape(DEPTH, n, PLE_DIM)
    ret_w_in, ret_w_out = bf(ret_w_in), bf(ret_w_out)
    attn_w_in, attn_w_out = bf(attn_w_in), bf(attn_w_out)
    gm_w_in, gm_w_out = bf(gm_w_in), bf(gm_w_out)
    ffn_w_in, ffn_w_out, ple_w_proj, ple_w_gate = bf(ffn_w_in), bf(ffn_w_out), bf(ple_w_proj), bf(ple_w_gate)
    gm_ln_g = gm_ln_g.reshape(-1, 1, GM_WIDTH)
    gm_ln_b = gm_ln_b.reshape(-1, 1, GM_WIDTH)
    gm_b_s = gm_b_s.reshape(-1, GM_GROUPS, GM_CHUNK, 1)
    ret_consts = _retention_consts(s)
    plans = tuple(_group_plan(dil) for _, dil in DILATION_PAIRS)

    h = x
    for i in range(DEPTH):
        kind, j = i % N_MIXERS, i // N_MIXERS
        if kind == 0:
            h = _retention_mixer(h.reshape(b, s, d), gains, ret_w_in, ret_w_out, ret_consts, i, j)
        elif kind == 1:
            bias = _attention_bias(rel_bias, plans)
            outs, lses = [], []
            for gi, plan in enumerate(plans):
                o, lse = _attention_group(h.reshape(b, s, d), gains, attn_w_in, bias, plan, gi, i, j)
                outs.append(o)
                lses.append(lse)
            h = _attention_merge(h.reshape(n, d), outs, lses, gains, attn_w_out, i, j)
        else:
            h = _gmlp_mixer(h.reshape(n, d), gains, gm_w_in, gm_ln_g, gm_ln_b, gm_w_s, gm_b_s, gm_w_out, i, j)
        h = _ffn_ple(h.reshape(n, d), p, gains, ffn_w_in, ffn_w_out, ple_w_proj, ple_w_gate, i)
    return h.reshape(b, s, d)
```

```python
import functools
import math
from typing import NamedTuple

import jax
import jax.numpy as jnp
import numpy as np
from jax import lax
from jax.experimental import pallas as pl
from jax.experimental.pallas import tpu as pltpu

D_MODEL = 1024
DEPTH = 4
N_MIXERS = 3
N_NORMS = 5
PLE_DIM = 256
EPS = 1e-6

RET_HEADS = 4
RET_QK_DIM = 256
RET_V_DIM = 512
RET_CHUNK = 256
ROPE_BASE = 10000.0

ATT_HEADS = 16
ATT_HEAD_DIM = 64
DILATION_PAIRS = ((128, 1), (512, 4), (2048, 16))
ATT_BLK = 128
REL_BUCKETS = 32
REL_MAX_EXACT = 16
REL_MAX_DIST = 2048

GM_CHUNK = 128
GM_WIDTH = 2 * D_MODEL
GM_GROUPS = 8
GM_GROUP_DIM = GM_WIDTH // GM_GROUPS

FFN_HIDDEN = 2816

V7X_LANES = 128
V7X_SUBLANES = 8
V7X_VMEM_BYTES = 64 * 1024 * 1024
VMEM_LIMIT_BYTES = V7X_VMEM_BYTES - 8 * 1024 * 1024

MASK_VALUE = -1e30

BF16 = jnp.bfloat16
F32 = jnp.float32


def _resident(shape):
    return pl.BlockSpec(shape, lambda *_: (0,) * len(shape), pipeline_mode=pl.Buffered(1))


def _layer_resident(shape, layer):
    rest = tuple(shape[1:])
    return pl.BlockSpec((None,) + rest, lambda *_: (layer,) + (0,) * len(rest), pipeline_mode=pl.Buffered(1))


def _rms(x, g):
    return x * lax.rsqrt(jnp.mean(x * x, axis=-1, keepdims=True) + EPS) * g


def _dot(a, b):
    return jnp.dot(a, b, preferred_element_type=F32)


def _dot_nt(a, b):
    return lax.dot_general(a, b, (((1,), (1,)), ((), ())), preferred_element_type=F32)


def _dot_tn(a, b):
    return lax.dot_general(a, b, (((0,), (0,)), ((), ())), preferred_element_type=F32)


def _params(*semantics):
    return pltpu.CompilerParams(dimension_semantics=semantics, vmem_limit_bytes=VMEM_LIMIT_BYTES)


FFN_TM = 512


def _ffn_ple_kernel(h_ref, p_ref, g2_ref, g3_ref, g4_ref, win_ref, wout_ref, wp_ref, wg_ref, o_ref):
    x = h_ref[...]
    a = _rms(x, g2_ref[...]).astype(BF16)
    gate = _dot(a, win_ref[:, :FFN_HIDDEN])
    up = _dot(a, win_ref[:, FFN_HIDDEN:])
    hid = (gate * jax.nn.sigmoid(gate) * up).astype(BF16)
    h1 = x + _rms(_dot(hid, wout_ref[...]), g3_ref[...])
    emb = _dot(p_ref[...].astype(BF16), wp_ref[...])
    gate = jax.nn.sigmoid(_dot(_rms(h1, g4_ref[...]).astype(BF16), wg_ref[...]))
    o_ref[...] = h1 + gate * emb


def _ffn_ple(h, p, gains, w_in, w_out, w_proj, w_gate, layer):
    n, d = h.shape
    row = lambda i: (i, 0)
    gain = lambda k: _layer_resident(gains.shape, layer * N_NORMS + k)
    return pl.pallas_call(
        _ffn_ple_kernel,
        grid=(n // FFN_TM,),
        in_specs=[
            pl.BlockSpec((FFN_TM, d), row),
            pl.BlockSpec((None, FFN_TM, PLE_DIM), lambda i: (layer, i, 0)),
            gain(2), gain(3), gain(4),
            _layer_resident(w_in.shape, layer), _layer_resident(w_out.shape, layer),
            _layer_resident(w_proj.shape, layer), _layer_resident(w_gate.shape, layer),
        ],
        out_specs=pl.BlockSpec((FFN_TM, d), row),
        out_shape=jax.ShapeDtypeStruct((n, d), F32),
        compiler_params=_params("arbitrary"),
        name="ffn_ple",
    )(h, p, gains, gains, gains, w_in, w_out, w_proj, w_gate)


RET_TM = 512
RET_Q_OFF = 0
RET_K_OFF = RET_HEADS * RET_QK_DIM
RET_V_OFF = 2 * RET_HEADS * RET_QK_DIM
RET_G_OFF = RET_V_OFF + RET_HEADS * RET_V_DIM


def _rope(t, cos, sin):
    half = RET_QK_DIM // 2
    t1, t2 = t[:, :half], t[:, half:]
    return jnp.concatenate([t1 * cos - t2 * sin, t2 * cos + t1 * sin], axis=-1)


def _retention_kernel(h_ref, cos_ref, sin_ref, g0_ref, g1_ref, win_ref, wout_ref,
                      dmask_ref, qdec_ref, kdec_ref, cdec_ref, o_ref, state_ref, y_ref):
    @pl.when(pl.program_id(1) == 0)
    def _():
        state_ref[...] = jnp.zeros(state_ref.shape, F32)

    x = h_ref[...]
    a = _rms(x, g0_ref[...]).astype(BF16)
    cos, sin = cos_ref[...], sin_ref[...]
    for hd in range(RET_HEADS):
        qo, ko, vo = RET_Q_OFF + hd * RET_QK_DIM, RET_K_OFF + hd * RET_QK_DIM, RET_V_OFF + hd * RET_V_DIM
        q = _rope(_dot(a, win_ref[:, qo:qo + RET_QK_DIM]), cos, sin).astype(BF16)
        k = _rope(_dot(a, win_ref[:, ko:ko + RET_QK_DIM]), cos, sin) * (RET_QK_DIM ** -0.5)
        v = _dot(a, win_ref[:, vo:vo + RET_V_DIM]).astype(BF16)
        for c in range(RET_TM // RET_CHUNK):
            rows = slice(c * RET_CHUNK, (c + 1) * RET_CHUNK)
            qc, kc, vc = q[rows], k[rows], v[rows]
            st = state_ref[hd]
            scores = _dot_nt(qc, kc.astype(BF16)) * dmask_ref[hd]
            inner = _dot(scores.astype(BF16), vc)
            cross = _dot(qc, st.astype(BF16)) * qdec_ref[hd]
            state_ref[hd] = st * cdec_ref[hd] + _dot_tn((kc * kdec_ref[hd]).astype(BF16), vc)
            y = inner + cross
            y_ref[rows, hd * RET_V_DIM:(hd + 1) * RET_V_DIM] = y * lax.rsqrt(
                jnp.mean(y * y, axis=-1, keepdims=True) + EPS)
    gate = _dot(a, win_ref[:, RET_G_OFF:RET_G_OFF + RET_HEADS * RET_V_DIM])
    z = (gate * jax.nn.sigmoid(gate) * y_ref[...]).astype(BF16)
    o_ref[...] = x + _rms(_dot(z, wout_ref[...]), g1_ref[...])


def _retention_consts(seq_len):
    hh, c, dk = RET_HEADS, RET_CHUNK, RET_QK_DIM
    inv_freq = np.float32(ROPE_BASE) ** (-np.arange(0, dk, 2, dtype=np.float32) / np.float32(dk))
    ang = np.arange(seq_len, dtype=np.float32)[:, None] * inv_freq[None, :]
    cos = np.cos(ang.astype(np.float64)).astype(np.float32)
    sin = np.sin(ang.astype(np.float64)).astype(np.float32)
    log_gamma = jnp.log1p(-jnp.exp2(-5.0 - jnp.arange(hh, dtype=F32)))
    idx = jnp.arange(c, dtype=F32)
    diff = idx[:, None] - idx[None, :]
    dmask = jnp.where(diff >= 0, jnp.exp(log_gamma[:, None, None] * jnp.maximum(diff, 0.0)), 0.0)
    qdec = jnp.exp(log_gamma[:, None] * (idx[None, :] + 1.0))[:, :, None]
    kdec = jnp.exp(log_gamma[:, None] * (c - 1.0 - idx[None, :]))[:, :, None]
    cdec = jnp.exp(log_gamma * c)[:, None, None]
    return jnp.asarray(cos), jnp.asarray(sin), dmask, qdec, kdec, cdec


def _retention_mixer(h, gains, w_in, w_out, consts, layer, j):
    b, s, d = h.shape
    cos, sin, dmask, qdec, kdec, cdec = consts
    tile = lambda bi, t: (bi, t, 0)
    pos = lambda bi, t: (t, 0)
    gain = lambda k: _layer_resident(gains.shape, layer * N_NORMS + k)
    return pl.pallas_call(
        _retention_kernel,
        grid=(b, s // RET_TM),
        in_specs=[
            pl.BlockSpec((None, RET_TM, d), tile),
            pl.BlockSpec((RET_TM, RET_QK_DIM // 2), pos),
            pl.BlockSpec((RET_TM, RET_QK_DIM // 2), pos),
            gain(0), gain(1),
            _layer_resident(w_in.shape, j), _layer_resident(w_out.shape, j),
            _resident(dmask.shape), _resident(qdec.shape), _resident(kdec.shape), _resident(cdec.shape),
        ],
        out_specs=pl.BlockSpec((None, RET_TM, d), tile),
        out_shape=jax.ShapeDtypeStruct((b, s, d), F32),
        scratch_shapes=[
            pltpu.VMEM((RET_HEADS, RET_QK_DIM, RET_V_DIM), F32),
            pltpu.VMEM((RET_TM, RET_HEADS * RET_V_DIM), F32),
        ],
        compiler_params=_params("arbitrary", "arbitrary"),
        name="retention_mixer",
    )(h, cos, sin, gains, gains, w_in, w_out, dmask, qdec, kdec, cdec)


GM_TM = 512


def _gmlp_kernel(h_ref, g0_ref, g1_ref, win_ref, lng_ref, lnb_ref, ws_ref, bs_ref, wout_ref, o_ref, s_ref):
    x = h_ref[...]
    a = _rms(x, g0_ref[...]).astype(BF16)

    def gelu(t):
        return 0.5 * t * (1.0 + lax.erf(t * (2.0 ** -0.5)))

    v = gelu(_dot(a, win_ref[:, GM_WIDTH:]))
    mu = jnp.mean(v, axis=-1, keepdims=True)
    vc = v - mu
    var = jnp.mean(vc * vc, axis=-1, keepdims=True)
    vn = (vc * lax.rsqrt(var + EPS) * lng_ref[...] + lnb_ref[...]).astype(BF16)
    row = lax.broadcasted_iota(jnp.int32, (GM_CHUNK, GM_CHUNK), 0)
    col = lax.broadcasted_iota(jnp.int32, (GM_CHUNK, GM_CHUNK), 1)
    causal = row >= col
    for g in range(GM_GROUPS):
        w = jnp.where(causal, ws_ref[g], 0.0).astype(BF16)
        cols = slice(g * GM_GROUP_DIM, (g + 1) * GM_GROUP_DIM)
        for c in range(GM_TM // GM_CHUNK):
            rows = slice(c * GM_CHUNK, (c + 1) * GM_CHUNK)
            s_ref[rows, cols] = _dot(w, vn[rows, cols]) + bs_ref[g]
    u = gelu(_dot(a, win_ref[:, :GM_WIDTH]))
    z = (u * s_ref[...]).astype(BF16)
    o_ref[...] = x + _rms(_dot(z, wout_ref[...]), g1_ref[...])


def _gmlp_mixer(h, gains, w_in, ln_g, ln_b, w_s, b_s, w_out, layer, j):
    n, d = h.shape
    row = lambda i: (i, 0)
    gain = lambda k: _layer_resident(gains.shape, layer * N_NORMS + k)
    return pl.pallas_call(
        _gmlp_kernel,
        grid=(n // GM_TM,),
        in_specs=[
            pl.BlockSpec((GM_TM, d), row),
            gain(0), gain(1),
            _layer_resident(w_in.shape, j), _layer_resident(ln_g.shape, j), _layer_resident(ln_b.shape, j),
            _layer_resident(w_s.shape, j), _layer_resident(b_s.shape, j), _layer_resident(w_out.shape, j),
        ],
        out_specs=pl.BlockSpec((GM_TM, d), row),
        out_shape=jax.ShapeDtypeStruct((n, d), F32),
        scratch_shapes=[pltpu.VMEM((GM_TM, GM_WIDTH), F32)],
        compiler_params=_params("arbitrary"),
        name="gmlp_mixer",
    )(h, gains, gains, w_in, ln_g, ln_b, w_s, b_s, w_out)


ATT_D = ATT_HEADS * ATT_HEAD_DIM


class _GroupPlan(NamedTuple):
    dilation: int
    view_minor: int
    block_rows: int
    halves: int
    n_units: int
    n_slots: int
    pieces: tuple
    first_units: tuple
    positions: np.ndarray

    @property
    def tokens_per_step(self):
        return self.n_units * ATT_BLK


def _group_plan(dilation):
    if dilation == 1:
        return _GroupPlan(1, 0, 4 * ATT_BLK, 1, 4, 1,
                          tuple(((None, u * ATT_BLK, ATT_BLK),) for u in range(4)), (0,),
                          np.arange(ATT_BLK))
    if dilation == 4:
        half = ATT_BLK // 2
        return _GroupPlan(4, 8, half, 1, 4, 4,
                          tuple(((c, 0, half), (c + 4, 0, half)) for c in range(4)), (0, 1, 2, 3),
                          np.concatenate([2 * np.arange(half), 2 * np.arange(half) + 1]))
    assert dilation == 16
    return _GroupPlan(16, 16, ATT_BLK, 2, 8, 8,
                      tuple(((c, 0, ATT_BLK),) for c in range(8)), tuple(range(8)),
                      np.arange(ATT_BLK))


def _t5_bucket(dist):
    d = jnp.maximum(dist, 1).astype(F32)
    large = REL_MAX_EXACT + (jnp.log(d / REL_MAX_EXACT) / math.log(REL_MAX_DIST / REL_MAX_EXACT)
                             * (REL_BUCKETS - REL_MAX_EXACT)).astype(jnp.int32)
    large = jnp.minimum(large, REL_BUCKETS - 1)
    return jnp.where(dist < REL_MAX_EXACT, dist, large)


def _bias_kernel(table_ref, bucket_ref, dist_ref, o_ref):
    g, hd = pl.program_id(0), pl.program_id(1)
    bucket = bucket_ref[...]
    acc = jnp.zeros(bucket.shape, F32)
    for b in range(REL_BUCKETS):
        acc = jnp.where(bucket == b, table_ref[b, g * ATT_HEADS + hd], acc)
    dist = dist_ref[...]
    o_ref[...] = jnp.where((dist >= 0) & (dist <= ATT_BLK), acc, MASK_VALUE)


def _attention_bias(rel_bias, plans):
    dists = []
    for plan in plans:
        pos_k = np.concatenate([plan.positions - ATT_BLK, plan.positions])
        dists.append(plan.positions[:, None] - pos_k[None, :])
    dist = jnp.asarray(np.stack(dists), jnp.int32)
    scale = jnp.asarray([plan.dilation for plan in plans], jnp.int32)[:, None, None]
    buckets = _t5_bucket(jnp.maximum(dist, 0) * scale).astype(jnp.int32)
    tile = pl.BlockSpec((None, ATT_BLK, 2 * ATT_BLK), lambda g, hd: (g, 0, 0))
    return pl.pallas_call(
        _bias_kernel,
        grid=(len(plans), ATT_HEADS),
        in_specs=[pl.BlockSpec(memory_space=pltpu.SMEM), tile, tile],
        out_specs=pl.BlockSpec((None, None, ATT_BLK, 2 * ATT_BLK), lambda g, hd: (g, hd, 0, 0)),
        out_shape=jax.ShapeDtypeStruct((len(plans), ATT_HEADS, ATT_BLK, 2 * ATT_BLK), F32),
        compiler_params=_params("arbitrary", "arbitrary"),
        name="attention_bias",
    )(rel_bias, buckets, dist)


def _unit_load(ref, pieces):
    parts = [ref[r0:r0 + n, :] if sub is None else ref[r0:r0 + n, sub, :] for sub, r0, n in pieces]
    return parts[0] if len(parts) == 1 else jnp.concatenate(parts, axis=0)


def _unit_store(ref, pieces, cols, tile):
    at = 0
    for sub, r0, n in pieces:
        if sub is None:
            ref[r0:r0 + n, cols] = tile[at:at + n]
        else:
            ref[r0:r0 + n, sub, cols] = tile[at:at + n]
        at += n


def _attn_group_kernel(x_ref, g0_ref, w_ref, bias_ref, o_ref, lse_ref, k_ref, v_ref, q_ref, *, plan):
    t = pl.program_id(2)

    @pl.when(t == 0)
    def _():
        k_ref[:, :ATT_BLK, :] = jnp.zeros((plan.n_slots, ATT_BLK, ATT_D), BF16)
        v_ref[:, :ATT_BLK, :] = jnp.zeros((plan.n_slots, ATT_BLK, ATT_D), BF16)

    xa = jnp.concatenate([_unit_load(x_ref, pcs) for pcs in plan.pieces], axis=0)
    a = _rms(xa, g0_ref[...]).astype(BF16)
    q_ref[...] = (_dot(a, w_ref[:, :ATT_D]) * (ATT_HEAD_DIM ** -0.5)).astype(BF16)
    kc = _dot(a, w_ref[:, ATT_D:2 * ATT_D]).astype(BF16)
    vc = _dot(a, w_ref[:, 2 * ATT_D:]).astype(BF16)

    lane = lax.broadcasted_iota(jnp.int32, (ATT_BLK, V7X_LANES), 1)
    low = lane < ATT_HEAD_DIM
    kcol = lax.broadcasted_iota(jnp.int32, (ATT_BLK, 2 * ATT_BLK), 1)
    first_pen = jnp.where((kcol < ATT_BLK) & (t == 0), MASK_VALUE, 0.0)
    for u in range(plan.n_units):
        slot = u % plan.n_slots
        rows = slice(u * ATT_BLK, (u + 1) * ATT_BLK)
        k_ref[slot, ATT_BLK:, :] = kc[rows]
        v_ref[slot, ATT_BLK:, :] = vc[rows]
        lse_tile = jnp.zeros((ATT_BLK, V7X_LANES), F32)
        for hp in range(ATT_HEADS // 2):
            cols = slice(hp * V7X_LANES, (hp + 1) * V7X_LANES)
            qp = q_ref[rows, cols]
            kk = k_ref[slot, :, cols]
            vv = v_ref[slot, :, cols]
            outs, lses = [], []
            for sub in range(2):
                qm = jnp.where(low if sub == 0 else ~low, qp, jnp.zeros_like(qp))
                s = _dot_nt(qm, kk) + bias_ref[2 * hp + sub]
                if u in plan.first_units:
                    s = s + first_pen
                m = jnp.max(s, axis=-1, keepdims=True)
                e = jnp.exp(s - m)
                den = jnp.sum(e, axis=-1, keepdims=True)
                outs.append(_dot(e.astype(BF16), vv) / den)
                lses.append(m + jnp.log(den))
            _unit_store(o_ref, plan.pieces[u], cols, jnp.where(low, outs[0], outs[1]))
            lse_tile = jnp.where(lane == 2 * hp, lses[0], jnp.where(lane == 2 * hp + 1, lses[1], lse_tile))
        _unit_store(lse_ref, plan.pieces[u], slice(0, V7X_LANES), lse_tile)
        k_ref[slot, :ATT_BLK, :] = k_ref[slot, ATT_BLK:, :]
        v_ref[slot, :ATT_BLK, :] = v_ref[slot, ATT_BLK:, :]


def _attention_group(h, gains, w_in, bias, plan, group, layer, j):
    b, s, d = h.shape
    if plan.view_minor:
        lead = s // plan.view_minor
        view = lambda width: (b, lead, plan.view_minor, width)
        block = lambda width: pl.BlockSpec((None, plan.block_rows, V7X_SUBLANES, width),
                                           lambda bi, hf, t: (bi, t, hf, 0))
        steps = lead // plan.block_rows
    else:
        view = lambda width: (b, s, width)
        block = lambda width: pl.BlockSpec((None, plan.block_rows, width), lambda bi, hf, t: (bi, t, 0))
        steps = s // plan.block_rows
    o, lse = pl.pallas_call(
        functools.partial(_attn_group_kernel, plan=plan),
        grid=(b, plan.halves, steps),
        in_specs=[
            block(d),
            _layer_resident(gains.shape, layer * N_NORMS),
            pl.BlockSpec((None, d, 3 * ATT_D), lambda bi, hf, t: (j, 0, group), pipeline_mode=pl.Buffered(1)),
            pl.BlockSpec((None, ATT_HEADS, ATT_BLK, 2 * ATT_BLK), lambda bi, hf, t: (group, 0, 0, 0),
                         pipeline_mode=pl.Buffered(1)),
        ],
        out_specs=[block(ATT_D), block(V7X_LANES)],
        out_shape=[jax.ShapeDtypeStruct(view(ATT_D), F32), jax.ShapeDtypeStruct(view(V7X_LANES), F32)],
        scratch_shapes=[
            pltpu.VMEM((plan.n_slots, 2 * ATT_BLK, ATT_D), BF16),
            pltpu.VMEM((plan.n_slots, 2 * ATT_BLK, ATT_D), BF16),
            pltpu.VMEM((plan.tokens_per_step, ATT_D), BF16),
        ],
        compiler_params=_params("arbitrary", "arbitrary", "arbitrary"),
        name=f"attention_group{group}",
    )(h.reshape(view(d)), gains, w_in, bias)
    return o.reshape(b * s, ATT_D), lse.reshape(b * s, V7X_LANES)


ATT_MERGE_TM = 512


def _attn_merge_kernel(h_ref, o0_ref, o1_ref, o2_ref, l0_ref, l1_ref, l2_ref, g1_ref, wout_ref, expand_ref,
                       out_ref):
    l0, l1, l2 = l0_ref[...], l1_ref[...], l2_ref[...]
    m = jnp.maximum(jnp.maximum(l0, l1), l2)
    e0, e1, e2 = jnp.exp(l0 - m), jnp.exp(l1 - m), jnp.exp(l2 - m)
    inv = 1.0 / (e0 + e1 + e2)
    expand = expand_ref[...]

    def per_channel(w):
        hi = w.astype(BF16)
        lo = (w - hi.astype(F32)).astype(BF16)
        return _dot(hi, expand) + _dot(lo, expand)

    o = (per_channel(e0 * inv) * o0_ref[...] + per_channel(e1 * inv) * o1_ref[...]
         + per_channel(e2 * inv) * o2_ref[...])
    out_ref[...] = h_ref[...] + _rms(_dot(o.astype(BF16), wout_ref[...]), g1_ref[...])


def _attention_merge(h, outs, lses, gains, w_out, layer, j):
    n, d = h.shape
    row = lambda i: (i, 0)
    tile = pl.BlockSpec((ATT_MERGE_TM, d), row)
    lse_tile = pl.BlockSpec((ATT_MERGE_TM, V7X_LANES), row)
    expand = np.zeros((V7X_LANES, ATT_D), np.float32)
    for hd in range(ATT_HEADS):
        expand[hd, hd * ATT_HEAD_DIM:(hd + 1) * ATT_HEAD_DIM] = 1.0
    expand = jnp.asarray(expand, BF16)
    return pl.pallas_call(
        _attn_merge_kernel,
        grid=(n // ATT_MERGE_TM,),
        in_specs=[tile] * 4 + [lse_tile] * 3 + [
            _layer_resident(gains.shape, layer * N_NORMS + 1), _layer_resident(w_out.shape, j),
            _resident(expand.shape)],
        out_specs=tile,
        out_shape=jax.ShapeDtypeStruct((n, d), F32),
        compiler_params=_params("arbitrary"),
        name="attention_merge",
    )(h, *outs, *lses, gains, w_out, expand)


def kernel(x, p, norm_g, ret_w_in, ret_w_out, attn_w_in, attn_w_out, rel_bias, gm_w_in, gm_ln_g, gm_ln_b,
           gm_w_s, gm_b_s, gm_w_out, ffn_w_in, ffn_w_out, ple_w_proj, ple_w_gate):
    b, s, d = x.shape
    n = b * s
    bf = lambda w: w.astype(BF16)
    gains = norm_g.reshape(DEPTH * N_NORMS, 1, d)
    p = p.reshape(DEPTH, n, PLE_DIM)
    ret_w_in, ret_w_out = bf(ret_w_in), bf(ret_w_out)
    attn_w_in, attn_w_out = bf(attn_w_in), bf(attn_w_out)
    gm_w_in, gm_w_out = bf(gm_w_in), bf(gm_w_out)
    ffn_w_in, ffn_w_out, ple_w_proj, ple_w_gate = bf(ffn_w_in), bf(ffn_w_out), bf(ple_w_proj), bf(ple_w_gate)
    gm_ln_g = gm_ln_g.reshape(-1, 1, GM_WIDTH)
    gm_ln_b = gm_ln_b.reshape(-1, 1, GM_WIDTH)
    gm_b_s = gm_b_s.reshape(-1, GM_GROUPS, GM_CHUNK, 1)
    ret_consts = _retention_consts(s)
    plans = tuple(_group_plan(dil) for _, dil in DILATION_PAIRS)

    h = x
    for i in range(DEPTH):
        kind, j = i % N_MIXERS, i // N_MIXERS
        if kind == 0:
            h = _retention_mixer(h.reshape(b, s, d), gains, ret_w_in, ret_w_out, ret_consts, i, j)
        elif kind == 1:
            bias = _attention_bias(rel_bias, plans)
            outs, lses = [], []
            for gi, plan in enumerate(plans):
                o, lse = _attention_group(h.reshape(b, s, d), gains, attn_w_in, bias, plan, gi, i, j)
                outs.append(o)
                lses.append(lse)
            h = _attention_merge(h.reshape(n, d), outs, lses, gains, attn_w_out, i, j)
        else:
            h = _gmlp_mixer(h.reshape(n, d), gains, gm_w_in, gm_ln_g, gm_ln_b, gm_w_s, gm_b_s, gm_w_out, i, j)
        h = _ffn_ple(h.reshape(n, d), p, gains, ffn_w_in, ffn_w_out, ple_w_proj, ple_w_gate, i)
    return h.reshape(b, s, d)
```

```python
import functools
import math
from typing import NamedTuple

import jax
import jax.numpy as jnp
import numpy as np
from jax import lax
from jax.experimental import pallas as pl
from jax.experimental.pallas import tpu as pltpu

D_MODEL = 1024
DEPTH = 4
N_MIXERS = 3
N_NORMS = 5
PLE_DIM = 256
EPS = 1e-6

RET_HEADS = 4
RET_QK_DIM = 256
RET_V_DIM = 512
RET_CHUNK = 256
ROPE_BASE = 10000.0

ATT_HEADS = 16
ATT_HEAD_DIM = 64
DILATION_PAIRS = ((128, 1), (512, 4), (2048, 16))
ATT_BLK = 128
REL_BUCKETS = 32
REL_MAX_EXACT = 16
REL_MAX_DIST = 2048

GM_CHUNK = 128
GM_WIDTH = 2 * D_MODEL
GM_GROUPS = 8
GM_GROUP_DIM = GM_WIDTH // GM_GROUPS

FFN_HIDDEN = 2816

V7X_LANES = 128
V7X_SUBLANES = 8
V7X_VMEM_BYTES = 64 * 1024 * 1024
VMEM_LIMIT_BYTES = V7X_VMEM_BYTES - 8 * 1024 * 1024

MASK_VALUE = -1e30
LOG2E = math.log2(math.e)
LN2 = math.log(2.0)

BF16 = jnp.bfloat16
F32 = jnp.float32


def _resident(shape):
    return pl.BlockSpec(shape, lambda *_: (0,) * len(shape), pipeline_mode=pl.Buffered(1))


def _layer_resident(shape, layer):
    rest = tuple(shape[1:])
    return pl.BlockSpec((None,) + rest, lambda *_: (layer,) + (0,) * len(rest), pipeline_mode=pl.Buffered(1))


def _rms(x, g):
    return x * lax.rsqrt(jnp.mean(x * x, axis=-1, keepdims=True) + EPS) * g


def _dot(a, b):
    return jnp.dot(a, b, preferred_element_type=F32)


def _dot_nt(a, b):
    return lax.dot_general(a, b, (((1,), (1,)), ((), ())), preferred_element_type=F32)


def _dot_tn(a, b):
    return lax.dot_general(a, b, (((0,), (0,)), ((), ())), preferred_element_type=F32)


def _params(*semantics):
    return pltpu.CompilerParams(dimension_semantics=semantics, vmem_limit_bytes=VMEM_LIMIT_BYTES)


FFN_TM = 1024
FFN_SUB = 512


def _ffn_ple_kernel(h_ref, p_ref, g2_ref, g3_ref, g4_ref, win_ref, wout_ref, wp_ref, wg_ref, o_ref):
    for part in range(FFN_TM // FFN_SUB):
        rows = slice(part * FFN_SUB, (part + 1) * FFN_SUB)
        x = h_ref[rows, :]
        a = _rms(x, g2_ref[...]).astype(BF16)
        gate = _dot(a, win_ref[:, :FFN_HIDDEN])
        up = _dot(a, win_ref[:, FFN_HIDDEN:])
        hid = (gate * jax.nn.sigmoid(gate) * up).astype(BF16)
        h1 = x + _rms(_dot(hid, wout_ref[...]), g3_ref[...])
        emb = _dot(p_ref[rows, :].astype(BF16), wp_ref[...])
        gate = jax.nn.sigmoid(_dot(_rms(h1, g4_ref[...]).astype(BF16), wg_ref[...]))
        o_ref[rows, :] = h1 + gate * emb


def _ffn_ple(h, p, gains, w_in, w_out, w_proj, w_gate, layer):
    n, d = h.shape
    row = lambda i: (i, 0)
    gain = lambda k: _layer_resident(gains.shape, layer * N_NORMS + k)
    return pl.pallas_call(
        _ffn_ple_kernel,
        grid=(n // FFN_TM,),
        in_specs=[
            pl.BlockSpec((FFN_TM, d), row),
            pl.BlockSpec((None, FFN_TM, PLE_DIM), lambda i: (layer, i, 0)),
            gain(2), gain(3), gain(4),
            _layer_resident(w_in.shape, layer), _layer_resident(w_out.shape, layer),
            _layer_resident(w_proj.shape, layer), _layer_resident(w_gate.shape, layer),
        ],
        out_specs=pl.BlockSpec((FFN_TM, d), row),
        out_shape=jax.ShapeDtypeStruct((n, d), F32),
        compiler_params=_params("arbitrary"),
        name="ffn_ple",
    )(h, p, gains, gains, gains, w_in, w_out, w_proj, w_gate)


RET_TM = 512
RET_Q_OFF = 0
RET_K_OFF = RET_HEADS * RET_QK_DIM
RET_V_OFF = 2 * RET_HEADS * RET_QK_DIM
RET_G_OFF = RET_V_OFF + RET_HEADS * RET_V_DIM


def _rope(t, cos, sin):
    half = RET_QK_DIM // 2
    t1, t2 = t[:, :half], t[:, half:]
    return jnp.concatenate([t1 * cos - t2 * sin, t2 * cos + t1 * sin], axis=-1)


def _retention_kernel(h_ref, cos_ref, sin_ref, g0_ref, g1_ref, win_ref, wout_ref,
                      dmask_ref, qdec_ref, kdec_ref, cdec_ref, o_ref, state_ref, y_ref):
    @pl.when(pl.program_id(1) == 0)
    def _():
        state_ref[...] = jnp.zeros(state_ref.shape, F32)

    x = h_ref[...]
    a = _rms(x, g0_ref[...]).astype(BF16)
    cos, sin = cos_ref[...], sin_ref[...]
    for hd in range(RET_HEADS):
        qo, ko, vo = RET_Q_OFF + hd * RET_QK_DIM, RET_K_OFF + hd * RET_QK_DIM, RET_V_OFF + hd * RET_V_DIM
        q = _rope(_dot(a, win_ref[:, qo:qo + RET_QK_DIM]), cos, sin).astype(BF16)
        k = _rope(_dot(a, win_ref[:, ko:ko + RET_QK_DIM]), cos, sin) * (RET_QK_DIM ** -0.5)
        v = _dot(a, win_ref[:, vo:vo + RET_V_DIM]).astype(BF16)
        for c in range(RET_TM // RET_CHUNK):
            rows = slice(c * RET_CHUNK, (c + 1) * RET_CHUNK)
            qc, kc, vc = q[rows], k[rows], v[rows]
            st = state_ref[hd]
            scores = _dot_nt(qc, kc.astype(BF16)) * dmask_ref[hd]
            inner = _dot(scores.astype(BF16), vc)
            cross = _dot(qc, st.astype(BF16)) * qdec_ref[hd]
            state_ref[hd] = st * cdec_ref[hd] + _dot_tn((kc * kdec_ref[hd]).astype(BF16), vc)
            y = inner + cross
            y_ref[rows, hd * RET_V_DIM:(hd + 1) * RET_V_DIM] = y * lax.rsqrt(
                jnp.mean(y * y, axis=-1, keepdims=True) + EPS)
    gate = _dot(a, win_ref[:, RET_G_OFF:RET_G_OFF + RET_HEADS * RET_V_DIM])
    z = (gate * jax.nn.sigmoid(gate) * y_ref[...]).astype(BF16)
    o_ref[...] = x + _rms(_dot(z, wout_ref[...]), g1_ref[...])


def _retention_consts(seq_len):
    hh, c, dk = RET_HEADS, RET_CHUNK, RET_QK_DIM
    inv_freq = np.float32(ROPE_BASE) ** (-np.arange(0, dk, 2, dtype=np.float32) / np.float32(dk))
    ang = np.arange(seq_len, dtype=np.float32)[:, None] * inv_freq[None, :]
    cos = np.cos(ang.astype(np.float64)).astype(np.float32)
    sin = np.sin(ang.astype(np.float64)).astype(np.float32)
    log_gamma = jnp.log1p(-jnp.exp2(-5.0 - jnp.arange(hh, dtype=F32)))
    idx = jnp.arange(c, dtype=F32)
    diff = idx[:, None] - idx[None, :]
    dmask = jnp.where(diff >= 0, jnp.exp(log_gamma[:, None, None] * jnp.maximum(diff, 0.0)), 0.0)
    qdec = jnp.exp(log_gamma[:, None] * (idx[None, :] + 1.0))[:, :, None]
    kdec = jnp.exp(log_gamma[:, None] * (c - 1.0 - idx[None, :]))[:, :, None]
    cdec = jnp.exp(log_gamma * c)[:, None, None]
    return jnp.asarray(cos), jnp.asarray(sin), dmask, qdec, kdec, cdec


def _retention_mixer(h, gains, w_in, w_out, consts, layer, j):
    b, s, d = h.shape
    cos, sin, dmask, qdec, kdec, cdec = consts
    tile = lambda bi, t: (bi, t, 0)
    pos = lambda bi, t: (t, 0)
    gain = lambda k: _layer_resident(gains.shape, layer * N_NORMS + k)
    return pl.pallas_call(
        _retention_kernel,
        grid=(b, s // RET_TM),
        in_specs=[
            pl.BlockSpec((None, RET_TM, d), tile),
            pl.BlockSpec((RET_TM, RET_QK_DIM // 2), pos),
            pl.BlockSpec((RET_TM, RET_QK_DIM // 2), pos),
            gain(0), gain(1),
            _layer_resident(w_in.shape, j), _layer_resident(w_out.shape, j),
            _resident(dmask.shape), _resident(qdec.shape), _resident(kdec.shape), _resident(cdec.shape),
        ],
        out_specs=pl.BlockSpec((None, RET_TM, d), tile),
        out_shape=jax.ShapeDtypeStruct((b, s, d), F32),
        scratch_shapes=[
            pltpu.VMEM((RET_HEADS, RET_QK_DIM, RET_V_DIM), F32),
            pltpu.VMEM((RET_TM, RET_HEADS * RET_V_DIM), F32),
        ],
        compiler_params=_params("arbitrary", "arbitrary"),
        name="retention_mixer",
    )(h, cos, sin, gains, gains, w_in, w_out, dmask, qdec, kdec, cdec)


GM_TM = 512


def _gmlp_kernel(h_ref, g0_ref, g1_ref, win_ref, lng_ref, lnb_ref, ws_ref, bs_ref, wout_ref, o_ref, s_ref):
    x = h_ref[...]
    a = _rms(x, g0_ref[...]).astype(BF16)

    def gelu(t):
        return 0.5 * t * (1.0 + lax.erf(t * (2.0 ** -0.5)))

    v = gelu(_dot(a, win_ref[:, GM_WIDTH:]))
    mu = jnp.mean(v, axis=-1, keepdims=True)
    vc = v - mu
    var = jnp.mean(vc * vc, axis=-1, keepdims=True)
    vn = (vc * lax.rsqrt(var + EPS) * lng_ref[...] + lnb_ref[...]).astype(BF16)
    row = lax.broadcasted_iota(jnp.int32, (GM_CHUNK, GM_CHUNK), 0)
    col = lax.broadcasted_iota(jnp.int32, (GM_CHUNK, GM_CHUNK), 1)
    causal = row >= col
    for g in range(GM_GROUPS):
        w = jnp.where(causal, ws_ref[g], 0.0).astype(BF16)
        cols = slice(g * GM_GROUP_DIM, (g + 1) * GM_GROUP_DIM)
        for c in range(GM_TM // GM_CHUNK):
            rows = slice(c * GM_CHUNK, (c + 1) * GM_CHUNK)
            s_ref[rows, cols] = _dot(w, vn[rows, cols]) + bs_ref[g]
    u = gelu(_dot(a, win_ref[:, :GM_WIDTH]))
    z = (u * s_ref[...]).astype(BF16)
    o_ref[...] = x + _rms(_dot(z, wout_ref[...]), g1_ref[...])


def _gmlp_mixer(h, gains, w_in, ln_g, ln_b, w_s, b_s, w_out, layer, j):
    n, d = h.shape
    row = lambda i: (i, 0)
    gain = lambda k: _layer_resident(gains.shape, layer * N_NORMS + k)
    return pl.pallas_call(
        _gmlp_kernel,
        grid=(n // GM_TM,),
        in_specs=[
            pl.BlockSpec((GM_TM, d), row),
            gain(0), gain(1),
            _layer_resident(w_in.shape, j), _layer_resident(ln_g.shape, j), _layer_resident(ln_b.shape, j),
            _layer_resident(w_s.shape, j), _layer_resident(b_s.shape, j), _layer_resident(w_out.shape, j),
        ],
        out_specs=pl.BlockSpec((GM_TM, d), row),
        out_shape=jax.ShapeDtypeStruct((n, d), F32),
        scratch_shapes=[pltpu.VMEM((GM_TM, GM_WIDTH), F32)],
        compiler_params=_params("arbitrary"),
        name="gmlp_mixer",
    )(h, gains, gains, w_in, ln_g, ln_b, w_s, b_s, w_out)


ATT_D = ATT_HEADS * ATT_HEAD_DIM


class _GroupPlan(NamedTuple):
    dilation: int
    view_minor: int
    block_rows: int
    halves: int
    n_units: int
    n_slots: int
    pieces: tuple
    first_units: tuple
    positions: np.ndarray

    @property
    def tokens_per_step(self):
        return self.n_units * ATT_BLK


def _group_plan(dilation):
    if dilation == 1:
        return _GroupPlan(1, 0, 4 * ATT_BLK, 1, 4, 1,
                          tuple(((None, u * ATT_BLK, ATT_BLK),) for u in range(4)), (0,),
                          np.arange(ATT_BLK))
    if dilation == 4:
        half = ATT_BLK // 2
        return _GroupPlan(4, 8, half, 1, 4, 4,
                          tuple(((c, 0, half), (c + 4, 0, half)) for c in range(4)), (0, 1, 2, 3),
                          np.concatenate([2 * np.arange(half), 2 * np.arange(half) + 1]))
    assert dilation == 16
    return _GroupPlan(16, 16, ATT_BLK, 2, 8, 8,
                      tuple(((c, 0, ATT_BLK),) for c in range(8)), tuple(range(8)),
                      np.arange(ATT_BLK))


def _t5_bucket(dist):
    d = jnp.maximum(dist, 1).astype(F32)
    large = REL_MAX_EXACT + (jnp.log(d / REL_MAX_EXACT) / math.log(REL_MAX_DIST / REL_MAX_EXACT)
                             * (REL_BUCKETS - REL_MAX_EXACT)).astype(jnp.int32)
    large = jnp.minimum(large, REL_BUCKETS - 1)
    return jnp.where(dist < REL_MAX_EXACT, dist, large)


def _bias_kernel(table_ref, bucket_ref, dist_ref, o_ref):
    g = pl.program_id(0)
    bucket = bucket_ref[...]
    dist = dist_ref[...]
    band = (dist >= 0) & (dist <= ATT_BLK)
    for hd in range(ATT_HEADS):
        acc = jnp.zeros(bucket.shape, F32)
        for b in range(REL_BUCKETS):
            acc = jnp.where(bucket == b, table_ref[b, g * ATT_HEADS + hd], acc)
        o_ref[hd] = jnp.where(band, acc * LOG2E, MASK_VALUE)


def _attention_bias(rel_bias, plans):
    dists = []
    for plan in plans:
        pos_k = np.concatenate([plan.positions - ATT_BLK, plan.positions])
        dists.append(plan.positions[:, None] - pos_k[None, :])
    dist = jnp.asarray(np.stack(dists), jnp.int32)
    scale = jnp.asarray([plan.dilation for plan in plans], jnp.int32)[:, None, None]
    buckets = _t5_bucket(jnp.maximum(dist, 0) * scale).astype(jnp.int32)
    tile = pl.BlockSpec((None, ATT_BLK, 2 * ATT_BLK), lambda g: (g, 0, 0))
    return pl.pallas_call(
        _bias_kernel,
        grid=(len(plans),),
        in_specs=[pl.BlockSpec(memory_space=pltpu.SMEM), tile, tile],
        out_specs=pl.BlockSpec((None, ATT_HEADS, ATT_BLK, 2 * ATT_BLK), lambda g: (g, 0, 0, 0)),
        out_shape=jax.ShapeDtypeStruct((len(plans), ATT_HEADS, ATT_BLK, 2 * ATT_BLK), F32),
        compiler_params=_params("arbitrary"),
        name="attention_bias",
    )(rel_bias, buckets, dist)


def _unit_load(ref, pieces):
    parts = [ref[r0:r0 + n, :] if sub is None else ref[r0:r0 + n, sub, :] for sub, r0, n in pieces]
    return parts[0] if len(parts) == 1 else jnp.concatenate(parts, axis=0)


def _unit_store(ref, pieces, cols, tile):
    at = 0
    for sub, r0, n in pieces:
        if sub is None:
            ref[r0:r0 + n, cols] = tile[at:at + n]
        else:
            ref[r0:r0 + n, sub, cols] = tile[at:at + n]
        at += n


def _attn_group_kernel(x_ref, g0_ref, w_ref, bias_ref, o_ref, lse_ref, k_ref, v_ref, q_ref, obuf_ref, *, plan):
    t = pl.program_id(2)

    @pl.when(t == 0)
    def _():
        k_ref[:, :ATT_BLK, :] = jnp.zeros((plan.n_slots, ATT_BLK, ATT_D), BF16)
        v_ref[:, :ATT_BLK, :] = jnp.zeros((plan.n_slots, ATT_BLK, ATT_D), BF16)

    xa = jnp.concatenate([_unit_load(x_ref, pcs) for pcs in plan.pieces], axis=0)
    a = _rms(xa, g0_ref[...]).astype(BF16)
    q_ref[...] = (_dot(a, w_ref[:, :ATT_D]) * (ATT_HEAD_DIM ** -0.5 * LOG2E)).astype(BF16)
    kc = _dot(a, w_ref[:, ATT_D:2 * ATT_D]).astype(BF16)
    vc = _dot(a, w_ref[:, 2 * ATT_D:]).astype(BF16)

    lane = lax.broadcasted_iota(jnp.int32, (ATT_BLK, V7X_LANES), 1)
    low = lane < ATT_HEAD_DIM
    kcol = lax.broadcasted_iota(jnp.int32, (ATT_BLK, 2 * ATT_BLK), 1)
    first_pen = jnp.where((kcol < ATT_BLK) & (t == 0), MASK_VALUE, 0.0)
    for u in range(plan.n_units):
        slot = u % plan.n_slots
        rows = slice(u * ATT_BLK, (u + 1) * ATT_BLK)
        k_ref[slot, ATT_BLK:, :] = kc[rows]
        v_ref[slot, ATT_BLK:, :] = vc[rows]
        m_tile = jnp.zeros((ATT_BLK, V7X_LANES), F32)
        den_tile = jnp.ones((ATT_BLK, V7X_LANES), F32)
        for hp in range(ATT_HEADS // 2):
            cols = slice(hp * V7X_LANES, (hp + 1) * V7X_LANES)
            qp = q_ref[rows, cols]
            kk = k_ref[slot, :, cols]
            vv = v_ref[slot, :, cols]
            outs, lses = [], []
            for sub in range(2):
                qm = jnp.where(low if sub == 0 else ~low, qp, jnp.zeros_like(qp))
                s = _dot_nt(qm, kk) + bias_ref[2 * hp + sub]
                if u in plan.first_units:
                    s = s + first_pen
                m = jnp.max(s, axis=-1, keepdims=True)
                e = jnp.exp2(s - m)
                den = jnp.sum(e, axis=-1, keepdims=True)
                outs.append(_dot(e.astype(BF16), vv) / den)
                lses.append((m, den))
            obuf_ref[rows, cols] = jnp.where(low, outs[0], outs[1])
            m_tile = jnp.where(lane == 2 * hp, lses[0][0], jnp.where(lane == 2 * hp + 1, lses[1][0], m_tile))
            den_tile = jnp.where(lane == 2 * hp, lses[0][1], jnp.where(lane == 2 * hp + 1, lses[1][1], den_tile))
        _unit_store(lse_ref, plan.pieces[u], slice(0, V7X_LANES), m_tile * LN2 + jnp.log(den_tile))
        k_ref[slot, :ATT_BLK, :] = k_ref[slot, ATT_BLK:, :]
        v_ref[slot, :ATT_BLK, :] = v_ref[slot, ATT_BLK:, :]
    for u in range(plan.n_units):
        _unit_store(o_ref, plan.pieces[u], slice(0, ATT_D), obuf_ref[u * ATT_BLK:(u + 1) * ATT_BLK, :])


def _attention_group(h, gains, w_in, bias, plan, group, layer, j):
    b, s, d = h.shape
    if plan.view_minor:
        lead = s // plan.view_minor
        view = lambda width: (b, lead, plan.view_minor, width)
        block = lambda width: pl.BlockSpec((None, plan.block_rows, V7X_SUBLANES, width),
                                           lambda bi, hf, t: (bi, t, hf, 0))
        steps = lead // plan.block_rows
    else:
        view = lambda width: (b, s, width)
        block = lambda width: pl.BlockSpec((None, plan.block_rows, width), lambda bi, hf, t: (bi, t, 0))
        steps = s // plan.block_rows
    o, lse = pl.pallas_call(
        functools.partial(_attn_group_kernel, plan=plan),
        grid=(b, plan.halves, steps),
        in_specs=[
            block(d),
            _layer_resident(gains.shape, layer * N_NORMS),
            pl.BlockSpec((None, d, 3 * ATT_D), lambda bi, hf, t: (j, 0, group), pipeline_mode=pl.Buffered(1)),
            pl.BlockSpec((None, ATT_HEADS, ATT_BLK, 2 * ATT_BLK), lambda bi, hf, t: (group, 0, 0, 0),
                         pipeline_mode=pl.Buffered(1)),
        ],
        out_specs=[block(ATT_D), block(V7X_LANES)],
        out_shape=[jax.ShapeDtypeStruct(view(ATT_D), F32), jax.ShapeDtypeStruct(view(V7X_LANES), F32)],
        scratch_shapes=[
            pltpu.VMEM((plan.n_slots, 2 * ATT_BLK, ATT_D), BF16),
            pltpu.VMEM((plan.n_slots, 2 * ATT_BLK, ATT_D), BF16),
            pltpu.VMEM((plan.tokens_per_step, ATT_D), BF16),
            pltpu.VMEM((plan.tokens_per_step, ATT_D), F32),
        ],
        compiler_params=_params("arbitrary", "arbitrary", "arbitrary"),
        name=f"attention_group{group}",
    )(h.reshape(view(d)), gains, w_in, bias)
    return o.reshape(b * s, ATT_D), lse.reshape(b * s, V7X_LANES)


ATT_MERGE_TM = 512


def _attn_merge_kernel(h_ref, o0_ref, o1_ref, o2_ref, l0_ref, l1_ref, l2_ref, g1_ref, wout_ref, expand_ref,
                       out_ref):
    l0, l1, l2 = l0_ref[...], l1_ref[...], l2_ref[...]
    m = jnp.maximum(jnp.maximum(l0, l1), l2)
    e0, e1, e2 = jnp.exp(l0 - m), jnp.exp(l1 - m), jnp.exp(l2 - m)
    inv = 1.0 / (e0 + e1 + e2)
    expand = expand_ref[...]

    def per_channel(w):
        hi = w.astype(BF16)
        lo = (w - hi.astype(F32)).astype(BF16)
        return _dot(hi, expand) + _dot(lo, expand)

    o = (per_channel(e0 * inv) * o0_ref[...] + per_channel(e1 * inv) * o1_ref[...]
         + per_channel(e2 * inv) * o2_ref[...])
    out_ref[...] = h_ref[...] + _rms(_dot(o.astype(BF16), wout_ref[...]), g1_ref[...])


def _attention_merge(h, outs, lses, gains, w_out, layer, j):
    n, d = h.shape
    row = lambda i: (i, 0)
    tile = pl.BlockSpec((ATT_MERGE_TM, d), row)
    lse_tile = pl.BlockSpec((ATT_MERGE_TM, V7X_LANES), row)
    expand = np.zeros((V7X_LANES, ATT_D), np.float32)
    for hd in range(ATT_HEADS):
        expand[hd, hd * ATT_HEAD_DIM:(hd + 1) * ATT_HEAD_DIM] = 1.0
    expand = jnp.asarray(expand, BF16)
    return pl.pallas_call(
        _attn_merge_kernel,
        grid=(n // ATT_MERGE_TM,),
        in_specs=[tile] * 4 + [lse_tile] * 3 + [
            _layer_resident(gains.shape, layer * N_NORMS + 1), _layer_resident(w_out.shape, j),
            _resident(expand.shape)],
        out_specs=tile,
        out_shape=jax.ShapeDtypeStruct((n, d), F32),
        compiler_params=_params("arbitrary"),
        name="attention_merge",
    )(h, *outs, *lses, gains, w_out, expand)


def kernel(x, p, norm_g, ret_w_in, ret_w_out, attn_w_in, attn_w_out, rel_bias, gm_w_in, gm_ln_g, gm_ln_b,
           gm_w_s, gm_b_s, gm_w_out, ffn_w_in, ffn_w_out, ple_w_proj, ple_w_gate):
    b, s, d = x.shape
    n = b * s
    bf = lambda w: w.astype(BF16)
    gains = norm_g.reshape(DEPTH * N_NORMS, 1, d)
    p = p.reshape(DEPTH, n, PLE_DIM)
    ret_w_in, ret_w_out = bf(ret_w_in), bf(ret_w_out)
    attn_w_in, attn_w_out = bf(attn_w_in), bf(attn_w_out)
    gm_w_in, gm_w_out = bf(gm_w_in), bf(gm_w_out)
    ffn_w_in, ffn_w_out, ple_w_proj, ple_w_gate = bf(ffn_w_in), bf(ffn_w_out), bf(ple_w_proj), bf(ple_w_gate)
    gm_ln_g = gm_ln_g.reshape(-1, 1, GM_WIDTH)
    gm_ln_b = gm_ln_b.reshape(-1, 1, GM_WIDTH)
    gm_b_s = gm_b_s.reshape(-1, GM_GROUPS, GM_CHUNK, 1)
    ret_consts = _retention_consts(s)
    plans = tuple(_group_plan(dil) for _, dil in DILATION_PAIRS)

    h = x
    for i in range(DEPTH):
        kind, j = i % N_MIXERS, i // N_MIXERS
        if kind == 0:
            h = _retention_mixer(h.reshape(b, s, d), gains, ret_w_in, ret_w_out, ret_consts, i, j)
        elif kind == 1:
            bias = _attention_bias(rel_bias, plans)
            outs, lses = [], []
            for gi, plan in enumerate(plans):
                o, lse = _attention_group(h.reshape(b, s, d), gains, attn_w_in, bias, plan, gi, i, j)
                outs.append(o)
                lses.append(lse)
            h = _attention_merge(h.reshape(n, d), outs, lses, gains, attn_w_out, i, j)
        else:
            h = _gmlp_mixer(h.reshape(n, d), gains, gm_w_in, gm_ln_g, gm_ln_b, gm_w_s, gm_b_s, gm_w_out, i, j)
        h = _ffn_ple(h.reshape(n, d), p, gains, ffn_w_in, ffn_w_out, ple_w_proj, ple_w_gate, i)
    return h.reshape(b, s, d)
```

```python
import functools
import math
from typing import NamedTuple

import jax
import jax.numpy as jnp
import numpy as np
from jax import lax
from jax.experimental import pallas as pl
from jax.experimental.pallas import tpu as pltpu

D_MODEL = 1024
DEPTH = 4
N_MIXERS = 3
N_NORMS = 5
PLE_DIM = 256
EPS = 1e-6

RET_HEADS = 4
RET_QK_DIM = 256
RET_V_DIM = 512
RET_CHUNK = 256
ROPE_BASE = 10000.0

ATT_HEADS = 16
ATT_HEAD_DIM = 64
DILATION_PAIRS = ((128, 1), (512, 4), (2048, 16))
ATT_BLK = 128
REL_BUCKETS = 32
REL_MAX_EXACT = 16
REL_MAX_DIST = 2048

GM_CHUNK = 128
GM_WIDTH = 2 * D_MODEL
GM_GROUPS = 8
GM_GROUP_DIM = GM_WIDTH // GM_GROUPS

FFN_HIDDEN = 2816

V7X_LANES = 128
V7X_SUBLANES = 8
V7X_VMEM_BYTES = 64 * 1024 * 1024
VMEM_LIMIT_BYTES = V7X_VMEM_BYTES - 8 * 1024 * 1024

MASK_VALUE = -1e30
LOG2E = math.log2(math.e)
LN2 = math.log(2.0)

BF16 = jnp.bfloat16
F32 = jnp.float32


def _resident(shape):
    return pl.BlockSpec(shape, lambda *_: (0,) * len(shape), pipeline_mode=pl.Buffered(1))


def _layer_resident(shape, layer):
    rest = tuple(shape[1:])
    return pl.BlockSpec((None,) + rest, lambda *_: (layer,) + (0,) * len(rest), pipeline_mode=pl.Buffered(1))


def _rms(x, g):
    return x * lax.rsqrt(jnp.mean(x * x, axis=-1, keepdims=True) + EPS) * g


def _dot(a, b):
    return jnp.dot(a, b, preferred_element_type=F32)


def _dot_nt(a, b):
    return lax.dot_general(a, b, (((1,), (1,)), ((), ())), preferred_element_type=F32)


def _dot_tn(a, b):
    return lax.dot_general(a, b, (((0,), (0,)), ((), ())), preferred_element_type=F32)


def _params(*semantics):
    return pltpu.CompilerParams(dimension_semantics=semantics, vmem_limit_bytes=VMEM_LIMIT_BYTES)


FFN_TM = 1024
FFN_SUB = 512


def _ffn_ple_kernel(h_ref, p_ref, g2_ref, g3_ref, g4_ref, win_ref, wout_ref, wp_ref, wg_ref, o_ref):
    for part in range(FFN_TM // FFN_SUB):
        rows = slice(part * FFN_SUB, (part + 1) * FFN_SUB)
        x = h_ref[rows, :]
        a = _rms(x, g2_ref[...]).astype(BF16)
        gate = _dot(a, win_ref[:, :FFN_HIDDEN])
        up = _dot(a, win_ref[:, FFN_HIDDEN:])
        hid = (gate * jax.nn.sigmoid(gate) * up).astype(BF16)
        h1 = x + _rms(_dot(hid, wout_ref[...]), g3_ref[...])
        emb = _dot(p_ref[rows, :].astype(BF16), wp_ref[...])
        gate = jax.nn.sigmoid(_dot(_rms(h1, g4_ref[...]).astype(BF16), wg_ref[...]))
        o_ref[rows, :] = h1 + gate * emb


def _ffn_ple(h, p, gains, w_in, w_out, w_proj, w_gate, layer):
    n, d = h.shape
    row = lambda i: (i, 0)
    gain = lambda k: _layer_resident(gains.shape, layer * N_NORMS + k)
    return pl.pallas_call(
        _ffn_ple_kernel,
        grid=(n // FFN_TM,),
        in_specs=[
            pl.BlockSpec((FFN_TM, d), row),
            pl.BlockSpec((None, FFN_TM, PLE_DIM), lambda i: (layer, i, 0)),
            gain(2), gain(3), gain(4),
            _layer_resident(w_in.shape, layer), _layer_resident(w_out.shape, layer),
            _layer_resident(w_proj.shape, layer), _layer_resident(w_gate.shape, layer),
        ],
        out_specs=pl.BlockSpec((FFN_TM, d), row),
        out_shape=jax.ShapeDtypeStruct((n, d), F32),
        compiler_params=_params("arbitrary"),
        name="ffn_ple",
    )(h, p, gains, gains, gains, w_in, w_out, w_proj, w_gate)


RET_TM = 512
RET_Q_OFF = 0
RET_K_OFF = RET_HEADS * RET_QK_DIM
RET_V_OFF = 2 * RET_HEADS * RET_QK_DIM
RET_G_OFF = RET_V_OFF + RET_HEADS * RET_V_DIM


def _rope(t, cos, sin):
    half = RET_QK_DIM // 2
    t1, t2 = t[:, :half], t[:, half:]
    return jnp.concatenate([t1 * cos - t2 * sin, t2 * cos + t1 * sin], axis=-1)


def _retention_kernel(h_ref, cos_ref, sin_ref, g0_ref, g1_ref, win_ref, wout_ref,
                      dmask_ref, qdec_ref, kdec_ref, cdec_ref, o_ref, state_ref, y_ref):
    @pl.when(pl.program_id(1) == 0)
    def _():
        state_ref[...] = jnp.zeros(state_ref.shape, F32)

    x = h_ref[...]
    a = _rms(x, g0_ref[...]).astype(BF16)
    cos, sin = cos_ref[...], sin_ref[...]
    for hd in range(RET_HEADS):
        qo, ko, vo = RET_Q_OFF + hd * RET_QK_DIM, RET_K_OFF + hd * RET_QK_DIM, RET_V_OFF + hd * RET_V_DIM
        q = _rope(_dot(a, win_ref[:, qo:qo + RET_QK_DIM]), cos, sin).astype(BF16)
        k = _rope(_dot(a, win_ref[:, ko:ko + RET_QK_DIM]), cos, sin) * (RET_QK_DIM ** -0.5)
        v = _dot(a, win_ref[:, vo:vo + RET_V_DIM]).astype(BF16)
        for c in range(RET_TM // RET_CHUNK):
            rows = slice(c * RET_CHUNK, (c + 1) * RET_CHUNK)
            qc, kc, vc = q[rows], k[rows], v[rows]
            st = state_ref[hd]
            scores = _dot_nt(qc, kc.astype(BF16)) * dmask_ref[hd]
            inner = _dot(scores.astype(BF16), vc)
            cross = _dot(qc, st.astype(BF16)) * qdec_ref[hd]
            state_ref[hd] = st * cdec_ref[hd] + _dot_tn((kc * kdec_ref[hd]).astype(BF16), vc)
            y = inner + cross
            y_ref[rows, hd * RET_V_DIM:(hd + 1) * RET_V_DIM] = y * lax.rsqrt(
                jnp.mean(y * y, axis=-1, keepdims=True) + EPS)
    gate = _dot(a, win_ref[:, RET_G_OFF:RET_G_OFF + RET_HEADS * RET_V_DIM])
    z = (gate * jax.nn.sigmoid(gate) * y_ref[...]).astype(BF16)
    o_ref[...] = x + _rms(_dot(z, wout_ref[...]), g1_ref[...])


def _retention_consts(seq_len):
    hh, c, dk = RET_HEADS, RET_CHUNK, RET_QK_DIM
    inv_freq = np.float32(ROPE_BASE) ** (-np.arange(0, dk, 2, dtype=np.float32) / np.float32(dk))
    ang = np.arange(seq_len, dtype=np.float32)[:, None] * inv_freq[None, :]
    cos = np.cos(ang.astype(np.float64)).astype(np.float32)
    sin = np.sin(ang.astype(np.float64)).astype(np.float32)
    log_gamma = jnp.log1p(-jnp.exp2(-5.0 - jnp.arange(hh, dtype=F32)))
    idx = jnp.arange(c, dtype=F32)
    diff = idx[:, None] - idx[None, :]
    dmask = jnp.where(diff >= 0, jnp.exp(log_gamma[:, None, None] * jnp.maximum(diff, 0.0)), 0.0)
    qdec = jnp.exp(log_gamma[:, None] * (idx[None, :] + 1.0))[:, :, None]
    kdec = jnp.exp(log_gamma[:, None] * (c - 1.0 - idx[None, :]))[:, :, None]
    cdec = jnp.exp(log_gamma * c)[:, None, None]
    return jnp.asarray(cos), jnp.asarray(sin), dmask, qdec, kdec, cdec


def _retention_mixer(h, gains, w_in, w_out, consts, layer, j):
    b, s, d = h.shape
    cos, sin, dmask, qdec, kdec, cdec = consts
    tile = lambda bi, t: (bi, t, 0)
    pos = lambda bi, t: (t, 0)
    gain = lambda k: _layer_resident(gains.shape, layer * N_NORMS + k)
    return pl.pallas_call(
        _retention_kernel,
        grid=(b, s // RET_TM),
        in_specs=[
            pl.BlockSpec((None, RET_TM, d), tile),
            pl.BlockSpec((RET_TM, RET_QK_DIM // 2), pos),
            pl.BlockSpec((RET_TM, RET_QK_DIM // 2), pos),
            gain(0), gain(1),
            _layer_resident(w_in.shape, j), _layer_resident(w_out.shape, j),
            _resident(dmask.shape), _resident(qdec.shape), _resident(kdec.shape), _resident(cdec.shape),
        ],
        out_specs=pl.BlockSpec((None, RET_TM, d), tile),
        out_shape=jax.ShapeDtypeStruct((b, s, d), F32),
        scratch_shapes=[
            pltpu.VMEM((RET_HEADS, RET_QK_DIM, RET_V_DIM), F32),
            pltpu.VMEM((RET_TM, RET_HEADS * RET_V_DIM), F32),
        ],
        compiler_params=_params("arbitrary", "arbitrary"),
        name="retention_mixer",
    )(h, cos, sin, gains, gains, w_in, w_out, dmask, qdec, kdec, cdec)


GM_TM = 512


def _gmlp_kernel(h_ref, g0_ref, g1_ref, win_ref, lng_ref, lnb_ref, ws_ref, bs_ref, wout_ref, o_ref, s_ref):
    x = h_ref[...]
    a = _rms(x, g0_ref[...]).astype(BF16)

    def gelu(t):
        return 0.5 * t * (1.0 + lax.erf(t * (2.0 ** -0.5)))

    v = gelu(_dot(a, win_ref[:, GM_WIDTH:]))
    u = gelu(_dot(a, win_ref[:, :GM_WIDTH]))
    mu = jnp.mean(v, axis=-1, keepdims=True)
    vc = v - mu
    var = jnp.mean(vc * vc, axis=-1, keepdims=True)
    vn = (vc * lax.rsqrt(var + EPS) * lng_ref[...] + lnb_ref[...]).astype(BF16)
    row = lax.broadcasted_iota(jnp.int32, (GM_CHUNK, GM_CHUNK), 0)
    col = lax.broadcasted_iota(jnp.int32, (GM_CHUNK, GM_CHUNK), 1)
    causal = row >= col
    for g in range(GM_GROUPS):
        w = jnp.where(causal, ws_ref[g], 0.0).astype(BF16)
        cols = slice(g * GM_GROUP_DIM, (g + 1) * GM_GROUP_DIM)
        for c in range(GM_TM // GM_CHUNK):
            rows = slice(c * GM_CHUNK, (c + 1) * GM_CHUNK)
            s_ref[rows, cols] = _dot(w, vn[rows, cols]) + bs_ref[g]
    z = (u * s_ref[...]).astype(BF16)
    o_ref[...] = x + _rms(_dot(z, wout_ref[...]), g1_ref[...])


def _gmlp_mixer(h, gains, w_in, ln_g, ln_b, w_s, b_s, w_out, layer, j):
    n, d = h.shape
    row = lambda i: (i, 0)
    gain = lambda k: _layer_resident(gains.shape, layer * N_NORMS + k)
    return pl.pallas_call(
        _gmlp_kernel,
        grid=(n // GM_TM,),
        in_specs=[
            pl.BlockSpec((GM_TM, d), row),
            gain(0), gain(1),
            _layer_resident(w_in.shape, j), _layer_resident(ln_g.shape, j), _layer_resident(ln_b.shape, j),
            _layer_resident(w_s.shape, j), _layer_resident(b_s.shape, j), _layer_resident(w_out.shape, j),
        ],
        out_specs=pl.BlockSpec((GM_TM, d), row),
        out_shape=jax.ShapeDtypeStruct((n, d), F32),
        scratch_shapes=[pltpu.VMEM((GM_TM, GM_WIDTH), F32)],
        compiler_params=_params("arbitrary"),
        name="gmlp_mixer",
    )(h, gains, gains, w_in, ln_g, ln_b, w_s, b_s, w_out)


ATT_D = ATT_HEADS * ATT_HEAD_DIM


class _GroupPlan(NamedTuple):
    dilation: int
    view_minor: int
    block_rows: int
    halves: int
    n_units: int
    n_slots: int
    pieces: tuple
    first_units: tuple
    positions: np.ndarray

    @property
    def tokens_per_step(self):
        return self.n_units * ATT_BLK


def _group_plan(dilation):
    if dilation == 1:
        return _GroupPlan(1, 0, 8 * ATT_BLK, 1, 8, 1,
                          tuple(((None, u * ATT_BLK, ATT_BLK),) for u in range(8)), (0,),
                          np.arange(ATT_BLK))
    if dilation == 4:
        half = ATT_BLK // 2
        return _GroupPlan(4, 8, ATT_BLK, 1, 8, 4,
                          tuple(((c, blk * half, half), (c + 4, blk * half, half))
                                for blk in range(2) for c in range(4)),
                          (0, 1, 2, 3),
                          np.concatenate([2 * np.arange(half), 2 * np.arange(half) + 1]))
    assert dilation == 16
    return _GroupPlan(16, 16, ATT_BLK, 2, 8, 8,
                      tuple(((c, 0, ATT_BLK),) for c in range(8)), tuple(range(8)),
                      np.arange(ATT_BLK))


def _t5_bucket(dist):
    d = jnp.maximum(dist, 1).astype(F32)
    large = REL_MAX_EXACT + (jnp.log(d / REL_MAX_EXACT) / math.log(REL_MAX_DIST / REL_MAX_EXACT)
                             * (REL_BUCKETS - REL_MAX_EXACT)).astype(jnp.int32)
    large = jnp.minimum(large, REL_BUCKETS - 1)
    return jnp.where(dist < REL_MAX_EXACT, dist, large)


def _bias_kernel(table_ref, bucket_ref, dist_ref, o_ref):
    g = pl.program_id(0)
    bucket = bucket_ref[...]
    dist = dist_ref[...]
    band = (dist >= 0) & (dist <= ATT_BLK)
    for hd in range(ATT_HEADS):
        acc = jnp.zeros(bucket.shape, F32)
        for b in range(REL_BUCKETS):
            acc = jnp.where(bucket == b, table_ref[b, g * ATT_HEADS + hd], acc)
        o_ref[hd] = jnp.where(band, acc * LOG2E, MASK_VALUE)


def _attention_bias(rel_bias, plans):
    dists = []
    for plan in plans:
        pos_k = np.concatenate([plan.positions - ATT_BLK, plan.positions])
        dists.append(plan.positions[:, None] - pos_k[None, :])
    dist = jnp.asarray(np.stack(dists), jnp.int32)
    scale = jnp.asarray([plan.dilation for plan in plans], jnp.int32)[:, None, None]
    buckets = _t5_bucket(jnp.maximum(dist, 0) * scale).astype(jnp.int32)
    tile = pl.BlockSpec((None, ATT_BLK, 2 * ATT_BLK), lambda g: (g, 0, 0))
    return pl.pallas_call(
        _bias_kernel,
        grid=(len(plans),),
        in_specs=[pl.BlockSpec(memory_space=pltpu.SMEM), tile, tile],
        out_specs=pl.BlockSpec((None, ATT_HEADS, ATT_BLK, 2 * ATT_BLK), lambda g: (g, 0, 0, 0)),
        out_shape=jax.ShapeDtypeStruct((len(plans), ATT_HEADS, ATT_BLK, 2 * ATT_BLK), F32),
        compiler_params=_params("arbitrary"),
        name="attention_bias",
    )(rel_bias, buckets, dist)


def _attn_units(plan, t, xa, g0_ref, w_ref, bias_ref, k_ref, v_ref, q_ref, put_o, put_lse):
    @pl.when(t == 0)
    def _():
        k_ref[:, :ATT_BLK, :] = jnp.zeros((plan.n_slots, ATT_BLK, ATT_D), BF16)
        v_ref[:, :ATT_BLK, :] = jnp.zeros((plan.n_slots, ATT_BLK, ATT_D), BF16)

    a = _rms(xa, g0_ref[...]).astype(BF16)
    q_ref[...] = (_dot(a, w_ref[:, :ATT_D]) * (ATT_HEAD_DIM ** -0.5 * LOG2E)).astype(BF16)
    kc = _dot(a, w_ref[:, ATT_D:2 * ATT_D]).astype(BF16)
    vc = _dot(a, w_ref[:, 2 * ATT_D:]).astype(BF16)

    lane = lax.broadcasted_iota(jnp.int32, (ATT_BLK, V7X_LANES), 1)
    low = lane < ATT_HEAD_DIM
    kcol = lax.broadcasted_iota(jnp.int32, (ATT_BLK, 2 * ATT_BLK), 1)
    first_pen = jnp.where((kcol < ATT_BLK) & (t == 0), MASK_VALUE, 0.0)
    for u in range(plan.n_units):
        slot = u % plan.n_slots
        rows = slice(u * ATT_BLK, (u + 1) * ATT_BLK)
        k_ref[slot, ATT_BLK:, :] = kc[rows]
        v_ref[slot, ATT_BLK:, :] = vc[rows]
        m_tile = jnp.zeros((ATT_BLK, V7X_LANES), F32)
        den_tile = jnp.ones((ATT_BLK, V7X_LANES), F32)
        for hp in range(ATT_HEADS // 2):
            cols = slice(hp * V7X_LANES, (hp + 1) * V7X_LANES)
            qp = q_ref[rows, cols]
            kk = k_ref[slot, :, cols]
            vv = v_ref[slot, :, cols]
            outs, lses = [], []
            for sub in range(2):
                qm = jnp.where(low if sub == 0 else ~low, qp, jnp.zeros_like(qp))
                s = _dot_nt(qm, kk) + bias_ref[2 * hp + sub]
                if u in plan.first_units:
                    s = s + first_pen
                m = jnp.max(s, axis=-1, keepdims=True)
                e = jnp.exp2(s - m)
                den = jnp.sum(e, axis=-1, keepdims=True)
                outs.append(_dot(e.astype(BF16), vv) / den)
                lses.append((m, den))
            put_o(rows, cols, jnp.where(low, outs[0], outs[1]))
            m_tile = jnp.where(lane == 2 * hp, lses[0][0], jnp.where(lane == 2 * hp + 1, lses[1][0], m_tile))
            den_tile = jnp.where(lane == 2 * hp, lses[0][1], jnp.where(lane == 2 * hp + 1, lses[1][1], den_tile))
        put_lse(rows, m_tile * LN2 + jnp.log(den_tile))
        k_ref[slot, :ATT_BLK, :] = k_ref[slot, ATT_BLK:, :]
        v_ref[slot, :ATT_BLK, :] = v_ref[slot, ATT_BLK:, :]


def _attn_dense_kernel(x_ref, g0_ref, w_ref, bias_ref, o_ref, lse_ref, k_ref, v_ref, q_ref, *, plan):
    def put_o(rows, cols, tile):
        o_ref[rows, cols] = tile

    def put_lse(rows, tile):
        lse_ref[rows, :] = tile

    _attn_units(plan, pl.program_id(2), x_ref[...], g0_ref, w_ref, bias_ref, k_ref, v_ref, q_ref, put_o, put_lse)


def _attn_strided_kernel(x_hbm, g0_ref, w_ref, bias_ref, o_hbm, lse_hbm, k_ref, v_ref, q_ref,
                         xbuf, obuf, lbuf, xsem, osem, lsem, *, plan, n_batch, n_steps):
    b, hf, t = pl.program_id(0), pl.program_id(1), pl.program_id(2)
    step = (b * plan.halves + hf) * n_steps + t
    n_total = n_batch * plan.halves * n_steps
    buf = lax.rem(step, 2)

    def class_rows(ref, bb, hh, tt, sub, r0, n):
        return ref.at[bb, pl.ds(tt * plan.block_rows + r0, n), hh * V7X_SUBLANES + sub]

    def copies(kind, bb, hh, tt, which):
        out = []
        for u, pcs in enumerate(plan.pieces):
            at = u * ATT_BLK
            for sub, r0, n in pcs:
                dense = pl.ds(at, n)
                if kind == "gather":
                    out.append(pltpu.make_async_copy(class_rows(x_hbm, bb, hh, tt, sub, r0, n),
                                                     xbuf.at[which, dense], xsem.at[which]))
                else:
                    out.append(pltpu.make_async_copy(obuf.at[which, dense],
                                                     class_rows(o_hbm, bb, hh, tt, sub, r0, n), osem.at[which]))
                    out.append(pltpu.make_async_copy(lbuf.at[which, dense],
                                                     class_rows(lse_hbm, bb, hh, tt, sub, r0, n), lsem.at[which]))
                at += n
        return out

    @pl.when(step == 0)
    def _():
        for cp in copies("gather", b, hf, t, buf):
            cp.start()

    wrap_t = t + 1 == n_steps
    wrap_h = hf + 1 == plan.halves
    nt = jnp.where(wrap_t, 0, t + 1)
    nh = jnp.where(wrap_t, jnp.where(wrap_h, 0, hf + 1), hf)
    nb = jnp.where(wrap_t & wrap_h, b + 1, b)

    @pl.when(step + 1 < n_total)
    def _():
        for cp in copies("gather", nb, nh, nt, 1 - buf):
            cp.start()

    for cp in copies("gather", b, hf, t, buf):
        cp.wait()

    @pl.when(step >= 2)
    def _():
        for cp in copies("scatter", b, hf, t, buf):
            cp.wait()

    def put_o(rows, cols, tile):
        obuf[buf, rows, cols] = tile

    def put_lse(rows, tile):
        lbuf[buf, rows, :] = tile

    _attn_units(plan, t, xbuf[buf], g0_ref, w_ref, bias_ref, k_ref, v_ref, q_ref, put_o, put_lse)

    for cp in copies("scatter", b, hf, t, buf):
        cp.start()

    @pl.when(step == n_total - 1)
    def _():
        for cp in copies("scatter", b, hf, t, 1 - buf):
            cp.wait()
        for cp in copies("scatter", b, hf, t, buf):
            cp.wait()


def _attention_group(h, gains, w_in, bias, plan, group, layer, j):
    b, s, d = h.shape
    tokens = plan.tokens_per_step
    resident = [
        _layer_resident(gains.shape, layer * N_NORMS),
        pl.BlockSpec((None, d, 3 * ATT_D), lambda bi, hf, t: (j, 0, group), pipeline_mode=pl.Buffered(1)),
        pl.BlockSpec((None, ATT_HEADS, ATT_BLK, 2 * ATT_BLK), lambda bi, hf, t: (group, 0, 0, 0),
                     pipeline_mode=pl.Buffered(1)),
    ]
    kv_q = [
        pltpu.VMEM((plan.n_slots, 2 * ATT_BLK, ATT_D), BF16),
        pltpu.VMEM((plan.n_slots, 2 * ATT_BLK, ATT_D), BF16),
        pltpu.VMEM((tokens, ATT_D), BF16),
    ]
    if not plan.view_minor:
        steps = s // plan.block_rows
        block = lambda width: pl.BlockSpec((None, plan.block_rows, width), lambda bi, hf, t: (bi, t, 0))
        o, lse = pl.pallas_call(
            functools.partial(_attn_dense_kernel, plan=plan),
            grid=(b, 1, steps),
            in_specs=[block(d)] + resident,
            out_specs=[block(ATT_D), block(V7X_LANES)],
            out_shape=[jax.ShapeDtypeStruct((b, s, ATT_D), F32), jax.ShapeDtypeStruct((b, s, V7X_LANES), F32)],
            scratch_shapes=kv_q,
            compiler_params=_params("arbitrary", "arbitrary", "arbitrary"),
            name=f"attention_group{group}",
        )(h, gains, w_in, bias)
    else:
        lead = s // plan.view_minor
        steps = lead // plan.block_rows
        assert b * plan.halves * steps >= 2
        view = lambda width: (b, lead, plan.view_minor, width)
        hbm = pl.BlockSpec(memory_space=pl.ANY)
        o, lse = pl.pallas_call(
            functools.partial(_attn_strided_kernel, plan=plan, n_batch=b, n_steps=steps),
            grid=(b, plan.halves, steps),
            in_specs=[hbm] + resident,
            out_specs=[hbm, hbm],
            out_shape=[jax.ShapeDtypeStruct(view(ATT_D), F32), jax.ShapeDtypeStruct(view(V7X_LANES), F32)],
            scratch_shapes=kv_q + [
                pltpu.VMEM((2, tokens, d), F32),
                pltpu.VMEM((2, tokens, ATT_D), F32),
                pltpu.VMEM((2, tokens, V7X_LANES), F32),
                pltpu.SemaphoreType.DMA((2,)), pltpu.SemaphoreType.DMA((2,)), pltpu.SemaphoreType.DMA((2,)),
            ],
            compiler_params=_params("arbitrary", "arbitrary", "arbitrary"),
            name=f"attention_group{group}",
        )(h.reshape(view(d)), gains, w_in, bias)
    return o.reshape(b * s, ATT_D), lse.reshape(b * s, V7X_LANES)


ATT_MERGE_TM = 512


def _attn_merge_kernel(h_ref, o0_ref, o1_ref, o2_ref, l0_ref, l1_ref, l2_ref, g1_ref, wout_ref, expand_ref,
                       out_ref):
    l0, l1, l2 = l0_ref[...], l1_ref[...], l2_ref[...]
    m = jnp.maximum(jnp.maximum(l0, l1), l2)
    e0, e1, e2 = jnp.exp(l0 - m), jnp.exp(l1 - m), jnp.exp(l2 - m)
    inv = 1.0 / (e0 + e1 + e2)
    expand = expand_ref[...]

    def per_channel(w):
        hi = w.astype(BF16)
        lo = (w - hi.astype(F32)).astype(BF16)
        return _dot(hi, expand) + _dot(lo, expand)

    o = (per_channel(e0 * inv) * o0_ref[...] + per_channel(e1 * inv) * o1_ref[...]
         + per_channel(e2 * inv) * o2_ref[...])
    out_ref[...] = h_ref[...] + _rms(_dot(o.astype(BF16), wout_ref[...]), g1_ref[...])


def _attention_merge(h, outs, lses, gains, w_out, layer, j):
    n, d = h.shape
    row = lambda i: (i, 0)
    tile = pl.BlockSpec((ATT_MERGE_TM, d), row)
    lse_tile = pl.BlockSpec((ATT_MERGE_TM, V7X_LANES), row)
    expand = np.zeros((V7X_LANES, ATT_D), np.float32)
    for hd in range(ATT_HEADS):
        expand[hd, hd * ATT_HEAD_DIM:(hd + 1) * ATT_HEAD_DIM] = 1.0
    expand = jnp.asarray(expand, BF16)
    return pl.pallas_call(
        _attn_merge_kernel,
        grid=(n // ATT_MERGE_TM,),
        in_specs=[tile] * 4 + [lse_tile] * 3 + [
            _layer_resident(gains.shape, layer * N_NORMS + 1), _layer_resident(w_out.shape, j),
            _resident(expand.shape)],
        out_specs=tile,
        out_shape=jax.ShapeDtypeStruct((n, d), F32),
        compiler_params=_params("arbitrary"),
        name="attention_merge",
    )(h, *outs, *lses, gains, w_out, expand)


def kernel(x, p, norm_g, ret_w_in, ret_w_out, attn_w_in, attn_w_out, rel_bias, gm_w_in, gm_ln_g, gm_ln_b,
           gm_w_s, gm_b_s, gm_w_out, ffn_w_in, ffn_w_out, ple_w_proj, ple_w_gate):
    b, s, d = x.shape
    n = b * s
    bf = lambda w: w.astype(BF16)
    gains = norm_g.reshape(DEPTH * N_NORMS, 1, d)
    p = p.reshape(DEPTH, n, PLE_DIM)
    ret_w_in, ret_w_out = bf(ret_w_in), bf(ret_w_out)
    attn_w_in, attn_w_out = bf(attn_w_in), bf(attn_w_out)
    gm_w_in, gm_w_out = bf(gm_w_in), bf(gm_w_out)
    ffn_w_in, ffn_w_out, ple_w_proj, ple_w_gate = bf(ffn_w_in), bf(ffn_w_out), bf(ple_w_proj), bf(ple_w_gate)
    gm_ln_g = gm_ln_g.reshape(-1, 1, GM_WIDTH)
    gm_ln_b = gm_ln_b.reshape(-1, 1, GM_WIDTH)
    gm_b_s = gm_b_s.reshape(-1, GM_GROUPS, GM_CHUNK, 1)
    ret_consts = _retention_consts(s)
    plans = tuple(_group_plan(dil) for _, dil in DILATION_PAIRS)

    h = x
    for i in range(DEPTH):
        kind, j = i % N_MIXERS, i // N_MIXERS
        if kind == 0:
            h = _retention_mixer(h.reshape(b, s, d), gains, ret_w_in, ret_w_out, ret_consts, i, j)
        elif kind == 1:
            bias = _attention_bias(rel_bias, plans)
            outs, lses = [], []
            for gi, plan in enumerate(plans):
                o, lse = _attention_group(h.reshape(b, s, d), gains, attn_w_in, bias, plan, gi, i, j)
                outs.append(o)
                lses.append(lse)
            h = _attention_merge(h.reshape(n, d), outs, lses, gains, attn_w_out, i, j)
        else:
            h = _gmlp_mixer(h.reshape(n, d), gains, gm_w_in, gm_ln_g, gm_ln_b, gm_w_s, gm_b_s, gm_w_out, i, j)
        h = _ffn_ple(h.reshape(n, d), p, gains, ffn_w_in, ffn_w_out, ple_w_proj, ple_w_gate, i)
    return h.reshape(b, s, d)
```

```python
import functools
import math
from typing import NamedTuple

import jax
import jax.numpy as jnp
import numpy as np
from jax import lax
from jax.experimental import pallas as pl
from jax.experimental.pallas import tpu as pltpu

D_MODEL = 1024
DEPTH = 4
N_MIXERS = 3
N_NORMS = 5
PLE_DIM = 256
EPS = 1e-6

RET_HEADS = 4
RET_QK_DIM = 256
RET_V_DIM = 512
RET_CHUNK = 256
ROPE_BASE = 10000.0

ATT_HEADS = 16
ATT_HEAD_DIM = 64
DILATION_PAIRS = ((128, 1), (512, 4), (2048, 16))
ATT_BLK = 128
REL_BUCKETS = 32
REL_MAX_EXACT = 16
REL_MAX_DIST = 2048

GM_CHUNK = 128
GM_WIDTH = 2 * D_MODEL
GM_GROUPS = 8
GM_GROUP_DIM = GM_WIDTH // GM_GROUPS

FFN_HIDDEN = 2816

V7X_LANES = 128
V7X_SUBLANES = 8
V7X_VMEM_BYTES = 64 * 1024 * 1024
VMEM_LIMIT_BYTES = V7X_VMEM_BYTES - 8 * 1024 * 1024

MASK_VALUE = -1e30
LOG2E = math.log2(math.e)
LN2 = math.log(2.0)

BF16 = jnp.bfloat16
F32 = jnp.float32


def _resident(shape):
    return pl.BlockSpec(shape, lambda *_: (0,) * len(shape), pipeline_mode=pl.Buffered(1))


def _layer_resident(shape, layer):
    rest = tuple(shape[1:])
    return pl.BlockSpec((None,) + rest, lambda *_: (layer,) + (0,) * len(rest), pipeline_mode=pl.Buffered(1))


def _rms(x, g):
    return x * lax.rsqrt(jnp.mean(x * x, axis=-1, keepdims=True) + EPS) * g


def _dot(a, b):
    return jnp.dot(a, b, preferred_element_type=F32)


def _dot_nt(a, b):
    return lax.dot_general(a, b, (((1,), (1,)), ((), ())), preferred_element_type=F32)


def _dot_tn(a, b):
    return lax.dot_general(a, b, (((0,), (0,)), ((), ())), preferred_element_type=F32)


def _params(*semantics):
    return pltpu.CompilerParams(dimension_semantics=semantics, vmem_limit_bytes=VMEM_LIMIT_BYTES)


WEIGHT_STAGE_BYTES = 1024 * 1024
BF16_SUBLANES = 16

HBM = pl.BlockSpec(memory_space=pl.ANY)


class _WeightSpec(NamedTuple):
    layer: int
    rows: int
    cols: int
    col0: int

    @property
    def chunk_rows(self):
        fits = [r for r in range(BF16_SUBLANES, self.rows + 1, BF16_SUBLANES)
                if self.rows % r == 0 and r * self.cols * 4 <= WEIGHT_STAGE_BYTES]
        return max(fits)

    @property
    def scratch(self):
        slots = 2 if self.rows > self.chunk_rows else 1
        return [pltpu.VMEM((self.rows, self.cols), BF16), pltpu.VMEM((slots, self.chunk_rows, self.cols), F32),
                pltpu.SemaphoreType.DMA((slots,))]


def _weight(array, layer, col0=0, cols=None):
    return _WeightSpec(layer, array.shape[1], array.shape[2] if cols is None else cols, col0)


def _load_weight(spec, w_hbm, w_ref, stage, sem):
    slots, rows = stage.shape[0], stage.shape[1]
    n_chunks = spec.rows // rows

    def copy(c):
        src = w_hbm.at[spec.layer, pl.ds(c * rows, rows), pl.ds(spec.col0, spec.cols)]
        return pltpu.make_async_copy(src, stage.at[c % slots], sem.at[c % slots])

    copy(0).start()
    for c in range(n_chunks):
        if c + 1 < n_chunks:
            copy(c + 1).start()
        copy(c).wait()
        w_ref[c * rows:(c + 1) * rows, :] = stage[c % slots].astype(BF16)


def _resident_weights(first, specs, hbm_refs, scratch):
    triples = [scratch[3 * i:3 * i + 3] for i in range(len(specs))]

    @pl.when(first)
    def _():
        for spec, w_hbm, (w_ref, stage, sem) in zip(specs, hbm_refs, triples):
            _load_weight(spec, w_hbm, w_ref, stage, sem)

    return [w_ref for w_ref, _, _ in triples]


def _weight_scratch(specs):
    return [s for spec in specs for s in spec.scratch]


FFN_TM = 1024
FFN_SUB = 512


def _ffn_ple_kernel(h_ref, p_ref, g2_ref, g3_ref, g4_ref, win_hbm, wout_hbm, wp_hbm, wg_hbm, o_ref, *wscr, specs):
    win_ref, wout_ref, wp_ref, wg_ref = _resident_weights(
        pl.program_id(0) == 0, specs, (win_hbm, wout_hbm, wp_hbm, wg_hbm), wscr)
    for part in range(FFN_TM // FFN_SUB):
        rows = slice(part * FFN_SUB, (part + 1) * FFN_SUB)
        x = h_ref[rows, :]
        a = _rms(x, g2_ref[...]).astype(BF16)
        gate = _dot(a, win_ref[:, :FFN_HIDDEN])
        up = _dot(a, win_ref[:, FFN_HIDDEN:])
        hid = (gate * jax.nn.sigmoid(gate) * up).astype(BF16)
        h1 = x + _rms(_dot(hid, wout_ref[...]), g3_ref[...])
        emb = _dot(p_ref[rows, :].astype(BF16), wp_ref[...])
        gate = jax.nn.sigmoid(_dot(_rms(h1, g4_ref[...]).astype(BF16), wg_ref[...]))
        o_ref[rows, :] = h1 + gate * emb


def _ffn_ple(h, p, gains, w_in, w_out, w_proj, w_gate, layer):
    n, d = h.shape
    row = lambda i: (i, 0)
    gain = lambda k: _layer_resident(gains.shape, layer * N_NORMS + k)
    specs = tuple(_weight(w, layer) for w in (w_in, w_out, w_proj, w_gate))
    return pl.pallas_call(
        functools.partial(_ffn_ple_kernel, specs=specs),
        grid=(n // FFN_TM,),
        in_specs=[
            pl.BlockSpec((FFN_TM, d), row),
            pl.BlockSpec((None, FFN_TM, PLE_DIM), lambda i: (layer, i, 0)),
            gain(2), gain(3), gain(4),
            HBM, HBM, HBM, HBM,
        ],
        out_specs=pl.BlockSpec((FFN_TM, d), row),
        out_shape=jax.ShapeDtypeStruct((n, d), F32),
        scratch_shapes=_weight_scratch(specs),
        compiler_params=_params("arbitrary"),
        name="ffn_ple",
    )(h, p, gains, gains, gains, w_in, w_out, w_proj, w_gate)


RET_TM = 512
RET_Q_OFF = 0
RET_K_OFF = RET_HEADS * RET_QK_DIM
RET_V_OFF = 2 * RET_HEADS * RET_QK_DIM
RET_G_OFF = RET_V_OFF + RET_HEADS * RET_V_DIM


def _rope(t, cos, sin):
    half = RET_QK_DIM // 2
    t1, t2 = t[:, :half], t[:, half:]
    return jnp.concatenate([t1 * cos - t2 * sin, t2 * cos + t1 * sin], axis=-1)


def _retention_kernel(h_ref, cos_ref, sin_ref, g0_ref, g1_ref, win_hbm, wout_hbm,
                      dmask_ref, qdec_ref, kdec_ref, cdec_ref, o_ref, state_ref, y_ref, *wscr, specs):
    win_ref, wout_ref = _resident_weights(
        (pl.program_id(0) == 0) & (pl.program_id(1) == 0), specs, (win_hbm, wout_hbm), wscr)

    @pl.when(pl.program_id(1) == 0)
    def _():
        state_ref[...] = jnp.zeros(state_ref.shape, F32)

    x = h_ref[...]
    a = _rms(x, g0_ref[...]).astype(BF16)
    cos, sin = cos_ref[...], sin_ref[...]
    for hd in range(RET_HEADS):
        qo, ko, vo = RET_Q_OFF + hd * RET_QK_DIM, RET_K_OFF + hd * RET_QK_DIM, RET_V_OFF + hd * RET_V_DIM
        q = _rope(_dot(a, win_ref[:, qo:qo + RET_QK_DIM]), cos, sin).astype(BF16)
        k = _rope(_dot(a, win_ref[:, ko:ko + RET_QK_DIM]), cos, sin) * (RET_QK_DIM ** -0.5)
        v = _dot(a, win_ref[:, vo:vo + RET_V_DIM]).astype(BF16)
        for c in range(RET_TM // RET_CHUNK):
            rows = slice(c * RET_CHUNK, (c + 1) * RET_CHUNK)
            qc, kc, vc = q[rows], k[rows], v[rows]
            st = state_ref[hd]
            scores = _dot_nt(qc, kc.astype(BF16)) * dmask_ref[hd]
            inner = _dot(scores.astype(BF16), vc)
            cross = _dot(qc, st.astype(BF16)) * qdec_ref[hd]
            state_ref[hd] = st * cdec_ref[hd] + _dot_tn((kc * kdec_ref[hd]).astype(BF16), vc)
            y = inner + cross
            y_ref[rows, hd * RET_V_DIM:(hd + 1) * RET_V_DIM] = y * lax.rsqrt(
                jnp.mean(y * y, axis=-1, keepdims=True) + EPS)
    gate = _dot(a, win_ref[:, RET_G_OFF:RET_G_OFF + RET_HEADS * RET_V_DIM])
    z = (gate * jax.nn.sigmoid(gate) * y_ref[...]).astype(BF16)
    o_ref[...] = x + _rms(_dot(z, wout_ref[...]), g1_ref[...])


def _retention_consts(seq_len):
    hh, c, dk = RET_HEADS, RET_CHUNK, RET_QK_DIM
    inv_freq = np.float32(ROPE_BASE) ** (-np.arange(0, dk, 2, dtype=np.float32) / np.float32(dk))
    ang = np.arange(seq_len, dtype=np.float32)[:, None] * inv_freq[None, :]
    cos = np.cos(ang.astype(np.float64)).astype(np.float32)
    sin = np.sin(ang.astype(np.float64)).astype(np.float32)
    log_gamma = jnp.log1p(-jnp.exp2(-5.0 - jnp.arange(hh, dtype=F32)))
    idx = jnp.arange(c, dtype=F32)
    diff = idx[:, None] - idx[None, :]
    dmask = jnp.where(diff >= 0, jnp.exp(log_gamma[:, None, None] * jnp.maximum(diff, 0.0)), 0.0)
    qdec = jnp.exp(log_gamma[:, None] * (idx[None, :] + 1.0))[:, :, None]
    kdec = jnp.exp(log_gamma[:, None] * (c - 1.0 - idx[None, :]))[:, :, None]
    cdec = jnp.exp(log_gamma * c)[:, None, None]
    return jnp.asarray(cos), jnp.asarray(sin), dmask, qdec, kdec, cdec


def _retention_mixer(h, gains, w_in, w_out, consts, layer, j):
    b, s, d = h.shape
    cos, sin, dmask, qdec, kdec, cdec = consts
    tile = lambda bi, t: (bi, t, 0)
    pos = lambda bi, t: (t, 0)
    gain = lambda k: _layer_resident(gains.shape, layer * N_NORMS + k)
    specs = (_weight(w_in, j), _weight(w_out, j))
    return pl.pallas_call(
        functools.partial(_retention_kernel, specs=specs),
        grid=(b, s // RET_TM),
        in_specs=[
            pl.BlockSpec((None, RET_TM, d), tile),
            pl.BlockSpec((RET_TM, RET_QK_DIM // 2), pos),
            pl.BlockSpec((RET_TM, RET_QK_DIM // 2), pos),
            gain(0), gain(1),
            HBM, HBM,
            _resident(dmask.shape), _resident(qdec.shape), _resident(kdec.shape), _resident(cdec.shape),
        ],
        out_specs=pl.BlockSpec((None, RET_TM, d), tile),
        out_shape=jax.ShapeDtypeStruct((b, s, d), F32),
        scratch_shapes=[
            pltpu.VMEM((RET_HEADS, RET_QK_DIM, RET_V_DIM), F32),
            pltpu.VMEM((RET_TM, RET_HEADS * RET_V_DIM), F32),
        ] + _weight_scratch(specs),
        compiler_params=_params("arbitrary", "arbitrary"),
        name="retention_mixer",
    )(h, cos, sin, gains, gains, w_in, w_out, dmask, qdec, kdec, cdec)


GM_TM = 512


def _gmlp_kernel(h_ref, g0_ref, g1_ref, win_hbm, lng_ref, lnb_ref, ws_ref, bs_ref, wout_hbm, o_ref, s_ref, *wscr,
                 specs):
    win_ref, wout_ref = _resident_weights(pl.program_id(0) == 0, specs, (win_hbm, wout_hbm), wscr)
    x = h_ref[...]
    a = _rms(x, g0_ref[...]).astype(BF16)

    def gelu(t):
        return 0.5 * t * (1.0 + lax.erf(t * (2.0 ** -0.5)))

    v = gelu(_dot(a, win_ref[:, GM_WIDTH:]))
    u = gelu(_dot(a, win_ref[:, :GM_WIDTH]))
    mu = jnp.mean(v, axis=-1, keepdims=True)
    vc = v - mu
    var = jnp.mean(vc * vc, axis=-1, keepdims=True)
    vn = (vc * lax.rsqrt(var + EPS) * lng_ref[...] + lnb_ref[...]).astype(BF16)
    row = lax.broadcasted_iota(jnp.int32, (GM_CHUNK, GM_CHUNK), 0)
    col = lax.broadcasted_iota(jnp.int32, (GM_CHUNK, GM_CHUNK), 1)
    causal = row >= col
    for g in range(GM_GROUPS):
        w = jnp.where(causal, ws_ref[g], 0.0).astype(BF16)
        cols = slice(g * GM_GROUP_DIM, (g + 1) * GM_GROUP_DIM)
        for c in range(GM_TM // GM_CHUNK):
            rows = slice(c * GM_CHUNK, (c + 1) * GM_CHUNK)
            s_ref[rows, cols] = _dot(w, vn[rows, cols]) + bs_ref[g]
    z = (u * s_ref[...]).astype(BF16)
    o_ref[...] = x + _rms(_dot(z, wout_ref[...]), g1_ref[...])


def _gmlp_mixer(h, gains, w_in, ln_g, ln_b, w_s, b_s, w_out, layer, j):
    n, d = h.shape
    row = lambda i: (i, 0)
    gain = lambda k: _layer_resident(gains.shape, layer * N_NORMS + k)
    specs = (_weight(w_in, j), _weight(w_out, j))
    return pl.pallas_call(
        functools.partial(_gmlp_kernel, specs=specs),
        grid=(n // GM_TM,),
        in_specs=[
            pl.BlockSpec((GM_TM, d), row),
            gain(0), gain(1),
            HBM, _layer_resident(ln_g.shape, j), _layer_resident(ln_b.shape, j),
            _layer_resident(w_s.shape, j), _layer_resident(b_s.shape, j), HBM,
        ],
        out_specs=pl.BlockSpec((GM_TM, d), row),
        out_shape=jax.ShapeDtypeStruct((n, d), F32),
        scratch_shapes=[pltpu.VMEM((GM_TM, GM_WIDTH), F32)] + _weight_scratch(specs),
        compiler_params=_params("arbitrary"),
        name="gmlp_mixer",
    )(h, gains, gains, w_in, ln_g, ln_b, w_s, b_s, w_out)


ATT_D = ATT_HEADS * ATT_HEAD_DIM


class _GroupPlan(NamedTuple):
    dilation: int
    view_minor: int
    block_rows: int
    halves: int
    n_units: int
    n_slots: int
    pieces: tuple
    first_units: tuple
    positions: np.ndarray

    @property
    def tokens_per_step(self):
        return self.n_units * ATT_BLK


def _group_plan(dilation):
    if dilation == 1:
        return _GroupPlan(1, 0, 8 * ATT_BLK, 1, 8, 1,
                          tuple(((None, u * ATT_BLK, ATT_BLK),) for u in range(8)), (0,),
                          np.arange(ATT_BLK))
    if dilation == 4:
        half = ATT_BLK // 2
        return _GroupPlan(4, 8, ATT_BLK, 1, 8, 4,
                          tuple(((c, blk * half, half), (c + 4, blk * half, half))
                                for blk in range(2) for c in range(4)),
                          (0, 1, 2, 3),
                          np.concatenate([2 * np.arange(half), 2 * np.arange(half) + 1]))
    assert dilation == 16
    return _GroupPlan(16, 16, ATT_BLK, 2, 8, 8,
                      tuple(((c, 0, ATT_BLK),) for c in range(8)), tuple(range(8)),
                      np.arange(ATT_BLK))


def _t5_bucket(dist):
    d = jnp.maximum(dist, 1).astype(F32)
    large = REL_MAX_EXACT + (jnp.log(d / REL_MAX_EXACT) / math.log(REL_MAX_DIST / REL_MAX_EXACT)
                             * (REL_BUCKETS - REL_MAX_EXACT)).astype(jnp.int32)
    large = jnp.minimum(large, REL_BUCKETS - 1)
    return jnp.where(dist < REL_MAX_EXACT, dist, large)


def _bias_kernel(table_ref, bucket_ref, dist_ref, o_ref):
    g = pl.program_id(0)
    bucket = bucket_ref[...]
    dist = dist_ref[...]
    band = (dist >= 0) & (dist <= ATT_BLK)
    for hd in range(ATT_HEADS):
        acc = jnp.zeros(bucket.shape, F32)
        for b in range(REL_BUCKETS):
            acc = jnp.where(bucket == b, table_ref[b, g * ATT_HEADS + hd], acc)
        o_ref[hd] = jnp.where(band, acc * LOG2E, MASK_VALUE)


def _attention_bias(rel_bias, plans):
    dists = []
    for plan in plans:
        pos_k = np.concatenate([plan.positions - ATT_BLK, plan.positions])
        dists.append(plan.positions[:, None] - pos_k[None, :])
    dist = jnp.asarray(np.stack(dists), jnp.int32)
    scale = jnp.asarray([plan.dilation for plan in plans], jnp.int32)[:, None, None]
    buckets = _t5_bucket(jnp.maximum(dist, 0) * scale).astype(jnp.int32)
    tile = pl.BlockSpec((None, ATT_BLK, 2 * ATT_BLK), lambda g: (g, 0, 0))
    return pl.pallas_call(
        _bias_kernel,
        grid=(len(plans),),
        in_specs=[pl.BlockSpec(memory_space=pltpu.SMEM), tile, tile],
        out_specs=pl.BlockSpec((None, ATT_HEADS, ATT_BLK, 2 * ATT_BLK), lambda g: (g, 0, 0, 0)),
        out_shape=jax.ShapeDtypeStruct((len(plans), ATT_HEADS, ATT_BLK, 2 * ATT_BLK), F32),
        compiler_params=_params("arbitrary"),
        name="attention_bias",
    )(rel_bias, buckets, dist)


def _attn_units(plan, t, xa, g0_ref, w_ref, bias_ref, k_ref, v_ref, q_ref, put_o, put_lse):
    @pl.when(t == 0)
    def _():
        k_ref[:, :ATT_BLK, :] = jnp.zeros((plan.n_slots, ATT_BLK, ATT_D), BF16)
        v_ref[:, :ATT_BLK, :] = jnp.zeros((plan.n_slots, ATT_BLK, ATT_D), BF16)

    a = _rms(xa, g0_ref[...]).astype(BF16)
    q_ref[...] = (_dot(a, w_ref[:, :ATT_D]) * (ATT_HEAD_DIM ** -0.5 * LOG2E)).astype(BF16)
    kc = _dot(a, w_ref[:, ATT_D:2 * ATT_D]).astype(BF16)
    vc = _dot(a, w_ref[:, 2 * ATT_D:]).astype(BF16)

    lane = lax.broadcasted_iota(jnp.int32, (ATT_BLK, V7X_LANES), 1)
    low = lane < ATT_HEAD_DIM
    kcol = lax.broadcasted_iota(jnp.int32, (ATT_BLK, 2 * ATT_BLK), 1)
    first_pen = jnp.where((kcol < ATT_BLK) & (t == 0), MASK_VALUE, 0.0)
    for u in range(plan.n_units):
        slot = u % plan.n_slots
        rows = slice(u * ATT_BLK, (u + 1) * ATT_BLK)
        k_ref[slot, ATT_BLK:, :] = kc[rows]
        v_ref[slot, ATT_BLK:, :] = vc[rows]
        m_tile = jnp.zeros((ATT_BLK, V7X_LANES), F32)
        den_tile = jnp.ones((ATT_BLK, V7X_LANES), F32)
        for hp in range(ATT_HEADS // 2):
            cols = slice(hp * V7X_LANES, (hp + 1) * V7X_LANES)
            qp = q_ref[rows, cols]
            kk = k_ref[slot, :, cols]
            vv = v_ref[slot, :, cols]
            outs, lses = [], []
            for sub in range(2):
                qm = jnp.where(low if sub == 0 else ~low, qp, jnp.zeros_like(qp))
                s = _dot_nt(qm, kk) + bias_ref[2 * hp + sub]
                if u in plan.first_units:
                    s = s + first_pen
                m = jnp.max(s, axis=-1, keepdims=True)
                e = jnp.exp2(s - m)
                den = jnp.sum(e, axis=-1, keepdims=True)
                outs.append(_dot(e.astype(BF16), vv) / den)
                lses.append((m, den))
            put_o(rows, cols, jnp.where(low, outs[0], outs[1]))
            m_tile = jnp.where(lane == 2 * hp, lses[0][0], jnp.where(lane == 2 * hp + 1, lses[1][0], m_tile))
            den_tile = jnp.where(lane == 2 * hp, lses[0][1], jnp.where(lane == 2 * hp + 1, lses[1][1], den_tile))
        put_lse(rows, m_tile * LN2 + jnp.log(den_tile))
        k_ref[slot, :ATT_BLK, :] = k_ref[slot, ATT_BLK:, :]
        v_ref[slot, :ATT_BLK, :] = v_ref[slot, ATT_BLK:, :]


def _attn_dense_kernel(x_ref, g0_ref, w_hbm, bias_ref, o_ref, lse_ref, k_ref, v_ref, q_ref, *wscr, plan, specs):
    (w_ref,) = _resident_weights((pl.program_id(0) == 0) & (pl.program_id(2) == 0), specs, (w_hbm,), wscr)

    def put_o(rows, cols, tile):
        o_ref[rows, cols] = tile

    def put_lse(rows, tile):
        lse_ref[rows, :] = tile

    _attn_units(plan, pl.program_id(2), x_ref[...], g0_ref, w_ref, bias_ref, k_ref, v_ref, q_ref, put_o, put_lse)


def _attn_strided_kernel(x_hbm, g0_ref, w_hbm, bias_ref, o_hbm, lse_hbm, k_ref, v_ref, q_ref,
                         xbuf, obuf, lbuf, xsem, osem, lsem, *wscr, plan, specs, n_batch, n_steps):
    b, hf, t = pl.program_id(0), pl.program_id(1), pl.program_id(2)
    step = (b * plan.halves + hf) * n_steps + t
    n_total = n_batch * plan.halves * n_steps
    buf = lax.rem(step, 2)

    def class_rows(ref, bb, hh, tt, sub, r0, n):
        return ref.at[bb, pl.ds(tt * plan.block_rows + r0, n), hh * V7X_SUBLANES + sub]

    def copies(kind, bb, hh, tt, which):
        out = []
        for u, pcs in enumerate(plan.pieces):
            at = u * ATT_BLK
            for sub, r0, n in pcs:
                dense = pl.ds(at, n)
                if kind == "gather":
                    out.append(pltpu.make_async_copy(class_rows(x_hbm, bb, hh, tt, sub, r0, n),
                                                     xbuf.at[which, dense], xsem.at[which]))
                else:
                    out.append(pltpu.make_async_copy(obuf.at[which, dense],
                                                     class_rows(o_hbm, bb, hh, tt, sub, r0, n), osem.at[which]))
                    out.append(pltpu.make_async_copy(lbuf.at[which, dense],
                                                     class_rows(lse_hbm, bb, hh, tt, sub, r0, n), lsem.at[which]))
                at += n
        return out

    @pl.when(step == 0)
    def _():
        for cp in copies("gather", b, hf, t, buf):
            cp.start()

    wrap_t = t + 1 == n_steps
    wrap_h = hf + 1 == plan.halves
    nt = jnp.where(wrap_t, 0, t + 1)
    nh = jnp.where(wrap_t, jnp.where(wrap_h, 0, hf + 1), hf)
    nb = jnp.where(wrap_t & wrap_h, b + 1, b)

    @pl.when(step + 1 < n_total)
    def _():
        for cp in copies("gather", nb, nh, nt, 1 - buf):
            cp.start()

    (w_ref,) = _resident_weights(step == 0, specs, (w_hbm,), wscr)

    for cp in copies("gather", b, hf, t, buf):
        cp.wait()

    @pl.when(step >= 2)
    def _():
        for cp in copies("scatter", b, hf, t, buf):
            cp.wait()

    def put_o(rows, cols, tile):
        obuf[buf, rows, cols] = tile

    def put_lse(rows, tile):
        lbuf[buf, rows, :] = tile

    _attn_units(plan, t, xbuf[buf], g0_ref, w_ref, bias_ref, k_ref, v_ref, q_ref, put_o, put_lse)

    for cp in copies("scatter", b, hf, t, buf):
        cp.start()

    @pl.when(step == n_total - 1)
    def _():
        for cp in copies("scatter", b, hf, t, 1 - buf):
            cp.wait()
        for cp in copies("scatter", b, hf, t, buf):
            cp.wait()


def _attention_group(h, gains, w_in, bias, plan, group, layer, j):
    b, s, d = h.shape
    tokens = plan.tokens_per_step
    specs = (_weight(w_in, j, col0=group * 3 * ATT_D, cols=3 * ATT_D),)
    resident = [
        _layer_resident(gains.shape, layer * N_NORMS),
        HBM,
        pl.BlockSpec((None, ATT_HEADS, ATT_BLK, 2 * ATT_BLK), lambda bi, hf, t: (group, 0, 0, 0),
                     pipeline_mode=pl.Buffered(1)),
    ]
    kv_q = [
        pltpu.VMEM((plan.n_slots, 2 * ATT_BLK, ATT_D), BF16),
        pltpu.VMEM((plan.n_slots, 2 * ATT_BLK, ATT_D), BF16),
        pltpu.VMEM((tokens, ATT_D), BF16),
    ]
    if not plan.view_minor:
        steps = s // plan.block_rows
        block = lambda width: pl.BlockSpec((None, plan.block_rows, width), lambda bi, hf, t: (bi, t, 0))
        o, lse = pl.pallas_call(
            functools.partial(_attn_dense_kernel, plan=plan, specs=specs),
            grid=(b, 1, steps),
            in_specs=[block(d)] + resident,
            out_specs=[block(ATT_D), block(V7X_LANES)],
            out_shape=[jax.ShapeDtypeStruct((b, s, ATT_D), F32), jax.ShapeDtypeStruct((b, s, V7X_LANES), F32)],
            scratch_shapes=kv_q + _weight_scratch(specs),
            compiler_params=_params("arbitrary", "arbitrary", "arbitrary"),
            name=f"attention_group{group}",
        )(h, gains, w_in, bias)
    else:
        lead = s // plan.view_minor
        steps = lead // plan.block_rows
        assert b * plan.halves * steps >= 2
        view = lambda width: (b, lead, plan.view_minor, width)
        o, lse = pl.pallas_call(
            functools.partial(_attn_strided_kernel, plan=plan, specs=specs, n_batch=b, n_steps=steps),
            grid=(b, plan.halves, steps),
            in_specs=[HBM] + resident,
            out_specs=[HBM, HBM],
            out_shape=[jax.ShapeDtypeStruct(view(ATT_D), F32), jax.ShapeDtypeStruct(view(V7X_LANES), F32)],
            scratch_shapes=kv_q + [
                pltpu.VMEM((2, tokens, d), F32),
                pltpu.VMEM((2, tokens, ATT_D), F32),
                pltpu.VMEM((2, tokens, V7X_LANES), F32),
                pltpu.SemaphoreType.DMA((2,)), pltpu.SemaphoreType.DMA((2,)), pltpu.SemaphoreType.DMA((2,)),
            ] + _weight_scratch(specs),
            compiler_params=_params("arbitrary", "arbitrary", "arbitrary"),
            name=f"attention_group{group}",
        )(h.reshape(view(d)), gains, w_in, bias)
    return o.reshape(b * s, ATT_D), lse.reshape(b * s, V7X_LANES)


ATT_MERGE_TM = 512


def _attn_merge_kernel(h_ref, o0_ref, o1_ref, o2_ref, l0_ref, l1_ref, l2_ref, g1_ref, wout_hbm, expand_ref,
                       out_ref, *wscr, specs):
    (wout_ref,) = _resident_weights(pl.program_id(0) == 0, specs, (wout_hbm,), wscr)
    l0, l1, l2 = l0_ref[...], l1_ref[...], l2_ref[...]
    m = jnp.maximum(jnp.maximum(l0, l1), l2)
    e0, e1, e2 = jnp.exp(l0 - m), jnp.exp(l1 - m), jnp.exp(l2 - m)
    inv = 1.0 / (e0 + e1 + e2)
    expand = expand_ref[...]

    def per_channel(w):
        hi = w.astype(BF16)
        lo = (w - hi.astype(F32)).astype(BF16)
        return _dot(hi, expand) + _dot(lo, expand)

    o = (per_channel(e0 * inv) * o0_ref[...] + per_channel(e1 * inv) * o1_ref[...]
         + per_channel(e2 * inv) * o2_ref[...])
    out_ref[...] = h_ref[...] + _rms(_dot(o.astype(BF16), wout_ref[...]), g1_ref[...])


def _attention_merge(h, outs, lses, gains, w_out, layer, j):
    n, d = h.shape
    row = lambda i: (i, 0)
    tile = pl.BlockSpec((ATT_MERGE_TM, d), row)
    lse_tile = pl.BlockSpec((ATT_MERGE_TM, V7X_LANES), row)
    expand = np.zeros((V7X_LANES, ATT_D), np.float32)
    for hd in range(ATT_HEADS):
        expand[hd, hd * ATT_HEAD_DIM:(hd + 1) * ATT_HEAD_DIM] = 1.0
    expand = jnp.asarray(expand, BF16)
    specs = (_weight(w_out, j),)
    return pl.pallas_call(
        functools.partial(_attn_merge_kernel, specs=specs),
        grid=(n // ATT_MERGE_TM,),
        in_specs=[tile] * 4 + [lse_tile] * 3 + [
            _layer_resident(gains.shape, layer * N_NORMS + 1), HBM, _resident(expand.shape)],
        out_specs=tile,
        out_shape=jax.ShapeDtypeStruct((n, d), F32),
        scratch_shapes=_weight_scratch(specs),
        compiler_params=_params("arbitrary"),
        name="attention_merge",
    )(h, *outs, *lses, gains, w_out, expand)


def kernel(x, p, norm_g, ret_w_in, ret_w_out, attn_w_in, attn_w_out, rel_bias, gm_w_in, gm_ln_g, gm_ln_b,
           gm_w_s, gm_b_s, gm_w_out, ffn_w_in, ffn_w_out, ple_w_proj, ple_w_gate):
    b, s, d = x.shape
    n = b * s
    gains = norm_g.reshape(DEPTH * N_NORMS, 1, d)
    p = p.reshape(DEPTH, n, PLE_DIM)
    gm_ln_g = gm_ln_g.reshape(-1, 1, GM_WIDTH)
    gm_ln_b = gm_ln_b.reshape(-1, 1, GM_WIDTH)
    gm_b_s = gm_b_s.reshape(-1, GM_GROUPS, GM_CHUNK, 1)
    ret_consts = _retention_consts(s)
    plans = tuple(_group_plan(dil) for _, dil in DILATION_PAIRS)

    h = x
    for i in range(DEPTH):
        kind, j = i % N_MIXERS, i // N_MIXERS
        if kind == 0:
            h = _retention_mixer(h.reshape(b, s, d), gains, ret_w_in, ret_w_out, ret_consts, i, j)
        elif kind == 1:
            bias = _attention_bias(rel_bias, plans)
            outs, lses = [], []
            for gi, plan in enumerate(plans):
                o, lse = _attention_group(h.reshape(b, s, d), gains, attn_w_in, bias, plan, gi, i, j)
                outs.append(o)
                lses.append(lse)
            h = _attention_merge(h.reshape(n, d), outs, lses, gains, attn_w_out, i, j)
        else:
            h = _gmlp_mixer(h.reshape(n, d), gains, gm_w_in, gm_ln_g, gm_ln_b, gm_w_s, gm_b_s, gm_w_out, i, j)
        h = _ffn_ple(h.reshape(n, d), p, gains, ffn_w_in, ffn_w_out, ple_w_proj, ple_w_gate, i)
    return h.reshape(b, s, d)
```

```python
import functools
import math
from typing import NamedTuple

import jax
import jax.numpy as jnp
import numpy as np
from jax import lax
from jax.experimental import pallas as pl
from jax.experimental.pallas import tpu as pltpu

D_MODEL = 1024
DEPTH = 4
N_MIXERS = 3
N_NORMS = 5
PLE_DIM = 256
EPS = 1e-6

RET_HEADS = 4
RET_QK_DIM = 256
RET_V_DIM = 512
RET_CHUNK = 256
ROPE_BASE = 10000.0

ATT_HEADS = 16
ATT_HEAD_DIM = 64
DILATION_PAIRS = ((128, 1), (512, 4), (2048, 16))
ATT_BLK = 128
REL_BUCKETS = 32
REL_MAX_EXACT = 16
REL_MAX_DIST = 2048

GM_CHUNK = 128
GM_WIDTH = 2 * D_MODEL
GM_GROUPS = 8
GM_GROUP_DIM = GM_WIDTH // GM_GROUPS

FFN_HIDDEN = 2816

V7X_LANES = 128
V7X_SUBLANES = 8
V7X_VMEM_BYTES = 64 * 1024 * 1024
VMEM_LIMIT_BYTES = V7X_VMEM_BYTES - 8 * 1024 * 1024

MASK_VALUE = -1e30
LOG2E = math.log2(math.e)
LN2 = math.log(2.0)

BF16 = jnp.bfloat16
F32 = jnp.float32


def _resident(shape):
    return pl.BlockSpec(shape, lambda *_: (0,) * len(shape), pipeline_mode=pl.Buffered(1))


def _layer_resident(shape, layer):
    rest = tuple(shape[1:])
    return pl.BlockSpec((None,) + rest, lambda *_: (layer,) + (0,) * len(rest), pipeline_mode=pl.Buffered(1))


def _rms(x, g):
    return x * lax.rsqrt(jnp.mean(x * x, axis=-1, keepdims=True) + EPS) * g


def _dot(a, b):
    return jnp.dot(a, b, preferred_element_type=F32)


def _dot_nt(a, b):
    return lax.dot_general(a, b, (((1,), (1,)), ((), ())), preferred_element_type=F32)


def _dot_tn(a, b):
    return lax.dot_general(a, b, (((0,), (0,)), ((), ())), preferred_element_type=F32)


def _params(*semantics):
    return pltpu.CompilerParams(dimension_semantics=semantics, vmem_limit_bytes=VMEM_LIMIT_BYTES)


WEIGHT_CHUNK = (256, 512)
WEIGHT_SLOTS = 12

HBM = pl.BlockSpec(memory_space=pl.ANY)


class _WeightSpec(NamedTuple):
    layer: int
    rows: int
    cols: int
    col0: int


def _weight(array, layer, col0=0, cols=None):
    spec = _WeightSpec(layer, array.shape[1], array.shape[2] if cols is None else cols, col0)
    assert spec.rows % WEIGHT_CHUNK[0] == 0 and spec.cols % WEIGHT_CHUNK[1] == 0
    return spec


def _weight_scratch(specs):
    return ([pltpu.VMEM((spec.rows, spec.cols), BF16) for spec in specs]
            + [pltpu.VMEM((WEIGHT_SLOTS,) + WEIGHT_CHUNK, F32), pltpu.SemaphoreType.DMA((WEIGHT_SLOTS,))])


def _resident_weights(first, specs, hbm_refs, scratch):
    *w_refs, stage, sem = scratch
    cr, cc = WEIGHT_CHUNK
    chunks = [(w_hbm, w_ref, spec, r, c)
              for spec, w_hbm, w_ref in zip(specs, hbm_refs, w_refs)
              for r in range(0, spec.rows, cr) for c in range(0, spec.cols, cc)]

    def copy(k):
        w_hbm, _, spec, r, c = chunks[k]
        src = w_hbm.at[spec.layer, pl.ds(r, cr), pl.ds(spec.col0 + c, cc)]
        return pltpu.make_async_copy(src, stage.at[k % WEIGHT_SLOTS], sem.at[k % WEIGHT_SLOTS])

    @pl.when(first)
    def _():
        for k in range(min(WEIGHT_SLOTS, len(chunks))):
            copy(k).start()
        for k, (_, w_ref, _, r, c) in enumerate(chunks):
            copy(k).wait()
            w_ref[r:r + cr, c:c + cc] = stage[k % WEIGHT_SLOTS].astype(BF16)
            if k + WEIGHT_SLOTS < len(chunks):
                copy(k + WEIGHT_SLOTS).start()

    return w_refs


FFN_TM = 1024
FFN_SUB = 512


def _ffn_ple_kernel(h_ref, p_ref, g2_ref, g3_ref, g4_ref, win_hbm, wout_hbm, wp_hbm, wg_hbm, o_ref, *wscr, specs):
    win_ref, wout_ref, wp_ref, wg_ref = _resident_weights(
        pl.program_id(0) == 0, specs, (win_hbm, wout_hbm, wp_hbm, wg_hbm), wscr)
    for part in range(FFN_TM // FFN_SUB):
        rows = slice(part * FFN_SUB, (part + 1) * FFN_SUB)
        x = h_ref[rows, :]
        a = _rms(x, g2_ref[...]).astype(BF16)
        gate = _dot(a, win_ref[:, :FFN_HIDDEN])
        up = _dot(a, win_ref[:, FFN_HIDDEN:])
        hid = (gate * jax.nn.sigmoid(gate) * up).astype(BF16)
        h1 = x + _rms(_dot(hid, wout_ref[...]), g3_ref[...])
        emb = _dot(p_ref[rows, :].astype(BF16), wp_ref[...])
        gate = jax.nn.sigmoid(_dot(_rms(h1, g4_ref[...]).astype(BF16), wg_ref[...]))
        o_ref[rows, :] = h1 + gate * emb


def _ffn_ple(h, p, gains, w_in, w_out, w_proj, w_gate, layer):
    n, d = h.shape
    row = lambda i: (i, 0)
    gain = lambda k: _layer_resident(gains.shape, layer * N_NORMS + k)
    specs = tuple(_weight(w, layer) for w in (w_in, w_out, w_proj, w_gate))
    return pl.pallas_call(
        functools.partial(_ffn_ple_kernel, specs=specs),
        grid=(n // FFN_TM,),
        in_specs=[
            pl.BlockSpec((FFN_TM, d), row),
            pl.BlockSpec((None, FFN_TM, PLE_DIM), lambda i: (layer, i, 0)),
            gain(2), gain(3), gain(4),
            HBM, HBM, HBM, HBM,
        ],
        out_specs=pl.BlockSpec((FFN_TM, d), row),
        out_shape=jax.ShapeDtypeStruct((n, d), F32),
        scratch_shapes=_weight_scratch(specs),
        compiler_params=_params("arbitrary"),
        name="ffn_ple",
    )(h, p, gains, gains, gains, w_in, w_out, w_proj, w_gate)


RET_TM = 512
RET_Q_OFF = 0
RET_K_OFF = RET_HEADS * RET_QK_DIM
RET_V_OFF = 2 * RET_HEADS * RET_QK_DIM
RET_G_OFF = RET_V_OFF + RET_HEADS * RET_V_DIM


def _rope(t, cos, sin):
    half = RET_QK_DIM // 2
    t1, t2 = t[:, :half], t[:, half:]
    return jnp.concatenate([t1 * cos - t2 * sin, t2 * cos + t1 * sin], axis=-1)


def _retention_kernel(h_ref, cos_ref, sin_ref, g0_ref, g1_ref, win_hbm, wout_hbm,
                      dmask_ref, qdec_ref, kdec_ref, cdec_ref, o_ref, state_ref, y_ref, *wscr, specs):
    win_ref, wout_ref = _resident_weights(
        (pl.program_id(0) == 0) & (pl.program_id(1) == 0), specs, (win_hbm, wout_hbm), wscr)

    @pl.when(pl.program_id(1) == 0)
    def _():
        state_ref[...] = jnp.zeros(state_ref.shape, F32)

    x = h_ref[...]
    a = _rms(x, g0_ref[...]).astype(BF16)
    cos, sin = cos_ref[...], sin_ref[...]
    for hd in range(RET_HEADS):
        qo, ko, vo = RET_Q_OFF + hd * RET_QK_DIM, RET_K_OFF + hd * RET_QK_DIM, RET_V_OFF + hd * RET_V_DIM
        q = _rope(_dot(a, win_ref[:, qo:qo + RET_QK_DIM]), cos, sin).astype(BF16)
        k = _rope(_dot(a, win_ref[:, ko:ko + RET_QK_DIM]), cos, sin) * (RET_QK_DIM ** -0.5)
        v = _dot(a, win_ref[:, vo:vo + RET_V_DIM]).astype(BF16)
        for c in range(RET_TM // RET_CHUNK):
            rows = slice(c * RET_CHUNK, (c + 1) * RET_CHUNK)
            qc, kc, vc = q[rows], k[rows], v[rows]
            st = state_ref[hd]
            scores = _dot_nt(qc, kc.astype(BF16)) * dmask_ref[hd]
            inner = _dot(scores.astype(BF16), vc)
            cross = _dot(qc, st.astype(BF16)) * qdec_ref[hd]
            state_ref[hd] = st * cdec_ref[hd] + _dot_tn((kc * kdec_ref[hd]).astype(BF16), vc)
            y = inner + cross
            y_ref[rows, hd * RET_V_DIM:(hd + 1) * RET_V_DIM] = y * lax.rsqrt(
                jnp.mean(y * y, axis=-1, keepdims=True) + EPS)
    gate = _dot(a, win_ref[:, RET_G_OFF:RET_G_OFF + RET_HEADS * RET_V_DIM])
    z = (gate * jax.nn.sigmoid(gate) * y_ref[...]).astype(BF16)
    o_ref[...] = x + _rms(_dot(z, wout_ref[...]), g1_ref[...])


def _retention_consts(seq_len):
    hh, c, dk = RET_HEADS, RET_CHUNK, RET_QK_DIM
    inv_freq = np.float32(ROPE_BASE) ** (-np.arange(0, dk, 2, dtype=np.float32) / np.float32(dk))
    ang = np.arange(seq_len, dtype=np.float32)[:, None] * inv_freq[None, :]
    cos = np.cos(ang.astype(np.float64)).astype(np.float32)
    sin = np.sin(ang.astype(np.float64)).astype(np.float32)
    log_gamma = jnp.log1p(-jnp.exp2(-5.0 - jnp.arange(hh, dtype=F32)))
    idx = jnp.arange(c, dtype=F32)
    diff = idx[:, None] - idx[None, :]
    dmask = jnp.where(diff >= 0, jnp.exp(log_gamma[:, None, None] * jnp.maximum(diff, 0.0)), 0.0)
    qdec = jnp.exp(log_gamma[:, None] * (idx[None, :] + 1.0))[:, :, None]
    kdec = jnp.exp(log_gamma[:, None] * (c - 1.0 - idx[None, :]))[:, :, None]
    cdec = jnp.exp(log_gamma * c)[:, None, None]
    return jnp.asarray(cos), jnp.asarray(sin), dmask, qdec, kdec, cdec


def _retention_mixer(h, gains, w_in, w_out, consts, layer, j):
    b, s, d = h.shape
    cos, sin, dmask, qdec, kdec, cdec = consts
    tile = lambda bi, t: (bi, t, 0)
    pos = lambda bi, t: (t, 0)
    gain = lambda k: _layer_resident(gains.shape, layer * N_NORMS + k)
    specs = (_weight(w_in, j), _weight(w_out, j))
    return pl.pallas_call(
        functools.partial(_retention_kernel, specs=specs),
        grid=(b, s // RET_TM),
        in_specs=[
            pl.BlockSpec((None, RET_TM, d), tile),
            pl.BlockSpec((RET_TM, RET_QK_DIM // 2), pos),
            pl.BlockSpec((RET_TM, RET_QK_DIM // 2), pos),
            gain(0), gain(1),
            HBM, HBM,
            _resident(dmask.shape), _resident(qdec.shape), _resident(kdec.shape), _resident(cdec.shape),
        ],
        out_specs=pl.BlockSpec((None, RET_TM, d), tile),
        out_shape=jax.ShapeDtypeStruct((b, s, d), F32),
        scratch_shapes=[
            pltpu.VMEM((RET_HEADS, RET_QK_DIM, RET_V_DIM), F32),
            pltpu.VMEM((RET_TM, RET_HEADS * RET_V_DIM), F32),
        ] + _weight_scratch(specs),
        compiler_params=_params("arbitrary", "arbitrary"),
        name="retention_mixer",
    )(h, cos, sin, gains, gains, w_in, w_out, dmask, qdec, kdec, cdec)


GM_TM = 512


def _gmlp_kernel(h_ref, g0_ref, g1_ref, win_hbm, lng_ref, lnb_ref, ws_ref, bs_ref, wout_hbm, o_ref, s_ref, *wscr,
                 specs):
    win_ref, wout_ref = _resident_weights(pl.program_id(0) == 0, specs, (win_hbm, wout_hbm), wscr)
    x = h_ref[...]
    a = _rms(x, g0_ref[...]).astype(BF16)

    def gelu(t):
        return 0.5 * t * (1.0 + lax.erf(t * (2.0 ** -0.5)))

    v = gelu(_dot(a, win_ref[:, GM_WIDTH:]))
    u = gelu(_dot(a, win_ref[:, :GM_WIDTH]))
    mu = jnp.mean(v, axis=-1, keepdims=True)
    vc = v - mu
    var = jnp.mean(vc * vc, axis=-1, keepdims=True)
    vn = (vc * lax.rsqrt(var + EPS) * lng_ref[...] + lnb_ref[...]).astype(BF16)
    row = lax.broadcasted_iota(jnp.int32, (GM_CHUNK, GM_CHUNK), 0)
    col = lax.broadcasted_iota(jnp.int32, (GM_CHUNK, GM_CHUNK), 1)
    causal = row >= col
    for g in range(GM_GROUPS):
        w = jnp.where(causal, ws_ref[g], 0.0).astype(BF16)
        cols = slice(g * GM_GROUP_DIM, (g + 1) * GM_GROUP_DIM)
        for c in range(GM_TM // GM_CHUNK):
            rows = slice(c * GM_CHUNK, (c + 1) * GM_CHUNK)
            s_ref[rows, cols] = _dot(w, vn[rows, cols]) + bs_ref[g]
    z = (u * s_ref[...]).astype(BF16)
    o_ref[...] = x + _rms(_dot(z, wout_ref[...]), g1_ref[...])


def _gmlp_mixer(h, gains, w_in, ln_g, ln_b, w_s, b_s, w_out, layer, j):
    n, d = h.shape
    row = lambda i: (i, 0)
    gain = lambda k: _layer_resident(gains.shape, layer * N_NORMS + k)
    specs = (_weight(w_in, j), _weight(w_out, j))
    return pl.pallas_call(
        functools.partial(_gmlp_kernel, specs=specs),
        grid=(n // GM_TM,),
        in_specs=[
            pl.BlockSpec((GM_TM, d), row),
            gain(0), gain(1),
            HBM, _layer_resident(ln_g.shape, j), _layer_resident(ln_b.shape, j),
            _layer_resident(w_s.shape, j), _layer_resident(b_s.shape, j), HBM,
        ],
        out_specs=pl.BlockSpec((GM_TM, d), row),
        out_shape=jax.ShapeDtypeStruct((n, d), F32),
        scratch_shapes=[pltpu.VMEM((GM_TM, GM_WIDTH), F32)] + _weight_scratch(specs),
        compiler_params=_params("arbitrary"),
        name="gmlp_mixer",
    )(h, gains, gains, w_in, ln_g, ln_b, w_s, b_s, w_out)


ATT_D = ATT_HEADS * ATT_HEAD_DIM


class _GroupPlan(NamedTuple):
    dilation: int
    view_minor: int
    block_rows: int
    halves: int
    n_units: int
    n_slots: int
    pieces: tuple
    first_units: tuple
    positions: np.ndarray

    @property
    def tokens_per_step(self):
        return self.n_units * ATT_BLK


def _group_plan(dilation):
    if dilation == 1:
        return _GroupPlan(1, 0, 8 * ATT_BLK, 1, 8, 1,
                          tuple(((None, u * ATT_BLK, ATT_BLK),) for u in range(8)), (0,),
                          np.arange(ATT_BLK))
    if dilation == 4:
        half = ATT_BLK // 2
        return _GroupPlan(4, 8, ATT_BLK, 1, 8, 4,
                          tuple(((c, blk * half, half), (c + 4, blk * half, half))
                                for blk in range(2) for c in range(4)),
                          (0, 1, 2, 3),
                          np.concatenate([2 * np.arange(half), 2 * np.arange(half) + 1]))
    assert dilation == 16
    return _GroupPlan(16, 16, ATT_BLK, 2, 8, 8,
                      tuple(((c, 0, ATT_BLK),) for c in range(8)), tuple(range(8)),
                      np.arange(ATT_BLK))


def _t5_bucket(dist):
    d = jnp.maximum(dist, 1).astype(F32)
    large = REL_MAX_EXACT + (jnp.log(d / REL_MAX_EXACT) / math.log(REL_MAX_DIST / REL_MAX_EXACT)
                             * (REL_BUCKETS - REL_MAX_EXACT)).astype(jnp.int32)
    large = jnp.minimum(large, REL_BUCKETS - 1)
    return jnp.where(dist < REL_MAX_EXACT, dist, large)


def _bias_kernel(table_ref, bucket_ref, dist_ref, o_ref):
    g = pl.program_id(0)
    bucket = bucket_ref[...]
    dist = dist_ref[...]
    band = (dist >= 0) & (dist <= ATT_BLK)
    for hd in range(ATT_HEADS):
        acc = jnp.zeros(bucket.shape, F32)
        for b in range(REL_BUCKETS):
            acc = jnp.where(bucket == b, table_ref[b, g * ATT_HEADS + hd], acc)
        o_ref[hd] = jnp.where(band, acc * LOG2E, MASK_VALUE)


def _attention_bias(rel_bias, plans):
    dists = []
    for plan in plans:
        pos_k = np.concatenate([plan.positions - ATT_BLK, plan.positions])
        dists.append(plan.positions[:, None] - pos_k[None, :])
    dist = jnp.asarray(np.stack(dists), jnp.int32)
    scale = jnp.asarray([plan.dilation for plan in plans], jnp.int32)[:, None, None]
    buckets = _t5_bucket(jnp.maximum(dist, 0) * scale).astype(jnp.int32)
    tile = pl.BlockSpec((None, ATT_BLK, 2 * ATT_BLK), lambda g: (g, 0, 0))
    return pl.pallas_call(
        _bias_kernel,
        grid=(len(plans),),
        in_specs=[pl.BlockSpec(memory_space=pltpu.SMEM), tile, tile],
        out_specs=pl.BlockSpec((None, ATT_HEADS, ATT_BLK, 2 * ATT_BLK), lambda g: (g, 0, 0, 0)),
        out_shape=jax.ShapeDtypeStruct((len(plans), ATT_HEADS, ATT_BLK, 2 * ATT_BLK), F32),
        compiler_params=_params("arbitrary"),
        name="attention_bias",
    )(rel_bias, buckets, dist)


def _attn_units(plan, t, xa, g0_ref, w_ref, bias_ref, k_ref, v_ref, q_ref, put_o, put_lse):
    @pl.when(t == 0)
    def _():
        k_ref[:, :ATT_BLK, :] = jnp.zeros((plan.n_slots, ATT_BLK, ATT_D), BF16)
        v_ref[:, :ATT_BLK, :] = jnp.zeros((plan.n_slots, ATT_BLK, ATT_D), BF16)

    a = _rms(xa, g0_ref[...]).astype(BF16)
    q_ref[...] = (_dot(a, w_ref[:, :ATT_D]) * (ATT_HEAD_DIM ** -0.5 * LOG2E)).astype(BF16)
    kc = _dot(a, w_ref[:, ATT_D:2 * ATT_D]).astype(BF16)
    vc = _dot(a, w_ref[:, 2 * ATT_D:]).astype(BF16)

    lane = lax.broadcasted_iota(jnp.int32, (ATT_BLK, V7X_LANES), 1)
    low = lane < ATT_HEAD_DIM
    kcol = lax.broadcasted_iota(jnp.int32, (ATT_BLK, 2 * ATT_BLK), 1)
    first_pen = jnp.where((kcol < ATT_BLK) & (t == 0), MASK_VALUE, 0.0)
    for u in range(plan.n_units):
        slot = u % plan.n_slots
        rows = slice(u * ATT_BLK, (u + 1) * ATT_BLK)
        k_ref[slot, ATT_BLK:, :] = kc[rows]
        v_ref[slot, ATT_BLK:, :] = vc[rows]
        m_tile = jnp.zeros((ATT_BLK, V7X_LANES), F32)
        den_tile = jnp.ones((ATT_BLK, V7X_LANES), F32)
        for hp in range(ATT_HEADS // 2):
            cols = slice(hp * V7X_LANES, (hp + 1) * V7X_LANES)
            qp = q_ref[rows, cols]
            kk = k_ref[slot, :, cols]
            vv = v_ref[slot, :, cols]
            outs, lses = [], []
            for sub in range(2):
                qm = jnp.where(low if sub == 0 else ~low, qp, jnp.zeros_like(qp))
                s = _dot_nt(qm, kk) + bias_ref[2 * hp + sub]
                if u in plan.first_units:
                    s = s + first_pen
                m = jnp.max(s, axis=-1, keepdims=True)
                e = jnp.exp2(s - m)
                den = jnp.sum(e, axis=-1, keepdims=True)
                outs.append(_dot(e.astype(BF16), vv) / den)
                lses.append((m, den))
            put_o(rows, cols, jnp.where(low, outs[0], outs[1]))
            m_tile = jnp.where(lane == 2 * hp, lses[0][0], jnp.where(lane == 2 * hp + 1, lses[1][0], m_tile))
            den_tile = jnp.where(lane == 2 * hp, lses[0][1], jnp.where(lane == 2 * hp + 1, lses[1][1], den_tile))
        put_lse(rows, m_tile * LN2 + jnp.log(den_tile))
        k_ref[slot, :ATT_BLK, :] = k_ref[slot, ATT_BLK:, :]
        v_ref[slot, :ATT_BLK, :] = v_ref[slot, ATT_BLK:, :]


def _attn_dense_kernel(x_ref, g0_ref, w_hbm, bias_ref, o_ref, lse_ref, k_ref, v_ref, q_ref, *wscr, plan, specs):
    (w_ref,) = _resident_weights((pl.program_id(0) == 0) & (pl.program_id(2) == 0), specs, (w_hbm,), wscr)

    def put_o(rows, cols, tile):
        o_ref[rows, cols] = tile

    def put_lse(rows, tile):
        lse_ref[rows, :] = tile

    _attn_units(plan, pl.program_id(2), x_ref[...], g0_ref, w_ref, bias_ref, k_ref, v_ref, q_ref, put_o, put_lse)


def _attn_strided_kernel(x_hbm, g0_ref, w_hbm, bias_ref, o_hbm, lse_hbm, k_ref, v_ref, q_ref,
                         xbuf, obuf, lbuf, xsem, osem, lsem, *wscr, plan, specs, n_batch, n_steps):
    b, hf, t = pl.program_id(0), pl.program_id(1), pl.program_id(2)
    step = (b * plan.halves + hf) * n_steps + t
    n_total = n_batch * plan.halves * n_steps
    buf = lax.rem(step, 2)

    def class_rows(ref, bb, hh, tt, sub, r0, n):
        return ref.at[bb, pl.ds(tt * plan.block_rows + r0, n), hh * V7X_SUBLANES + sub]

    def copies(kind, bb, hh, tt, which):
        out = []
        for u, pcs in enumerate(plan.pieces):
            at = u * ATT_BLK
            for sub, r0, n in pcs:
                dense = pl.ds(at, n)
                if kind == "gather":
                    out.append(pltpu.make_async_copy(class_rows(x_hbm, bb, hh, tt, sub, r0, n),
                                                     xbuf.at[which, dense], xsem.at[which]))
                else:
                    out.append(pltpu.make_async_copy(obuf.at[which, dense],
                                                     class_rows(o_hbm, bb, hh, tt, sub, r0, n), osem.at[which]))
                    out.append(pltpu.make_async_copy(lbuf.at[which, dense],
                                                     class_rows(lse_hbm, bb, hh, tt, sub, r0, n), lsem.at[which]))
                at += n
        return out

    @pl.when(step == 0)
    def _():
        for cp in copies("gather", b, hf, t, buf):
            cp.start()

    wrap_t = t + 1 == n_steps
    wrap_h = hf + 1 == plan.halves
    nt = jnp.where(wrap_t, 0, t + 1)
    nh = jnp.where(wrap_t, jnp.where(wrap_h, 0, hf + 1), hf)
    nb = jnp.where(wrap_t & wrap_h, b + 1, b)

    @pl.when(step + 1 < n_total)
    def _():
        for cp in copies("gather", nb, nh, nt, 1 - buf):
            cp.start()

    (w_ref,) = _resident_weights(step == 0, specs, (w_hbm,), wscr)

    for cp in copies("gather", b, hf, t, buf):
        cp.wait()

    @pl.when(step >= 2)
    def _():
        for cp in copies("scatter", b, hf, t, buf):
            cp.wait()

    def put_o(rows, cols, tile):
        obuf[buf, rows, cols] = tile

    def put_lse(rows, tile):
        lbuf[buf, rows, :] = tile

    _attn_units(plan, t, xbuf[buf], g0_ref, w_ref, bias_ref, k_ref, v_ref, q_ref, put_o, put_lse)

    for cp in copies("scatter", b, hf, t, buf):
        cp.start()

    @pl.when(step == n_total - 1)
    def _():
        for cp in copies("scatter", b, hf, t, 1 - buf):
            cp.wait()
        for cp in copies("scatter", b, hf, t, buf):
            cp.wait()


def _attention_group(h, gains, w_in, bias, plan, group, layer, j):
    b, s, d = h.shape
    tokens = plan.tokens_per_step
    specs = (_weight(w_in, j, col0=group * 3 * ATT_D, cols=3 * ATT_D),)
    resident = [
        _layer_resident(gains.shape, layer * N_NORMS),
        HBM,
        pl.BlockSpec((None, ATT_HEADS, ATT_BLK, 2 * ATT_BLK), lambda bi, hf, t: (group, 0, 0, 0),
                     pipeline_mode=pl.Buffered(1)),
    ]
    kv_q = [
        pltpu.VMEM((plan.n_slots, 2 * ATT_BLK, ATT_D), BF16),
        pltpu.VMEM((plan.n_slots, 2 * ATT_BLK, ATT_D), BF16),
        pltpu.VMEM((tokens, ATT_D), BF16),
    ]
    if not plan.view_minor:
        steps = s // plan.block_rows
        block = lambda width: pl.BlockSpec((None, plan.block_rows, width), lambda bi, hf, t: (bi, t, 0))
        o, lse = pl.pallas_call(
            functools.partial(_attn_dense_kernel, plan=plan, specs=specs),
            grid=(b, 1, steps),
            in_specs=[block(d)] + resident,
            out_specs=[block(ATT_D), block(V7X_LANES)],
            out_shape=[jax.ShapeDtypeStruct((b, s, ATT_D), F32), jax.ShapeDtypeStruct((b, s, V7X_LANES), F32)],
            scratch_shapes=kv_q + _weight_scratch(specs),
            compiler_params=_params("arbitrary", "arbitrary", "arbitrary"),
            name=f"attention_group{group}",
        )(h, gains, w_in, bias)
    else:
        lead = s // plan.view_minor
        steps = lead // plan.block_rows
        assert b * plan.halves * steps >= 2
        view = lambda width: (b, lead, plan.view_minor, width)
        o, lse = pl.pallas_call(
            functools.partial(_attn_strided_kernel, plan=plan, specs=specs, n_batch=b, n_steps=steps),
            grid=(b, plan.halves, steps),
            in_specs=[HBM] + resident,
            out_specs=[HBM, HBM],
            out_shape=[jax.ShapeDtypeStruct(view(ATT_D), F32), jax.ShapeDtypeStruct(view(V7X_LANES), F32)],
            scratch_shapes=kv_q + [
                pltpu.VMEM((2, tokens, d), F32),
                pltpu.VMEM((2, tokens, ATT_D), F32),
                pltpu.VMEM((2, tokens, V7X_LANES), F32),
                pltpu.SemaphoreType.DMA((2,)), pltpu.SemaphoreType.DMA((2,)), pltpu.SemaphoreType.DMA((2,)),
            ] + _weight_scratch(specs),
            compiler_params=_params("arbitrary", "arbitrary", "arbitrary"),
            name=f"attention_group{group}",
        )(h.reshape(view(d)), gains, w_in, bias)
    return o.reshape(b * s, ATT_D), lse.reshape(b * s, V7X_LANES)


ATT_MERGE_TM = 512


def _attn_merge_kernel(h_ref, o0_ref, o1_ref, o2_ref, l0_ref, l1_ref, l2_ref, g1_ref, wout_hbm, expand_ref,
                       out_ref, *wscr, specs):
    (wout_ref,) = _resident_weights(pl.program_id(0) == 0, specs, (wout_hbm,), wscr)
    l0, l1, l2 = l0_ref[...], l1_ref[...], l2_ref[...]
    m = jnp.maximum(jnp.maximum(l0, l1), l2)
    e0, e1, e2 = jnp.exp(l0 - m), jnp.exp(l1 - m), jnp.exp(l2 - m)
    inv = 1.0 / (e0 + e1 + e2)
    expand = expand_ref[...]

    def per_channel(w):
        hi = w.astype(BF16)
        lo = (w - hi.astype(F32)).astype(BF16)
        return _dot(hi, expand) + _dot(lo, expand)

    o = (per_channel(e0 * inv) * o0_ref[...] + per_channel(e1 * inv) * o1_ref[...]
         + per_channel(e2 * inv) * o2_ref[...])
    out_ref[...] = h_ref[...] + _rms(_dot(o.astype(BF16), wout_ref[...]), g1_ref[...])


def _attention_merge(h, outs, lses, gains, w_out, layer, j):
    n, d = h.shape
    row = lambda i: (i, 0)
    tile = pl.BlockSpec((ATT_MERGE_TM, d), row)
    lse_tile = pl.BlockSpec((ATT_MERGE_TM, V7X_LANES), row)
    expand = np.zeros((V7X_LANES, ATT_D), np.float32)
    for hd in range(ATT_HEADS):
        expand[hd, hd * ATT_HEAD_DIM:(hd + 1) * ATT_HEAD_DIM] = 1.0
    expand = jnp.asarray(expand, BF16)
    specs = (_weight(w_out, j),)
    return pl.pallas_call(
        functools.partial(_attn_merge_kernel, specs=specs),
        grid=(n // ATT_MERGE_TM,),
        in_specs=[tile] * 4 + [lse_tile] * 3 + [
            _layer_resident(gains.shape, layer * N_NORMS + 1), HBM, _resident(expand.shape)],
        out_specs=tile,
        out_shape=jax.ShapeDtypeStruct((n, d), F32),
        scratch_shapes=_weight_scratch(specs),
        compiler_params=_params("arbitrary"),
        name="attention_merge",
    )(h, *outs, *lses, gains, w_out, expand)


def kernel(x, p, norm_g, ret_w_in, ret_w_out, attn_w_in, attn_w_out, rel_bias, gm_w_in, gm_ln_g, gm_ln_b,
           gm_w_s, gm_b_s, gm_w_out, ffn_w_in, ffn_w_out, ple_w_proj, ple_w_gate):
    b, s, d = x.shape
    n = b * s
    gains = norm_g.reshape(DEPTH * N_NORMS, 1, d)
    p = p.reshape(DEPTH, n, PLE_DIM)
    gm_ln_g = gm_ln_g.reshape(-1, 1, GM_WIDTH)
    gm_ln_b = gm_ln_b.reshape(-1, 1, GM_WIDTH)
    gm_b_s = gm_b_s.reshape(-1, GM_GROUPS, GM_CHUNK, 1)
    ret_consts = _retention_consts(s)
    plans = tuple(_group_plan(dil) for _, dil in DILATION_PAIRS)

    h = x
    for i in range(DEPTH):
        kind, j = i % N_MIXERS, i // N_MIXERS
        if kind == 0:
            h = _retention_mixer(h.reshape(b, s, d), gains, ret_w_in, ret_w_out, ret_consts, i, j)
        elif kind == 1:
            bias = _attention_bias(rel_bias, plans)
            outs, lses = [], []
            for gi, plan in enumerate(plans):
                o, lse = _attention_group(h.reshape(b, s, d), gains, attn_w_in, bias, plan, gi, i, j)
                outs.append(o)
                lses.append(lse)
            h = _attention_merge(h.reshape(n, d), outs, lses, gains, attn_w_out, i, j)
        else:
            h = _gmlp_mixer(h.reshape(n, d), gains, gm_w_in, gm_ln_g, gm_ln_b, gm_w_s, gm_b_s, gm_w_out, i, j)
        h = _ffn_ple(h.reshape(n, d), p, gains, ffn_w_in, ffn_w_out, ple_w_proj, ple_w_gate, i)
    return h.reshape(b, s, d)
```

```python
import functools
import math
from typing import NamedTuple

import jax
import jax.numpy as jnp
import numpy as np
from jax import lax
from jax.experimental import pallas as pl
from jax.experimental.pallas import tpu as pltpu

D_MODEL = 1024
DEPTH = 4
N_MIXERS = 3
N_NORMS = 5
PLE_DIM = 256
EPS = 1e-6

RET_HEADS = 4
RET_QK_DIM = 256
RET_V_DIM = 512
RET_CHUNK = 256
ROPE_BASE = 10000.0

ATT_HEADS = 16
ATT_HEAD_DIM = 64
DILATION_PAIRS = ((128, 1), (512, 4), (2048, 16))
ATT_BLK = 128
REL_BUCKETS = 32
REL_MAX_EXACT = 16
REL_MAX_DIST = 2048

GM_CHUNK = 128
GM_WIDTH = 2 * D_MODEL
GM_GROUPS = 8
GM_GROUP_DIM = GM_WIDTH // GM_GROUPS

FFN_HIDDEN = 2816

V7X_LANES = 128
V7X_SUBLANES = 8
V7X_VMEM_BYTES = 64 * 1024 * 1024
VMEM_LIMIT_BYTES = V7X_VMEM_BYTES - 8 * 1024 * 1024

MASK_VALUE = -1e30
LOG2E = math.log2(math.e)
LN2 = math.log(2.0)

BF16 = jnp.bfloat16
F32 = jnp.float32


def _resident(shape):
    return pl.BlockSpec(shape, lambda *_: (0,) * len(shape), pipeline_mode=pl.Buffered(1))


def _layer_resident(shape, layer):
    rest = tuple(shape[1:])
    return pl.BlockSpec((None,) + rest, lambda *_: (layer,) + (0,) * len(rest), pipeline_mode=pl.Buffered(1))


def _rms(x, g):
    return x * lax.rsqrt(jnp.mean(x * x, axis=-1, keepdims=True) + EPS) * g


def _dot(a, b):
    return jnp.dot(a, b, preferred_element_type=F32)


def _dot_nt(a, b):
    return lax.dot_general(a, b, (((1,), (1,)), ((), ())), preferred_element_type=F32)


def _dot_tn(a, b):
    return lax.dot_general(a, b, (((0,), (0,)), ((), ())), preferred_element_type=F32)


def _params(*semantics):
    return pltpu.CompilerParams(dimension_semantics=semantics, vmem_limit_bytes=VMEM_LIMIT_BYTES)


WEIGHT_CHUNK = (256, 512)
WEIGHT_SLOTS = 12

HBM = pl.BlockSpec(memory_space=pl.ANY)


class _WeightSpec(NamedTuple):
    layer: int
    rows: int
    cols: int
    col0: int


def _weight(array, layer, col0=0, cols=None):
    spec = _WeightSpec(layer, array.shape[1], array.shape[2] if cols is None else cols, col0)
    assert spec.rows % WEIGHT_CHUNK[0] == 0 and spec.cols % WEIGHT_CHUNK[1] == 0
    return spec


def _weight_scratch(specs):
    return ([pltpu.VMEM((spec.rows, spec.cols), BF16) for spec in specs]
            + [pltpu.VMEM((WEIGHT_SLOTS,) + WEIGHT_CHUNK, F32), pltpu.SemaphoreType.DMA((WEIGHT_SLOTS,))])


def _resident_weights(first, specs, hbm_refs, scratch):
    *w_refs, stage, sem = scratch
    cr, cc = WEIGHT_CHUNK
    chunks = [(w_hbm, w_ref, spec, r, c)
              for spec, w_hbm, w_ref in zip(specs, hbm_refs, w_refs)
              for r in range(0, spec.rows, cr) for c in range(0, spec.cols, cc)]

    def copy(k):
        w_hbm, _, spec, r, c = chunks[k]
        src = w_hbm.at[spec.layer, pl.ds(r, cr), pl.ds(spec.col0 + c, cc)]
        return pltpu.make_async_copy(src, stage.at[k % WEIGHT_SLOTS], sem.at[k % WEIGHT_SLOTS])

    @pl.when(first)
    def _():
        for k in range(min(WEIGHT_SLOTS, len(chunks))):
            copy(k).start()
        for k, (_, w_ref, _, r, c) in enumerate(chunks):
            copy(k).wait()
            w_ref[r:r + cr, c:c + cc] = stage[k % WEIGHT_SLOTS].astype(BF16)
            if k + WEIGHT_SLOTS < len(chunks):
                copy(k + WEIGHT_SLOTS).start()

    return w_refs


FFN_TM = 1024
FFN_SUB = 512


def _ffn_ple_kernel(h_ref, p_ref, g2_ref, g3_ref, g4_ref, win_hbm, wout_hbm, wp_hbm, wg_hbm, o_ref, *wscr, specs):
    win_ref, wout_ref, wp_ref, wg_ref = _resident_weights(
        pl.program_id(0) == 0, specs, (win_hbm, wout_hbm, wp_hbm, wg_hbm), wscr)
    for part in range(FFN_TM // FFN_SUB):
        rows = slice(part * FFN_SUB, (part + 1) * FFN_SUB)
        x = h_ref[rows, :]
        a = _rms(x, g2_ref[...]).astype(BF16)
        gate = _dot(a, win_ref[:, :FFN_HIDDEN])
        up = _dot(a, win_ref[:, FFN_HIDDEN:])
        hid = (gate * jax.nn.sigmoid(gate) * up).astype(BF16)
        h1 = x + _rms(_dot(hid, wout_ref[...]), g3_ref[...])
        emb = _dot(p_ref[rows, :].astype(BF16), wp_ref[...])
        gate = jax.nn.sigmoid(_dot(_rms(h1, g4_ref[...]).astype(BF16), wg_ref[...]))
        o_ref[rows, :] = h1 + gate * emb


def _ffn_ple(h, p, gains, w_in, w_out, w_proj, w_gate, layer):
    n, d = h.shape
    row = lambda i: (i, 0)
    gain = lambda k: _layer_resident(gains.shape, layer * N_NORMS + k)
    specs = tuple(_weight(w, layer) for w in (w_in, w_out, w_proj, w_gate))
    return pl.pallas_call(
        functools.partial(_ffn_ple_kernel, specs=specs),
        grid=(n // FFN_TM,),
        in_specs=[
            pl.BlockSpec((FFN_TM, d), row),
            pl.BlockSpec((None, FFN_TM, PLE_DIM), lambda i: (layer, i, 0)),
            gain(2), gain(3), gain(4),
            HBM, HBM, HBM, HBM,
        ],
        out_specs=pl.BlockSpec((FFN_TM, d), row),
        out_shape=jax.ShapeDtypeStruct((n, d), F32),
        scratch_shapes=_weight_scratch(specs),
        compiler_params=_params("arbitrary"),
        name="ffn_ple",
    )(h, p, gains, gains, gains, w_in, w_out, w_proj, w_gate)


RET_TM = 1024
RET_SUB = 512
RET_Q_OFF = 0
RET_K_OFF = RET_HEADS * RET_QK_DIM
RET_V_OFF = 2 * RET_HEADS * RET_QK_DIM
RET_G_OFF = RET_V_OFF + RET_HEADS * RET_V_DIM


def _rope(t, cos, sin):
    half = RET_QK_DIM // 2
    t1, t2 = t[:, :half], t[:, half:]
    return jnp.concatenate([t1 * cos - t2 * sin, t2 * cos + t1 * sin], axis=-1)


def _retention_kernel(h_ref, cos_ref, sin_ref, g0_ref, g1_ref, win_hbm, wout_hbm,
                      dmask_ref, qdec_ref, kdec_ref, cdec_ref, o_ref, state_ref, y_ref, *wscr, specs):
    win_ref, wout_ref = _resident_weights(
        (pl.program_id(0) == 0) & (pl.program_id(1) == 0), specs, (win_hbm, wout_hbm), wscr)

    @pl.when(pl.program_id(1) == 0)
    def _():
        state_ref[...] = jnp.zeros(state_ref.shape, F32)

    for part in range(RET_TM // RET_SUB):
        prow = slice(part * RET_SUB, (part + 1) * RET_SUB)
        x = h_ref[prow, :]
        a = _rms(x, g0_ref[...]).astype(BF16)
        cos, sin = cos_ref[prow, :], sin_ref[prow, :]
        for hd in range(RET_HEADS):
            qo, ko, vo = RET_Q_OFF + hd * RET_QK_DIM, RET_K_OFF + hd * RET_QK_DIM, RET_V_OFF + hd * RET_V_DIM
            q = _rope(_dot(a, win_ref[:, qo:qo + RET_QK_DIM]), cos, sin).astype(BF16)
            k = _rope(_dot(a, win_ref[:, ko:ko + RET_QK_DIM]), cos, sin) * (RET_QK_DIM ** -0.5)
            v = _dot(a, win_ref[:, vo:vo + RET_V_DIM]).astype(BF16)
            for c in range(RET_SUB // RET_CHUNK):
                rows = slice(c * RET_CHUNK, (c + 1) * RET_CHUNK)
                qc, kc, vc = q[rows], k[rows], v[rows]
                st = state_ref[hd]
                scores = _dot_nt(qc, kc.astype(BF16)) * dmask_ref[hd]
                inner = _dot(scores.astype(BF16), vc)
                cross = _dot(qc, st.astype(BF16)) * qdec_ref[hd]
                state_ref[hd] = st * cdec_ref[hd] + _dot_tn((kc * kdec_ref[hd]).astype(BF16), vc)
                y = inner + cross
                y_ref[rows, hd * RET_V_DIM:(hd + 1) * RET_V_DIM] = y * lax.rsqrt(
                    jnp.mean(y * y, axis=-1, keepdims=True) + EPS)
        gate = _dot(a, win_ref[:, RET_G_OFF:RET_G_OFF + RET_HEADS * RET_V_DIM])
        z = (gate * jax.nn.sigmoid(gate) * y_ref[...]).astype(BF16)
        o_ref[prow, :] = x + _rms(_dot(z, wout_ref[...]), g1_ref[...])


def _retention_consts(seq_len):
    hh, c, dk = RET_HEADS, RET_CHUNK, RET_QK_DIM
    inv_freq = np.float32(ROPE_BASE) ** (-np.arange(0, dk, 2, dtype=np.float32) / np.float32(dk))
    ang = np.arange(seq_len, dtype=np.float32)[:, None] * inv_freq[None, :]
    cos = np.cos(ang.astype(np.float64)).astype(np.float32)
    sin = np.sin(ang.astype(np.float64)).astype(np.float32)
    log_gamma = jnp.log1p(-jnp.exp2(-5.0 - jnp.arange(hh, dtype=F32)))
    idx = jnp.arange(c, dtype=F32)
    diff = idx[:, None] - idx[None, :]
    dmask = jnp.where(diff >= 0, jnp.exp(log_gamma[:, None, None] * jnp.maximum(diff, 0.0)), 0.0)
    qdec = jnp.exp(log_gamma[:, None] * (idx[None, :] + 1.0))[:, :, None]
    kdec = jnp.exp(log_gamma[:, None] * (c - 1.0 - idx[None, :]))[:, :, None]
    cdec = jnp.exp(log_gamma * c)[:, None, None]
    return jnp.asarray(cos), jnp.asarray(sin), dmask, qdec, kdec, cdec


def _retention_mixer(h, gains, w_in, w_out, consts, layer, j):
    b, s, d = h.shape
    cos, sin, dmask, qdec, kdec, cdec = consts
    tile = lambda bi, t: (bi, t, 0)
    pos = lambda bi, t: (t, 0)
    gain = lambda k: _layer_resident(gains.shape, layer * N_NORMS + k)
    specs = (_weight(w_in, j), _weight(w_out, j))
    return pl.pallas_call(
        functools.partial(_retention_kernel, specs=specs),
        grid=(b, s // RET_TM),
        in_specs=[
            pl.BlockSpec((None, RET_TM, d), tile),
            pl.BlockSpec((RET_TM, RET_QK_DIM // 2), pos),
            pl.BlockSpec((RET_TM, RET_QK_DIM // 2), pos),
            gain(0), gain(1),
            HBM, HBM,
            _resident(dmask.shape), _resident(qdec.shape), _resident(kdec.shape), _resident(cdec.shape),
        ],
        out_specs=pl.BlockSpec((None, RET_TM, d), tile),
        out_shape=jax.ShapeDtypeStruct((b, s, d), F32),
        scratch_shapes=[
            pltpu.VMEM((RET_HEADS, RET_QK_DIM, RET_V_DIM), F32),
            pltpu.VMEM((RET_SUB, RET_HEADS * RET_V_DIM), F32),
        ] + _weight_scratch(specs),
        compiler_params=_params("arbitrary", "arbitrary"),
        name="retention_mixer",
    )(h, cos, sin, gains, gains, w_in, w_out, dmask, qdec, kdec, cdec)


GM_TM = 512


def _gmlp_kernel(h_ref, g0_ref, g1_ref, win_hbm, lng_ref, lnb_ref, ws_ref, bs_ref, wout_hbm, o_ref, s_ref, *wscr,
                 specs):
    win_ref, wout_ref = _resident_weights(pl.program_id(0) == 0, specs, (win_hbm, wout_hbm), wscr)
    x = h_ref[...]
    a = _rms(x, g0_ref[...]).astype(BF16)

    def gelu(t):
        return 0.5 * t * (1.0 + lax.erf(t * (2.0 ** -0.5)))

    v = gelu(_dot(a, win_ref[:, GM_WIDTH:]))
    u = gelu(_dot(a, win_ref[:, :GM_WIDTH]))
    mu = jnp.mean(v, axis=-1, keepdims=True)
    vc = v - mu
    var = jnp.mean(vc * vc, axis=-1, keepdims=True)
    vn = (vc * lax.rsqrt(var + EPS) * lng_ref[...] + lnb_ref[...]).astype(BF16)
    row = lax.broadcasted_iota(jnp.int32, (GM_CHUNK, GM_CHUNK), 0)
    col = lax.broadcasted_iota(jnp.int32, (GM_CHUNK, GM_CHUNK), 1)
    causal = row >= col
    for g in range(GM_GROUPS):
        w = jnp.where(causal, ws_ref[g], 0.0).astype(BF16)
        cols = slice(g * GM_GROUP_DIM, (g + 1) * GM_GROUP_DIM)
        for c in range(GM_TM // GM_CHUNK):
            rows = slice(c * GM_CHUNK, (c + 1) * GM_CHUNK)
            s_ref[rows, cols] = _dot(w, vn[rows, cols]) + bs_ref[g]
    z = (u * s_ref[...]).astype(BF16)
    o_ref[...] = x + _rms(_dot(z, wout_ref[...]), g1_ref[...])


def _gmlp_mixer(h, gains, w_in, ln_g, ln_b, w_s, b_s, w_out, layer, j):
    n, d = h.shape
    row = lambda i: (i, 0)
    gain = lambda k: _layer_resident(gains.shape, layer * N_NORMS + k)
    specs = (_weight(w_in, j), _weight(w_out, j))
    return pl.pallas_call(
        functools.partial(_gmlp_kernel, specs=specs),
        grid=(n // GM_TM,),
        in_specs=[
            pl.BlockSpec((GM_TM, d), row),
            gain(0), gain(1),
            HBM, _layer_resident(ln_g.shape, j), _layer_resident(ln_b.shape, j),
            _layer_resident(w_s.shape, j), _layer_resident(b_s.shape, j), HBM,
        ],
        out_specs=pl.BlockSpec((GM_TM, d), row),
        out_shape=jax.ShapeDtypeStruct((n, d), F32),
        scratch_shapes=[pltpu.VMEM((GM_TM, GM_WIDTH), F32)] + _weight_scratch(specs),
        compiler_params=_params("arbitrary"),
        name="gmlp_mixer",
    )(h, gains, gains, w_in, ln_g, ln_b, w_s, b_s, w_out)


ATT_D = ATT_HEADS * ATT_HEAD_DIM


class _GroupPlan(NamedTuple):
    dilation: int
    view_minor: int
    block_rows: int
    halves: int
    n_units: int
    n_slots: int
    pieces: tuple
    first_units: tuple
    positions: np.ndarray

    @property
    def tokens_per_step(self):
        return self.n_units * ATT_BLK


def _group_plan(dilation):
    if dilation == 1:
        return _GroupPlan(1, 0, 8 * ATT_BLK, 1, 8, 1,
                          tuple(((None, u * ATT_BLK, ATT_BLK),) for u in range(8)), (0,),
                          np.arange(ATT_BLK))
    if dilation == 4:
        half = ATT_BLK // 2
        return _GroupPlan(4, 8, ATT_BLK, 1, 8, 4,
                          tuple(((c, blk * half, half), (c + 4, blk * half, half))
                                for blk in range(2) for c in range(4)),
                          (0, 1, 2, 3),
                          np.concatenate([2 * np.arange(half), 2 * np.arange(half) + 1]))
    assert dilation == 16
    return _GroupPlan(16, 16, ATT_BLK, 2, 8, 8,
                      tuple(((c, 0, ATT_BLK),) for c in range(8)), tuple(range(8)),
                      np.arange(ATT_BLK))


def _t5_bucket(dist):
    d = jnp.maximum(dist, 1).astype(F32)
    large = REL_MAX_EXACT + (jnp.log(d / REL_MAX_EXACT) / math.log(REL_MAX_DIST / REL_MAX_EXACT)
                             * (REL_BUCKETS - REL_MAX_EXACT)).astype(jnp.int32)
    large = jnp.minimum(large, REL_BUCKETS - 1)
    return jnp.where(dist < REL_MAX_EXACT, dist, large)


def _bias_kernel(table_ref, bucket_ref, dist_ref, o_ref):
    g = pl.program_id(0)
    bucket = bucket_ref[...]
    dist = dist_ref[...]
    band = (dist >= 0) & (dist <= ATT_BLK)
    for hd in range(ATT_HEADS):
        acc = jnp.zeros(bucket.shape, F32)
        for b in range(REL_BUCKETS):
            acc = jnp.where(bucket == b, table_ref[b, g * ATT_HEADS + hd], acc)
        o_ref[hd] = jnp.where(band, acc * LOG2E, MASK_VALUE)


def _attention_bias(rel_bias, plans):
    dists = []
    for plan in plans:
        pos_k = np.concatenate([plan.positions - ATT_BLK, plan.positions])
        dists.append(plan.positions[:, None] - pos_k[None, :])
    dist = jnp.asarray(np.stack(dists), jnp.int32)
    scale = jnp.asarray([plan.dilation for plan in plans], jnp.int32)[:, None, None]
    buckets = _t5_bucket(jnp.maximum(dist, 0) * scale).astype(jnp.int32)
    tile = pl.BlockSpec((None, ATT_BLK, 2 * ATT_BLK), lambda g: (g, 0, 0))
    return pl.pallas_call(
        _bias_kernel,
        grid=(len(plans),),
        in_specs=[pl.BlockSpec(memory_space=pltpu.SMEM), tile, tile],
        out_specs=pl.BlockSpec((None, ATT_HEADS, ATT_BLK, 2 * ATT_BLK), lambda g: (g, 0, 0, 0)),
        out_shape=jax.ShapeDtypeStruct((len(plans), ATT_HEADS, ATT_BLK, 2 * ATT_BLK), F32),
        compiler_params=_params("arbitrary"),
        name="attention_bias",
    )(rel_bias, buckets, dist)


def _attn_units(plan, t, xa, g0_ref, w_ref, bias_ref, k_ref, v_ref, q_ref, put_o, put_lse):
    @pl.when(t == 0)
    def _():
        k_ref[:, :ATT_BLK, :] = jnp.zeros((plan.n_slots, ATT_BLK, ATT_D), BF16)
        v_ref[:, :ATT_BLK, :] = jnp.zeros((plan.n_slots, ATT_BLK, ATT_D), BF16)

    a = _rms(xa, g0_ref[...]).astype(BF16)
    q_ref[...] = (_dot(a, w_ref[:, :ATT_D]) * (ATT_HEAD_DIM ** -0.5 * LOG2E)).astype(BF16)
    kc = _dot(a, w_ref[:, ATT_D:2 * ATT_D]).astype(BF16)
    vc = _dot(a, w_ref[:, 2 * ATT_D:]).astype(BF16)

    lane = lax.broadcasted_iota(jnp.int32, (ATT_BLK, V7X_LANES), 1)
    low = lane < ATT_HEAD_DIM
    kcol = lax.broadcasted_iota(jnp.int32, (ATT_BLK, 2 * ATT_BLK), 1)
    first_pen = jnp.where((kcol < ATT_BLK) & (t == 0), MASK_VALUE, 0.0)
    for u in range(plan.n_units):
        slot = u % plan.n_slots
        rows = slice(u * ATT_BLK, (u + 1) * ATT_BLK)
        k_ref[slot, ATT_BLK:, :] = kc[rows]
        v_ref[slot, ATT_BLK:, :] = vc[rows]
        m_tile = jnp.zeros((ATT_BLK, V7X_LANES), F32)
        den_tile = jnp.ones((ATT_BLK, V7X_LANES), F32)
        for hp in range(ATT_HEADS // 2):
            cols = slice(hp * V7X_LANES, (hp + 1) * V7X_LANES)
            qp = q_ref[rows, cols]
            kk = k_ref[slot, :, cols]
            vv = v_ref[slot, :, cols]
            outs, lses = [], []
            for sub in range(2):
                qm = jnp.where(low if sub == 0 else ~low, qp, jnp.zeros_like(qp))
                s = _dot_nt(qm, kk) + bias_ref[2 * hp + sub]
                if u in plan.first_units:
                    s = s + first_pen
                m = jnp.max(s, axis=-1, keepdims=True)
                e = jnp.exp2(s - m)
                den = jnp.sum(e, axis=-1, keepdims=True)
                outs.append(_dot(e.astype(BF16), vv) / den)
                lses.append((m, den))
            put_o(rows, cols, jnp.where(low, outs[0], outs[1]))
            m_tile = jnp.where(lane == 2 * hp, lses[0][0], jnp.where(lane == 2 * hp + 1, lses[1][0], m_tile))
            den_tile = jnp.where(lane == 2 * hp, lses[0][1], jnp.where(lane == 2 * hp + 1, lses[1][1], den_tile))
        put_lse(rows, m_tile * LN2 + jnp.log(den_tile))
        k_ref[slot, :ATT_BLK, :] = k_ref[slot, ATT_BLK:, :]
        v_ref[slot, :ATT_BLK, :] = v_ref[slot, ATT_BLK:, :]


def _attn_dense_kernel(x_ref, g0_ref, w_hbm, bias_ref, o_ref, lse_ref, k_ref, v_ref, q_ref, *wscr, plan, specs):
    (w_ref,) = _resident_weights((pl.program_id(0) == 0) & (pl.program_id(2) == 0), specs, (w_hbm,), wscr)

    def put_o(rows, cols, tile):
        o_ref[rows, cols] = tile

    def put_lse(rows, tile):
        lse_ref[rows, :] = tile

    _attn_units(plan, pl.program_id(2), x_ref[...], g0_ref, w_ref, bias_ref, k_ref, v_ref, q_ref, put_o, put_lse)


def _attn_strided_kernel(x_hbm, g0_ref, w_hbm, bias_ref, o_hbm, lse_hbm, k_ref, v_ref, q_ref,
                         xbuf, obuf, lbuf, xsem, osem, lsem, *wscr, plan, specs, n_batch, n_steps):
    b, hf, t = pl.program_id(0), pl.program_id(1), pl.program_id(2)
    step = (b * plan.halves + hf) * n_steps + t
    n_total = n_batch * plan.halves * n_steps
    buf = lax.rem(step, 2)

    def class_rows(ref, bb, hh, tt, sub, r0, n):
        return ref.at[bb, pl.ds(tt * plan.block_rows + r0, n), hh * V7X_SUBLANES + sub]

    def copies(kind, bb, hh, tt, which):
        out = []
        for u, pcs in enumerate(plan.pieces):
            at = u * ATT_BLK
            for sub, r0, n in pcs:
                dense = pl.ds(at, n)
                if kind == "gather":
                    out.append(pltpu.make_async_copy(class_rows(x_hbm, bb, hh, tt, sub, r0, n),
                                                     xbuf.at[which, dense], xsem.at[which]))
                else:
                    out.append(pltpu.make_async_copy(obuf.at[which, dense],
                                                     class_rows(o_hbm, bb, hh, tt, sub, r0, n), osem.at[which]))
                    out.append(pltpu.make_async_copy(lbuf.at[which, dense],
                                                     class_rows(lse_hbm, bb, hh, tt, sub, r0, n), lsem.at[which]))
                at += n
        return out

    @pl.when(step == 0)
    def _():
        for cp in copies("gather", b, hf, t, buf):
            cp.start()

    wrap_t = t + 1 == n_steps
    wrap_h = hf + 1 == plan.halves
    nt = jnp.where(wrap_t, 0, t + 1)
    nh = jnp.where(wrap_t, jnp.where(wrap_h, 0, hf + 1), hf)
    nb = jnp.where(wrap_t & wrap_h, b + 1, b)

    @pl.when(step + 1 < n_total)
    def _():
        for cp in copies("gather", nb, nh, nt, 1 - buf):
            cp.start()

    (w_ref,) = _resident_weights(step == 0, specs, (w_hbm,), wscr)

    for cp in copies("gather", b, hf, t, buf):
        cp.wait()

    @pl.when(step >= 2)
    def _():
        for cp in copies("scatter", b, hf, t, buf):
            cp.wait()

    def put_o(rows, cols, tile):
        obuf[buf, rows, cols] = tile

    def put_lse(rows, tile):
        lbuf[buf, rows, :] = tile

    _attn_units(plan, t, xbuf[buf], g0_ref, w_ref, bias_ref, k_ref, v_ref, q_ref, put_o, put_lse)

    for cp in copies("scatter", b, hf, t, buf):
        cp.start()

    @pl.when(step == n_total - 1)
    def _():
        for cp in copies("scatter", b, hf, t, 1 - buf):
            cp.wait()
        for cp in copies("scatter", b, hf, t, buf):
            cp.wait()


def _attention_group(h, gains, w_in, bias, plan, group, layer, j):
    b, s, d = h.shape
    tokens = plan.tokens_per_step
    specs = (_weight(w_in, j, col0=group * 3 * ATT_D, cols=3 * ATT_D),)
    resident = [
        _layer_resident(gains.shape, layer * N_NORMS),
        HBM,
        pl.BlockSpec((None, ATT_HEADS, ATT_BLK, 2 * ATT_BLK), lambda bi, hf, t: (group, 0, 0, 0),
                     pipeline_mode=pl.Buffered(1)),
    ]
    kv_q = [
        pltpu.VMEM((plan.n_slots, 2 * ATT_BLK, ATT_D), BF16),
        pltpu.VMEM((plan.n_slots, 2 * ATT_BLK, ATT_D), BF16),
        pltpu.VMEM((tokens, ATT_D), BF16),
    ]
    if not plan.view_minor:
        steps = s // plan.block_rows
        block = lambda width: pl.BlockSpec((None, plan.block_rows, width), lambda bi, hf, t: (bi, t, 0))
        o, lse = pl.pallas_call(
            functools.partial(_attn_dense_kernel, plan=plan, specs=specs),
            grid=(b, 1, steps),
            in_specs=[block(d)] + resident,
            out_specs=[block(ATT_D), block(V7X_LANES)],
            out_shape=[jax.ShapeDtypeStruct((b, s, ATT_D), F32), jax.ShapeDtypeStruct((b, s, V7X_LANES), F32)],
            scratch_shapes=kv_q + _weight_scratch(specs),
            compiler_params=_params("arbitrary", "arbitrary", "arbitrary"),
            name=f"attention_group{group}",
        )(h, gains, w_in, bias)
    else:
        lead = s // plan.view_minor
        steps = lead // plan.block_rows
        assert b * plan.halves * steps >= 2
        view = lambda width: (b, lead, plan.view_minor, width)
        o, lse = pl.pallas_call(
            functools.partial(_attn_strided_kernel, plan=plan, specs=specs, n_batch=b, n_steps=steps),
            grid=(b, plan.halves, steps),
            in_specs=[HBM] + resident,
            out_specs=[HBM, HBM],
            out_shape=[jax.ShapeDtypeStruct(view(ATT_D), F32), jax.ShapeDtypeStruct(view(V7X_LANES), F32)],
            scratch_shapes=kv_q + [
                pltpu.VMEM((2, tokens, d), F32),
                pltpu.VMEM((2, tokens, ATT_D), F32),
                pltpu.VMEM((2, tokens, V7X_LANES), F32),
                pltpu.SemaphoreType.DMA((2,)), pltpu.SemaphoreType.DMA((2,)), pltpu.SemaphoreType.DMA((2,)),
            ] + _weight_scratch(specs),
            compiler_params=_params("arbitrary", "arbitrary", "arbitrary"),
            name=f"attention_group{group}",
        )(h.reshape(view(d)), gains, w_in, bias)
    return o.reshape(b * s, ATT_D), lse.reshape(b * s, V7X_LANES)


ATT_MERGE_TM = 512


def _attn_merge_kernel(h_ref, o0_ref, o1_ref, o2_ref, l0_ref, l1_ref, l2_ref, g1_ref, wout_hbm, expand_ref,
                       out_ref, *wscr, specs):
    (wout_ref,) = _resident_weights(pl.program_id(0) == 0, specs, (wout_hbm,), wscr)
    l0, l1, l2 = l0_ref[...], l1_ref[...], l2_ref[...]
    m = jnp.maximum(jnp.maximum(l0, l1), l2)
    e0, e1, e2 = jnp.exp(l0 - m), jnp.exp(l1 - m), jnp.exp(l2 - m)
    inv = 1.0 / (e0 + e1 + e2)
    expand = expand_ref[...]

    def per_channel(w):
        hi = w.astype(BF16)
        lo = (w - hi.astype(F32)).astype(BF16)
        return _dot(jnp.concatenate([hi, lo], axis=1), expand)

    o = (per_channel(e0 * inv) * o0_ref[...] + per_channel(e1 * inv) * o1_ref[...]
         + per_channel(e2 * inv) * o2_ref[...])
    out_ref[...] = h_ref[...] + _rms(_dot(o.astype(BF16), wout_ref[...]), g1_ref[...])


def _attention_merge(h, outs, lses, gains, w_out, layer, j):
    n, d = h.shape
    row = lambda i: (i, 0)
    tile = pl.BlockSpec((ATT_MERGE_TM, d), row)
    lse_tile = pl.BlockSpec((ATT_MERGE_TM, V7X_LANES), row)
    expand = np.zeros((V7X_LANES, ATT_D), np.float32)
    for hd in range(ATT_HEADS):
        expand[hd, hd * ATT_HEAD_DIM:(hd + 1) * ATT_HEAD_DIM] = 1.0
    expand = jnp.asarray(np.concatenate([expand, expand], axis=0), BF16)
    specs = (_weight(w_out, j),)
    return pl.pallas_call(
        functools.partial(_attn_merge_kernel, specs=specs),
        grid=(n // ATT_MERGE_TM,),
        in_specs=[tile] * 4 + [lse_tile] * 3 + [
            _layer_resident(gains.shape, layer * N_NORMS + 1), HBM, _resident(expand.shape)],
        out_specs=tile,
        out_shape=jax.ShapeDtypeStruct((n, d), F32),
        scratch_shapes=_weight_scratch(specs),
        compiler_params=_params("arbitrary"),
        name="attention_merge",
    )(h, *outs, *lses, gains, w_out, expand)


def kernel(x, p, norm_g, ret_w_in, ret_w_out, attn_w_in, attn_w_out, rel_bias, gm_w_in, gm_ln_g, gm_ln_b,
           gm_w_s, gm_b_s, gm_w_out, ffn_w_in, ffn_w_out, ple_w_proj, ple_w_gate):
    b, s, d = x.shape
    n = b * s
    gains = norm_g.reshape(DEPTH * N_NORMS, 1, d)
    p = p.reshape(DEPTH, n, PLE_DIM)
    gm_ln_g = gm_ln_g.reshape(-1, 1, GM_WIDTH)
    gm_ln_b = gm_ln_b.reshape(-1, 1, GM_WIDTH)
    gm_b_s = gm_b_s.reshape(-1, GM_GROUPS, GM_CHUNK, 1)
    ret_consts = _retention_consts(s)
    plans = tuple(_group_plan(dil) for _, dil in DILATION_PAIRS)

    h = x
    for i in range(DEPTH):
        kind, j = i % N_MIXERS, i // N_MIXERS
        if kind == 0:
            h = _retention_mixer(h.reshape(b, s, d), gains, ret_w_in, ret_w_out, ret_consts, i, j)
        elif kind == 1:
            bias = _attention_bias(rel_bias, plans)
            outs, lses = [], []
            for gi, plan in enumerate(plans):
                o, lse = _attention_group(h.reshape(b, s, d), gains, attn_w_in, bias, plan, gi, i, j)
                outs.append(o)
                lses.append(lse)
            h = _attention_merge(h.reshape(n, d), outs, lses, gains, attn_w_out, i, j)
        else:
            h = _gmlp_mixer(h.reshape(n, d), gains, gm_w_in, gm_ln_g, gm_ln_b, gm_w_s, gm_b_s, gm_w_out, i, j)
        h = _ffn_ple(h.reshape(n, d), p, gains, ffn_w_in, ffn_w_out, ple_w_proj, ple_w_gate, i)
    return h.reshape(b, s, d)
```

```python
import functools
import math
from typing import NamedTuple

import jax
import jax.numpy as jnp
import numpy as np
from jax import lax
from jax.experimental import pallas as pl
from jax.experimental.pallas import tpu as pltpu

D_MODEL = 1024
DEPTH = 4
N_MIXERS = 3
N_NORMS = 5
PLE_DIM = 256
EPS = 1e-6

RET_HEADS = 4
RET_QK_DIM = 256
RET_V_DIM = 512
RET_CHUNK = 256
ROPE_BASE = 10000.0

ATT_HEADS = 16
ATT_HEAD_DIM = 64
DILATION_PAIRS = ((128, 1), (512, 4), (2048, 16))
ATT_BLK = 128
REL_BUCKETS = 32
REL_MAX_EXACT = 16
REL_MAX_DIST = 2048

GM_CHUNK = 128
GM_WIDTH = 2 * D_MODEL
GM_GROUPS = 8
GM_GROUP_DIM = GM_WIDTH // GM_GROUPS

FFN_HIDDEN = 2816

V7X_LANES = 128
V7X_SUBLANES = 8
V7X_VMEM_BYTES = 64 * 1024 * 1024
VMEM_LIMIT_BYTES = V7X_VMEM_BYTES - 8 * 1024 * 1024

MASK_VALUE = -1e30
LOG2E = math.log2(math.e)
LN2 = math.log(2.0)

BF16 = jnp.bfloat16
F32 = jnp.float32


def _resident(shape):
    return pl.BlockSpec(shape, lambda *_: (0,) * len(shape), pipeline_mode=pl.Buffered(1))


def _layer_resident(shape, layer):
    rest = tuple(shape[1:])
    return pl.BlockSpec((None,) + rest, lambda *_: (layer,) + (0,) * len(rest), pipeline_mode=pl.Buffered(1))


def _rms(x, g):
    return x * lax.rsqrt(jnp.mean(x * x, axis=-1, keepdims=True) + EPS) * g


def _dot(a, b):
    return jnp.dot(a, b, preferred_element_type=F32)


def _dot_nt(a, b):
    return lax.dot_general(a, b, (((1,), (1,)), ((), ())), preferred_element_type=F32)


def _dot_tn(a, b):
    return lax.dot_general(a, b, (((0,), (0,)), ((), ())), preferred_element_type=F32)


def _params(*semantics):
    return pltpu.CompilerParams(dimension_semantics=semantics, vmem_limit_bytes=VMEM_LIMIT_BYTES)


WEIGHT_CHUNK = (256, 512)
WEIGHT_SLOTS = 12

HBM = pl.BlockSpec(memory_space=pl.ANY)


class _WeightSpec(NamedTuple):
    layer: int
    rows: int
    cols: int
    col0: int


def _weight(array, layer, col0=0, cols=None):
    spec = _WeightSpec(layer, array.shape[1], array.shape[2] if cols is None else cols, col0)
    assert spec.rows % WEIGHT_CHUNK[0] == 0 and spec.cols % WEIGHT_CHUNK[1] == 0
    return spec


def _weight_scratch(specs):
    return ([pltpu.VMEM((spec.rows, spec.cols), BF16) for spec in specs]
            + [pltpu.VMEM((WEIGHT_SLOTS,) + WEIGHT_CHUNK, F32), pltpu.SemaphoreType.DMA((WEIGHT_SLOTS,))])


def _resident_weights(first, specs, hbm_refs, scratch):
    *w_refs, stage, sem = scratch
    cr, cc = WEIGHT_CHUNK
    chunks = [(w_hbm, w_ref, spec, r, c)
              for spec, w_hbm, w_ref in zip(specs, hbm_refs, w_refs)
              for r in range(0, spec.rows, cr) for c in range(0, spec.cols, cc)]

    def copy(k):
        w_hbm, _, spec, r, c = chunks[k]
        src = w_hbm.at[spec.layer, pl.ds(r, cr), pl.ds(spec.col0 + c, cc)]
        return pltpu.make_async_copy(src, stage.at[k % WEIGHT_SLOTS], sem.at[k % WEIGHT_SLOTS])

    @pl.when(first)
    def _():
        for k in range(min(WEIGHT_SLOTS, len(chunks))):
            copy(k).start()
        for k, (_, w_ref, _, r, c) in enumerate(chunks):
            copy(k).wait()
            w_ref[r:r + cr, c:c + cc] = stage[k % WEIGHT_SLOTS].astype(BF16)
            if k + WEIGHT_SLOTS < len(chunks):
                copy(k + WEIGHT_SLOTS).start()

    return w_refs


FFN_TM = 1024
FFN_SUB = 512


def _ffn_ple_kernel(h_ref, p_ref, g2_ref, g3_ref, g4_ref, win_hbm, wout_hbm, wp_hbm, wg_hbm, o_ref, *wscr, specs):
    win_ref, wout_ref, wp_ref, wg_ref = _resident_weights(
        pl.program_id(0) == 0, specs, (win_hbm, wout_hbm, wp_hbm, wg_hbm), wscr)
    for part in range(FFN_TM // FFN_SUB):
        rows = slice(part * FFN_SUB, (part + 1) * FFN_SUB)
        x = h_ref[rows, :]
        a = _rms(x, g2_ref[...]).astype(BF16)
        gate = _dot(a, win_ref[:, :FFN_HIDDEN])
        up = _dot(a, win_ref[:, FFN_HIDDEN:])
        hid = (gate * jax.nn.sigmoid(gate) * up).astype(BF16)
        h1 = x + _rms(_dot(hid, wout_ref[...]), g3_ref[...])
        emb = _dot(p_ref[rows, :].astype(BF16), wp_ref[...])
        gate = jax.nn.sigmoid(_dot(_rms(h1, g4_ref[...]).astype(BF16), wg_ref[...]))
        o_ref[rows, :] = h1 + gate * emb


def _ffn_ple(h, p, gains, w_in, w_out, w_proj, w_gate, layer):
    n, d = h.shape
    row = lambda i: (i, 0)
    gain = lambda k: _layer_resident(gains.shape, layer * N_NORMS + k)
    specs = tuple(_weight(w, layer) for w in (w_in, w_out, w_proj, w_gate))
    return pl.pallas_call(
        functools.partial(_ffn_ple_kernel, specs=specs),
        grid=(n // FFN_TM,),
        in_specs=[
            pl.BlockSpec((FFN_TM, d), row),
            pl.BlockSpec((None, FFN_TM, PLE_DIM), lambda i: (layer, i, 0)),
            gain(2), gain(3), gain(4),
            HBM, HBM, HBM, HBM,
        ],
        out_specs=pl.BlockSpec((FFN_TM, d), row),
        out_shape=jax.ShapeDtypeStruct((n, d), F32),
        scratch_shapes=_weight_scratch(specs),
        compiler_params=_params("arbitrary"),
        name="ffn_ple",
    )(h, p, gains, gains, gains, w_in, w_out, w_proj, w_gate)


RET_TM = 512
RET_SUB = 512
RET_Q_OFF = 0
RET_K_OFF = RET_HEADS * RET_QK_DIM
RET_V_OFF = 2 * RET_HEADS * RET_QK_DIM
RET_G_OFF = RET_V_OFF + RET_HEADS * RET_V_DIM


def _rope(t, cos, sin):
    half = RET_QK_DIM // 2
    t1, t2 = t[:, :half], t[:, half:]
    return jnp.concatenate([t1 * cos - t2 * sin, t2 * cos + t1 * sin], axis=-1)


def _retention_kernel(h_ref, cos_ref, sin_ref, g0_ref, g1_ref, win_hbm, wout_hbm,
                      dmask_ref, qdec_ref, kdec_ref, cdec_ref, o_ref, state_ref, y_ref, *wscr, specs):
    win_ref, wout_ref = _resident_weights(
        (pl.program_id(0) == 0) & (pl.program_id(1) == 0), specs, (win_hbm, wout_hbm), wscr)

    @pl.when(pl.program_id(1) == 0)
    def _():
        state_ref[...] = jnp.zeros(state_ref.shape, F32)

    for part in range(RET_TM // RET_SUB):
        prow = slice(part * RET_SUB, (part + 1) * RET_SUB)
        x = h_ref[prow, :]
        a = _rms(x, g0_ref[...]).astype(BF16)
        cos, sin = cos_ref[prow, :], sin_ref[prow, :]
        for hd in range(RET_HEADS):
            qo, ko, vo = RET_Q_OFF + hd * RET_QK_DIM, RET_K_OFF + hd * RET_QK_DIM, RET_V_OFF + hd * RET_V_DIM
            q = _rope(_dot(a, win_ref[:, qo:qo + RET_QK_DIM]), cos, sin).astype(BF16)
            k = _rope(_dot(a, win_ref[:, ko:ko + RET_QK_DIM]), cos, sin) * (RET_QK_DIM ** -0.5)
            v = _dot(a, win_ref[:, vo:vo + RET_V_DIM]).astype(BF16)
            for c in range(RET_SUB // RET_CHUNK):
                rows = slice(c * RET_CHUNK, (c + 1) * RET_CHUNK)
                qc, kc, vc = q[rows], k[rows], v[rows]
                st = state_ref[hd]
                scores = _dot_nt(qc, kc.astype(BF16)) * dmask_ref[hd]
                inner = _dot(scores.astype(BF16), vc)
                cross = _dot(qc, st.astype(BF16)) * qdec_ref[hd]
                state_ref[hd] = st * cdec_ref[hd] + _dot_tn((kc * kdec_ref[hd]).astype(BF16), vc)
                y = inner + cross
                y_ref[rows, hd * RET_V_DIM:(hd + 1) * RET_V_DIM] = y * lax.rsqrt(
                    jnp.mean(y * y, axis=-1, keepdims=True) + EPS)
        gate = _dot(a, win_ref[:, RET_G_OFF:RET_G_OFF + RET_HEADS * RET_V_DIM])
        z = (gate * jax.nn.sigmoid(gate) * y_ref[...]).astype(BF16)
        o_ref[prow, :] = x + _rms(_dot(z, wout_ref[...]), g1_ref[...])


def _retention_consts(seq_len):
    hh, c, dk = RET_HEADS, RET_CHUNK, RET_QK_DIM
    inv_freq = np.float32(ROPE_BASE) ** (-np.arange(0, dk, 2, dtype=np.float32) / np.float32(dk))
    ang = np.arange(seq_len, dtype=np.float32)[:, None] * inv_freq[None, :]
    cos = np.cos(ang.astype(np.float64)).astype(np.float32)
    sin = np.sin(ang.astype(np.float64)).astype(np.float32)
    log_gamma = jnp.log1p(-jnp.exp2(-5.0 - jnp.arange(hh, dtype=F32)))
    idx = jnp.arange(c, dtype=F32)
    diff = idx[:, None] - idx[None, :]
    dmask = jnp.where(diff >= 0, jnp.exp(log_gamma[:, None, None] * jnp.maximum(diff, 0.0)), 0.0)
    qdec = jnp.exp(log_gamma[:, None] * (idx[None, :] + 1.0))[:, :, None]
    kdec = jnp.exp(log_gamma[:, None] * (c - 1.0 - idx[None, :]))[:, :, None]
    cdec = jnp.exp(log_gamma * c)[:, None, None]
    return jnp.asarray(cos), jnp.asarray(sin), dmask, qdec, kdec, cdec


def _retention_mixer(h, gains, w_in, w_out, consts, layer, j):
    b, s, d = h.shape
    cos, sin, dmask, qdec, kdec, cdec = consts
    tile = lambda bi, t: (bi, t, 0)
    pos = lambda bi, t: (t, 0)
    gain = lambda k: _layer_resident(gains.shape, layer * N_NORMS + k)
    specs = (_weight(w_in, j), _weight(w_out, j))
    return pl.pallas_call(
        functools.partial(_retention_kernel, specs=specs),
        grid=(b, s // RET_TM),
        in_specs=[
            pl.BlockSpec((None, RET_TM, d), tile),
            pl.BlockSpec((RET_TM, RET_QK_DIM // 2), pos),
            pl.BlockSpec((RET_TM, RET_QK_DIM // 2), pos),
            gain(0), gain(1),
            HBM, HBM,
            _resident(dmask.shape), _resident(qdec.shape), _resident(kdec.shape), _resident(cdec.shape),
        ],
        out_specs=pl.BlockSpec((None, RET_TM, d), tile),
        out_shape=jax.ShapeDtypeStruct((b, s, d), F32),
        scratch_shapes=[
            pltpu.VMEM((RET_HEADS, RET_QK_DIM, RET_V_DIM), F32),
            pltpu.VMEM((RET_SUB, RET_HEADS * RET_V_DIM), F32),
        ] + _weight_scratch(specs),
        compiler_params=_params("arbitrary", "arbitrary"),
        name="retention_mixer",
    )(h, cos, sin, gains, gains, w_in, w_out, dmask, qdec, kdec, cdec)


GM_TM = 512


def _gmlp_kernel(h_ref, g0_ref, g1_ref, win_hbm, lng_ref, lnb_ref, ws_ref, bs_ref, wout_hbm, o_ref, s_ref, *wscr,
                 specs):
    win_ref, wout_ref = _resident_weights(pl.program_id(0) == 0, specs, (win_hbm, wout_hbm), wscr)
    x = h_ref[...]
    a = _rms(x, g0_ref[...]).astype(BF16)

    def gelu(t):
        return 0.5 * t * (1.0 + lax.erf(t * (2.0 ** -0.5)))

    v = gelu(_dot(a, win_ref[:, GM_WIDTH:]))
    u = gelu(_dot(a, win_ref[:, :GM_WIDTH]))
    mu = jnp.mean(v, axis=-1, keepdims=True)
    vc = v - mu
    var = jnp.mean(vc * vc, axis=-1, keepdims=True)
    vn = (vc * lax.rsqrt(var + EPS) * lng_ref[...] + lnb_ref[...]).astype(BF16)
    row = lax.broadcasted_iota(jnp.int32, (GM_CHUNK, GM_CHUNK), 0)
    col = lax.broadcasted_iota(jnp.int32, (GM_CHUNK, GM_CHUNK), 1)
    causal = row >= col
    for g in range(GM_GROUPS):
        w = jnp.where(causal, ws_ref[g], 0.0).astype(BF16)
        cols = slice(g * GM_GROUP_DIM, (g + 1) * GM_GROUP_DIM)
        for c in range(GM_TM // GM_CHUNK):
            rows = slice(c * GM_CHUNK, (c + 1) * GM_CHUNK)
            s_ref[rows, cols] = _dot(w, vn[rows, cols]) + bs_ref[g]
    z = (u * s_ref[...]).astype(BF16)
    o_ref[...] = x + _rms(_dot(z, wout_ref[...]), g1_ref[...])


def _gmlp_mixer(h, gains, w_in, ln_g, ln_b, w_s, b_s, w_out, layer, j):
    n, d = h.shape
    row = lambda i: (i, 0)
    gain = lambda k: _layer_resident(gains.shape, layer * N_NORMS + k)
    specs = (_weight(w_in, j), _weight(w_out, j))
    return pl.pallas_call(
        functools.partial(_gmlp_kernel, specs=specs),
        grid=(n // GM_TM,),
        in_specs=[
            pl.BlockSpec((GM_TM, d), row),
            gain(0), gain(1),
            HBM, _layer_resident(ln_g.shape, j), _layer_resident(ln_b.shape, j),
            _layer_resident(w_s.shape, j), _layer_resident(b_s.shape, j), HBM,
        ],
        out_specs=pl.BlockSpec((GM_TM, d), row),
        out_shape=jax.ShapeDtypeStruct((n, d), F32),
        scratch_shapes=[pltpu.VMEM((GM_TM, GM_WIDTH), F32)] + _weight_scratch(specs),
        compiler_params=_params("arbitrary"),
        name="gmlp_mixer",
    )(h, gains, gains, w_in, ln_g, ln_b, w_s, b_s, w_out)


ATT_D = ATT_HEADS * ATT_HEAD_DIM


class _GroupPlan(NamedTuple):
    dilation: int
    view_minor: int
    block_rows: int
    halves: int
    n_units: int
    n_slots: int
    pieces: tuple
    first_units: tuple
    positions: np.ndarray

    @property
    def tokens_per_step(self):
        return self.n_units * ATT_BLK


def _group_plan(dilation):
    if dilation == 1:
        return _GroupPlan(1, 0, 8 * ATT_BLK, 1, 8, 1,
                          tuple(((None, u * ATT_BLK, ATT_BLK),) for u in range(8)), (0,),
                          np.arange(ATT_BLK))
    if dilation == 4:
        half = ATT_BLK // 2
        return _GroupPlan(4, 8, ATT_BLK, 1, 8, 4,
                          tuple(((c, blk * half, half), (c + 4, blk * half, half))
                                for blk in range(2) for c in range(4)),
                          (0, 1, 2, 3),
                          np.concatenate([2 * np.arange(half), 2 * np.arange(half) + 1]))
    assert dilation == 16
    return _GroupPlan(16, 16, ATT_BLK, 2, 8, 8,
                      tuple(((c, 0, ATT_BLK),) for c in range(8)), tuple(range(8)),
                      np.arange(ATT_BLK))


def _t5_bucket(dist):
    d = jnp.maximum(dist, 1).astype(F32)
    large = REL_MAX_EXACT + (jnp.log(d / REL_MAX_EXACT) / math.log(REL_MAX_DIST / REL_MAX_EXACT)
                             * (REL_BUCKETS - REL_MAX_EXACT)).astype(jnp.int32)
    large = jnp.minimum(large, REL_BUCKETS - 1)
    return jnp.where(dist < REL_MAX_EXACT, dist, large)


def _bias_kernel(table_ref, bucket_ref, dist_ref, o_ref):
    g = pl.program_id(0)
    bucket = bucket_ref[...]
    dist = dist_ref[...]
    band = (dist >= 0) & (dist <= ATT_BLK)
    for hd in range(ATT_HEADS):
        acc = jnp.zeros(bucket.shape, F32)
        for b in range(REL_BUCKETS):
            acc = jnp.where(bucket == b, table_ref[b, g * ATT_HEADS + hd], acc)
        o_ref[hd] = jnp.where(band, acc * LOG2E, MASK_VALUE)


def _attention_bias(rel_bias, plans):
    dists = []
    for plan in plans:
        pos_k = np.concatenate([plan.positions - ATT_BLK, plan.positions])
        dists.append(plan.positions[:, None] - pos_k[None, :])
    dist = jnp.asarray(np.stack(dists), jnp.int32)
    scale = jnp.asarray([plan.dilation for plan in plans], jnp.int32)[:, None, None]
    buckets = _t5_bucket(jnp.maximum(dist, 0) * scale).astype(jnp.int32)
    tile = pl.BlockSpec((None, ATT_BLK, 2 * ATT_BLK), lambda g: (g, 0, 0))
    return pl.pallas_call(
        _bias_kernel,
        grid=(len(plans),),
        in_specs=[pl.BlockSpec(memory_space=pltpu.SMEM), tile, tile],
        out_specs=pl.BlockSpec((None, ATT_HEADS, ATT_BLK, 2 * ATT_BLK), lambda g: (g, 0, 0, 0)),
        out_shape=jax.ShapeDtypeStruct((len(plans), ATT_HEADS, ATT_BLK, 2 * ATT_BLK), F32),
        compiler_params=_params("arbitrary"),
        name="attention_bias",
    )(rel_bias, buckets, dist)


def _attn_units(plan, t, xa, g0_ref, w_ref, bias_ref, k_ref, v_ref, q_ref, put_o, put_lse):
    @pl.when(t == 0)
    def _():
        k_ref[:, :ATT_BLK, :] = jnp.zeros((plan.n_slots, ATT_BLK, ATT_D), BF16)
        v_ref[:, :ATT_BLK, :] = jnp.zeros((plan.n_slots, ATT_BLK, ATT_D), BF16)

    a = _rms(xa, g0_ref[...]).astype(BF16)
    q_ref[...] = (_dot(a, w_ref[:, :ATT_D]) * (ATT_HEAD_DIM ** -0.5 * LOG2E)).astype(BF16)
    kc = _dot(a, w_ref[:, ATT_D:2 * ATT_D]).astype(BF16)
    vc = _dot(a, w_ref[:, 2 * ATT_D:]).astype(BF16)

    lane = lax.broadcasted_iota(jnp.int32, (ATT_BLK, V7X_LANES), 1)
    low = lane < ATT_HEAD_DIM
    kcol = lax.broadcasted_iota(jnp.int32, (ATT_BLK, 2 * ATT_BLK), 1)
    first_pen = jnp.where((kcol < ATT_BLK) & (t == 0), MASK_VALUE, 0.0)
    for u in range(plan.n_units):
        slot = u % plan.n_slots
        rows = slice(u * ATT_BLK, (u + 1) * ATT_BLK)
        k_ref[slot, ATT_BLK:, :] = kc[rows]
        v_ref[slot, ATT_BLK:, :] = vc[rows]
        m_tile = jnp.zeros((ATT_BLK, V7X_LANES), F32)
        den_tile = jnp.ones((ATT_BLK, V7X_LANES), F32)
        for hp in range(ATT_HEADS // 2):
            cols = slice(hp * V7X_LANES, (hp + 1) * V7X_LANES)
            qp = q_ref[rows, cols]
            kk = k_ref[slot, :, cols]
            vv = v_ref[slot, :, cols]
            outs, lses = [], []
            for sub in range(2):
                qm = jnp.where(low if sub == 0 else ~low, qp, jnp.zeros_like(qp))
                s = _dot_nt(qm, kk) + bias_ref[2 * hp + sub]
                if u in plan.first_units:
                    s = s + first_pen
                m = jnp.max(s, axis=-1, keepdims=True)
                e = jnp.exp2(s - m)
                den = jnp.sum(e, axis=-1, keepdims=True)
                outs.append(_dot(e.astype(BF16), vv) / den)
                lses.append((m, den))
            put_o(rows, cols, jnp.where(low, outs[0], outs[1]))
            m_tile = jnp.where(lane == 2 * hp, lses[0][0], jnp.where(lane == 2 * hp + 1, lses[1][0], m_tile))
            den_tile = jnp.where(lane == 2 * hp, lses[0][1], jnp.where(lane == 2 * hp + 1, lses[1][1], den_tile))
        put_lse(rows, m_tile * LN2 + jnp.log(den_tile))
        k_ref[slot, :ATT_BLK, :] = k_ref[slot, ATT_BLK:, :]
        v_ref[slot, :ATT_BLK, :] = v_ref[slot, ATT_BLK:, :]


def _attn_dense_kernel(x_ref, g0_ref, w_hbm, bias_ref, o_ref, lse_ref, k_ref, v_ref, q_ref, *wscr, plan, specs):
    (w_ref,) = _resident_weights((pl.program_id(0) == 0) & (pl.program_id(2) == 0), specs, (w_hbm,), wscr)

    def put_o(rows, cols, tile):
        o_ref[rows, cols] = tile.astype(o_ref.dtype)

    def put_lse(rows, tile):
        lse_ref[rows, :] = tile

    _attn_units(plan, pl.program_id(2), x_ref[...], g0_ref, w_ref, bias_ref, k_ref, v_ref, q_ref, put_o, put_lse)


def _attn_strided_kernel(x_hbm, g0_ref, w_hbm, bias_ref, o_hbm, lse_hbm, k_ref, v_ref, q_ref,
                         xbuf, obuf, lbuf, xsem, osem, lsem, *wscr, plan, specs, n_batch, n_steps):
    b, hf, t = pl.program_id(0), pl.program_id(1), pl.program_id(2)
    step = (b * plan.halves + hf) * n_steps + t
    n_total = n_batch * plan.halves * n_steps
    buf = lax.rem(step, 2)

    def class_rows(ref, bb, hh, tt, sub, r0, n):
        return ref.at[bb, pl.ds(tt * plan.block_rows + r0, n), hh * V7X_SUBLANES + sub]

    def copies(kind, bb, hh, tt, which):
        out = []
        for u, pcs in enumerate(plan.pieces):
            at = u * ATT_BLK
            for sub, r0, n in pcs:
                dense = pl.ds(at, n)
                if kind == "gather":
                    out.append(pltpu.make_async_copy(class_rows(x_hbm, bb, hh, tt, sub, r0, n),
                                                     xbuf.at[which, dense], xsem.at[which]))
                else:
                    out.append(pltpu.make_async_copy(obuf.at[which, dense],
                                                     class_rows(o_hbm, bb, hh, tt, sub, r0, n), osem.at[which]))
                    out.append(pltpu.make_async_copy(lbuf.at[which, dense],
                                                     class_rows(lse_hbm, bb, hh, tt, sub, r0, n), lsem.at[which]))
                at += n
        return out

    @pl.when(step == 0)
    def _():
        for cp in copies("gather", b, hf, t, buf):
            cp.start()

    wrap_t = t + 1 == n_steps
    wrap_h = hf + 1 == plan.halves
    nt = jnp.where(wrap_t, 0, t + 1)
    nh = jnp.where(wrap_t, jnp.where(wrap_h, 0, hf + 1), hf)
    nb = jnp.where(wrap_t & wrap_h, b + 1, b)

    @pl.when(step + 1 < n_total)
    def _():
        for cp in copies("gather", nb, nh, nt, 1 - buf):
            cp.start()

    (w_ref,) = _resident_weights(step == 0, specs, (w_hbm,), wscr)

    for cp in copies("gather", b, hf, t, buf):
        cp.wait()

    @pl.when(step >= 2)
    def _():
        for cp in copies("scatter", b, hf, t, buf):
            cp.wait()

    def put_o(rows, cols, tile):
        obuf[buf, rows, cols] = tile

    def put_lse(rows, tile):
        lbuf[buf, rows, :] = tile

    _attn_units(plan, t, xbuf[buf], g0_ref, w_ref, bias_ref, k_ref, v_ref, q_ref, put_o, put_lse)

    for cp in copies("scatter", b, hf, t, buf):
        cp.start()

    @pl.when(step == n_total - 1)
    def _():
        for cp in copies("scatter", b, hf, t, 1 - buf):
            cp.wait()
        for cp in copies("scatter", b, hf, t, buf):
            cp.wait()


def _attention_group(h, gains, w_in, bias, plan, group, layer, j):
    b, s, d = h.shape
    tokens = plan.tokens_per_step
    specs = (_weight(w_in, j, col0=group * 3 * ATT_D, cols=3 * ATT_D),)
    resident = [
        _layer_resident(gains.shape, layer * N_NORMS),
        HBM,
        pl.BlockSpec((None, ATT_HEADS, ATT_BLK, 2 * ATT_BLK), lambda bi, hf, t: (group, 0, 0, 0),
                     pipeline_mode=pl.Buffered(1)),
    ]
    kv_q = [
        pltpu.VMEM((plan.n_slots, 2 * ATT_BLK, ATT_D), BF16),
        pltpu.VMEM((plan.n_slots, 2 * ATT_BLK, ATT_D), BF16),
        pltpu.VMEM((tokens, ATT_D), BF16),
    ]
    if not plan.view_minor:
        steps = s // plan.block_rows
        block = lambda width: pl.BlockSpec((None, plan.block_rows, width), lambda bi, hf, t: (bi, t, 0))
        o, lse = pl.pallas_call(
            functools.partial(_attn_dense_kernel, plan=plan, specs=specs),
            grid=(b, 1, steps),
            in_specs=[block(d)] + resident,
            out_specs=[block(ATT_D), block(V7X_LANES)],
            out_shape=[jax.ShapeDtypeStruct((b, s, ATT_D), BF16), jax.ShapeDtypeStruct((b, s, V7X_LANES), F32)],
            scratch_shapes=kv_q + _weight_scratch(specs),
            compiler_params=_params("arbitrary", "arbitrary", "arbitrary"),
            name=f"attention_group{group}",
        )(h, gains, w_in, bias)
    else:
        lead = s // plan.view_minor
        steps = lead // plan.block_rows
        assert b * plan.halves * steps >= 2
        view = lambda width: (b, lead, plan.view_minor, width)
        o, lse = pl.pallas_call(
            functools.partial(_attn_strided_kernel, plan=plan, specs=specs, n_batch=b, n_steps=steps),
            grid=(b, plan.halves, steps),
            in_specs=[HBM] + resident,
            out_specs=[HBM, HBM],
            out_shape=[jax.ShapeDtypeStruct(view(ATT_D), F32), jax.ShapeDtypeStruct(view(V7X_LANES), F32)],
            scratch_shapes=kv_q + [
                pltpu.VMEM((2, tokens, d), F32),
                pltpu.VMEM((2, tokens, ATT_D), F32),
                pltpu.VMEM((2, tokens, V7X_LANES), F32),
                pltpu.SemaphoreType.DMA((2,)), pltpu.SemaphoreType.DMA((2,)), pltpu.SemaphoreType.DMA((2,)),
            ] + _weight_scratch(specs),
            compiler_params=_params("arbitrary", "arbitrary", "arbitrary"),
            name=f"attention_group{group}",
        )(h.reshape(view(d)), gains, w_in, bias)
    return o.reshape(b * s, ATT_D), lse.reshape(b * s, V7X_LANES)


ATT_MERGE_TM = 512


def _attn_merge_kernel(h_ref, o0_ref, o1_ref, o2_ref, l0_ref, l1_ref, l2_ref, g1_ref, wout_hbm, expand_ref,
                       out_ref, *wscr, specs):
    (wout_ref,) = _resident_weights(pl.program_id(0) == 0, specs, (wout_hbm,), wscr)
    l0, l1, l2 = l0_ref[...], l1_ref[...], l2_ref[...]
    m = jnp.maximum(jnp.maximum(l0, l1), l2)
    e0, e1, e2 = jnp.exp(l0 - m), jnp.exp(l1 - m), jnp.exp(l2 - m)
    inv = 1.0 / (e0 + e1 + e2)
    expand = expand_ref[...]

    def per_channel(w):
        hi = w.astype(BF16)
        lo = (w - hi.astype(F32)).astype(BF16)
        return _dot(jnp.concatenate([hi, lo], axis=1), expand)

    o = (per_channel(e0 * inv) * o0_ref[...].astype(F32) + per_channel(e1 * inv) * o1_ref[...]
         + per_channel(e2 * inv) * o2_ref[...])
    out_ref[...] = h_ref[...] + _rms(_dot(o.astype(BF16), wout_ref[...]), g1_ref[...])


def _attention_merge(h, outs, lses, gains, w_out, layer, j):
    n, d = h.shape
    row = lambda i: (i, 0)
    tile = pl.BlockSpec((ATT_MERGE_TM, d), row)
    lse_tile = pl.BlockSpec((ATT_MERGE_TM, V7X_LANES), row)
    expand = np.zeros((V7X_LANES, ATT_D), np.float32)
    for hd in range(ATT_HEADS):
        expand[hd, hd * ATT_HEAD_DIM:(hd + 1) * ATT_HEAD_DIM] = 1.0
    expand = jnp.asarray(np.concatenate([expand, expand], axis=0), BF16)
    specs = (_weight(w_out, j),)
    return pl.pallas_call(
        functools.partial(_attn_merge_kernel, specs=specs),
        grid=(n // ATT_MERGE_TM,),
        in_specs=[tile] * 4 + [lse_tile] * 3 + [
            _layer_resident(gains.shape, layer * N_NORMS + 1), HBM, _resident(expand.shape)],
        out_specs=tile,
        out_shape=jax.ShapeDtypeStruct((n, d), F32),
        scratch_shapes=_weight_scratch(specs),
        compiler_params=_params("arbitrary"),
        name="attention_merge",
    )(h, *outs, *lses, gains, w_out, expand)


def kernel(x, p, norm_g, ret_w_in, ret_w_out, attn_w_in, attn_w_out, rel_bias, gm_w_in, gm_ln_g, gm_ln_b,
           gm_w_s, gm_b_s, gm_w_out, ffn_w_in, ffn_w_out, ple_w_proj, ple_w_gate):
    b, s, d = x.shape
    n = b * s
    gains = norm_g.reshape(DEPTH * N_NORMS, 1, d)
    p = p.reshape(DEPTH, n, PLE_DIM)
    gm_ln_g = gm_ln_g.reshape(-1, 1, GM_WIDTH)
    gm_ln_b = gm_ln_b.reshape(-1, 1, GM_WIDTH)
    gm_b_s = gm_b_s.reshape(-1, GM_GROUPS, GM_CHUNK, 1)
    ret_consts = _retention_consts(s)
    plans = tuple(_group_plan(dil) for _, dil in DILATION_PAIRS)

    h = x
    for i in range(DEPTH):
        kind, j = i % N_MIXERS, i // N_MIXERS
        if kind == 0:
            h = _retention_mixer(h.reshape(b, s, d), gains, ret_w_in, ret_w_out, ret_consts, i, j)
        elif kind == 1:
            bias = _attention_bias(rel_bias, plans)
            outs, lses = [], []
            for gi, plan in enumerate(plans):
                o, lse = _attention_group(h.reshape(b, s, d), gains, attn_w_in, bias, plan, gi, i, j)
                outs.append(o)
                lses.append(lse)
            h = _attention_merge(h.reshape(n, d), outs, lses, gains, attn_w_out, i, j)
        else:
            h = _gmlp_mixer(h.reshape(n, d), gains, gm_w_in, gm_ln_g, gm_ln_b, gm_w_s, gm_b_s, gm_w_out, i, j)
        h = _ffn_ple(h.reshape(n, d), p, gains, ffn_w_in, ffn_w_out, ple_w_proj, ple_w_gate, i)
    return h.reshape(b, s, d)
```

```python
import functools
import math
from typing import NamedTuple

import jax
import jax.numpy as jnp
import numpy as np
from jax import lax
from jax.experimental import pallas as pl
from jax.experimental.pallas import tpu as pltpu

D_MODEL = 1024
DEPTH = 4
N_MIXERS = 3
N_NORMS = 5
PLE_DIM = 256
EPS = 1e-6

RET_HEADS = 4
RET_QK_DIM = 256
RET_V_DIM = 512
RET_CHUNK = 256
ROPE_BASE = 10000.0

ATT_HEADS = 16
ATT_HEAD_DIM = 64
DILATION_PAIRS = ((128, 1), (512, 4), (2048, 16))
ATT_BLK = 128
REL_BUCKETS = 32
REL_MAX_EXACT = 16
REL_MAX_DIST = 2048

GM_CHUNK = 128
GM_WIDTH = 2 * D_MODEL
GM_GROUPS = 8
GM_GROUP_DIM = GM_WIDTH // GM_GROUPS

FFN_HIDDEN = 2816

V7X_LANES = 128
V7X_SUBLANES = 8
V7X_VMEM_BYTES = 64 * 1024 * 1024
VMEM_LIMIT_BYTES = V7X_VMEM_BYTES - 8 * 1024 * 1024

MASK_VALUE = -1e30
LOG2E = math.log2(math.e)
LN2 = math.log(2.0)

BF16 = jnp.bfloat16
F32 = jnp.float32


def _resident(shape):
    return pl.BlockSpec(shape, lambda *_: (0,) * len(shape), pipeline_mode=pl.Buffered(1))


def _layer_resident(shape, layer):
    rest = tuple(shape[1:])
    return pl.BlockSpec((None,) + rest, lambda *_: (layer,) + (0,) * len(rest), pipeline_mode=pl.Buffered(1))


def _rms(x, g):
    return x * lax.rsqrt(jnp.mean(x * x, axis=-1, keepdims=True) + EPS) * g


def _dot(a, b):
    return jnp.dot(a, b, preferred_element_type=F32)


def _dot_nt(a, b):
    return lax.dot_general(a, b, (((1,), (1,)), ((), ())), preferred_element_type=F32)


def _dot_tn(a, b):
    return lax.dot_general(a, b, (((0,), (0,)), ((), ())), preferred_element_type=F32)


def _params(*semantics):
    return pltpu.CompilerParams(dimension_semantics=semantics, vmem_limit_bytes=VMEM_LIMIT_BYTES)


WEIGHT_CHUNK = (256, 512)
WEIGHT_SLOTS = 12

HBM = pl.BlockSpec(memory_space=pl.ANY)


class _WeightSpec(NamedTuple):
    layer: int
    rows: int
    cols: int
    col0: int


def _weight(array, layer, col0=0, cols=None):
    spec = _WeightSpec(layer, array.shape[1], array.shape[2] if cols is None else cols, col0)
    assert spec.rows % WEIGHT_CHUNK[0] == 0 and spec.cols % WEIGHT_CHUNK[1] == 0
    return spec


def _weight_scratch(specs):
    return ([pltpu.VMEM((spec.rows, spec.cols), BF16) for spec in specs]
            + [pltpu.VMEM((WEIGHT_SLOTS,) + WEIGHT_CHUNK, F32), pltpu.SemaphoreType.DMA((WEIGHT_SLOTS,))])


def _resident_weights(first, specs, hbm_refs, scratch):
    *w_refs, stage, sem = scratch
    cr, cc = WEIGHT_CHUNK
    chunks = [(w_hbm, w_ref, spec, r, c)
              for spec, w_hbm, w_ref in zip(specs, hbm_refs, w_refs)
              for r in range(0, spec.rows, cr) for c in range(0, spec.cols, cc)]

    def copy(k):
        w_hbm, _, spec, r, c = chunks[k]
        src = w_hbm.at[spec.layer, pl.ds(r, cr), pl.ds(spec.col0 + c, cc)]
        return pltpu.make_async_copy(src, stage.at[k % WEIGHT_SLOTS], sem.at[k % WEIGHT_SLOTS])

    @pl.when(first)
    def _():
        for k in range(min(WEIGHT_SLOTS, len(chunks))):
            copy(k).start()
        for k, (_, w_ref, _, r, c) in enumerate(chunks):
            copy(k).wait()
            w_ref[r:r + cr, c:c + cc] = stage[k % WEIGHT_SLOTS].astype(BF16)
            if k + WEIGHT_SLOTS < len(chunks):
                copy(k + WEIGHT_SLOTS).start()

    return w_refs


FFN_TM = 1024
FFN_SUB = 512


def _ffn_ple_kernel(h_ref, p_ref, g2_ref, g3_ref, g4_ref, win_hbm, wout_hbm, wp_hbm, wg_hbm, o_ref, *wscr, specs):
    win_ref, wout_ref, wp_ref, wg_ref = _resident_weights(
        pl.program_id(0) == 0, specs, (win_hbm, wout_hbm, wp_hbm, wg_hbm), wscr)
    for part in range(FFN_TM // FFN_SUB):
        rows = slice(part * FFN_SUB, (part + 1) * FFN_SUB)
        x = h_ref[rows, :]
        a = _rms(x, g2_ref[...]).astype(BF16)
        gate = _dot(a, win_ref[:, :FFN_HIDDEN])
        up = _dot(a, win_ref[:, FFN_HIDDEN:])
        hid = (gate * jax.nn.sigmoid(gate) * up).astype(BF16)
        h1 = x + _rms(_dot(hid, wout_ref[...]), g3_ref[...])
        emb = _dot(p_ref[rows, :].astype(BF16), wp_ref[...])
        gate = jax.nn.sigmoid(_dot(_rms(h1, g4_ref[...]).astype(BF16), wg_ref[...]))
        o_ref[rows, :] = h1 + gate * emb


def _ffn_ple(h, p, gains, w_in, w_out, w_proj, w_gate, layer):
    n, d = h.shape
    row = lambda i: (i, 0)
    gain = lambda k: _layer_resident(gains.shape, layer * N_NORMS + k)
    specs = tuple(_weight(w, layer) for w in (w_in, w_out, w_proj, w_gate))
    return pl.pallas_call(
        functools.partial(_ffn_ple_kernel, specs=specs),
        grid=(n // FFN_TM,),
        in_specs=[
            pl.BlockSpec((FFN_TM, d), row),
            pl.BlockSpec((None, FFN_TM, PLE_DIM), lambda i: (layer, i, 0)),
            gain(2), gain(3), gain(4),
            HBM, HBM, HBM, HBM,
        ],
        out_specs=pl.BlockSpec((FFN_TM, d), row),
        out_shape=jax.ShapeDtypeStruct((n, d), F32),
        scratch_shapes=_weight_scratch(specs),
        compiler_params=_params("arbitrary"),
        name="ffn_ple",
    )(h, p, gains, gains, gains, w_in, w_out, w_proj, w_gate)


RET_TM = 512
RET_SUB = 512
RET_Q_OFF = 0
RET_K_OFF = RET_HEADS * RET_QK_DIM
RET_V_OFF = 2 * RET_HEADS * RET_QK_DIM
RET_G_OFF = RET_V_OFF + RET_HEADS * RET_V_DIM


def _rope(t, cos, sin):
    half = RET_QK_DIM // 2
    t1, t2 = t[:, :half], t[:, half:]
    return jnp.concatenate([t1 * cos - t2 * sin, t2 * cos + t1 * sin], axis=-1)


def _retention_kernel(h_ref, cos_ref, sin_ref, g0_ref, g1_ref, win_hbm, wout_hbm,
                      dmask_ref, qdec_ref, kdec_ref, cdec_ref, o_ref, state_ref, y_ref, *wscr, specs):
    win_ref, wout_ref = _resident_weights(
        (pl.program_id(0) == 0) & (pl.program_id(1) == 0), specs, (win_hbm, wout_hbm), wscr)

    @pl.when(pl.program_id(1) == 0)
    def _():
        state_ref[...] = jnp.zeros(state_ref.shape, F32)

    for part in range(RET_TM // RET_SUB):
        prow = slice(part * RET_SUB, (part + 1) * RET_SUB)
        x = h_ref[prow, :]
        a = _rms(x, g0_ref[...]).astype(BF16)
        cos, sin = cos_ref[prow, :], sin_ref[prow, :]
        for hd in range(RET_HEADS):
            qo, ko, vo = RET_Q_OFF + hd * RET_QK_DIM, RET_K_OFF + hd * RET_QK_DIM, RET_V_OFF + hd * RET_V_DIM
            q = _rope(_dot(a, win_ref[:, qo:qo + RET_QK_DIM]), cos, sin).astype(BF16)
            k = _rope(_dot(a, win_ref[:, ko:ko + RET_QK_DIM]), cos, sin) * (RET_QK_DIM ** -0.5)
            v = _dot(a, win_ref[:, vo:vo + RET_V_DIM]).astype(BF16)
            for c in range(RET_SUB // RET_CHUNK):
                rows = slice(c * RET_CHUNK, (c + 1) * RET_CHUNK)
                qc, kc, vc = q[rows], k[rows], v[rows]
                st = state_ref[hd]
                scores = _dot_nt(qc, kc.astype(BF16)) * dmask_ref[hd]
                inner = _dot(scores.astype(BF16), vc)
                cross = _dot(qc, st.astype(BF16)) * qdec_ref[hd]
                state_ref[hd] = st * cdec_ref[hd] + _dot_tn((kc * kdec_ref[hd]).astype(BF16), vc)
                y = inner + cross
                y_ref[rows, hd * RET_V_DIM:(hd + 1) * RET_V_DIM] = y * lax.rsqrt(
                    jnp.mean(y * y, axis=-1, keepdims=True) + EPS)
        gate = _dot(a, win_ref[:, RET_G_OFF:RET_G_OFF + RET_HEADS * RET_V_DIM])
        z = (gate * jax.nn.sigmoid(gate) * y_ref[...]).astype(BF16)
        o_ref[prow, :] = x + _rms(_dot(z, wout_ref[...]), g1_ref[...])


def _retention_consts(seq_len):
    hh, c, dk = RET_HEADS, RET_CHUNK, RET_QK_DIM
    inv_freq = np.float32(ROPE_BASE) ** (-np.arange(0, dk, 2, dtype=np.float32) / np.float32(dk))
    ang = np.arange(seq_len, dtype=np.float32)[:, None] * inv_freq[None, :]
    cos = np.cos(ang.astype(np.float64)).astype(np.float32)
    sin = np.sin(ang.astype(np.float64)).astype(np.float32)
    log_gamma = jnp.log1p(-jnp.exp2(-5.0 - jnp.arange(hh, dtype=F32)))
    idx = jnp.arange(c, dtype=F32)
    diff = idx[:, None] - idx[None, :]
    dmask = jnp.where(diff >= 0, jnp.exp(log_gamma[:, None, None] * jnp.maximum(diff, 0.0)), 0.0)
    qdec = jnp.exp(log_gamma[:, None] * (idx[None, :] + 1.0))[:, :, None]
    kdec = jnp.exp(log_gamma[:, None] * (c - 1.0 - idx[None, :]))[:, :, None]
    cdec = jnp.exp(log_gamma * c)[:, None, None]
    return jnp.asarray(cos), jnp.asarray(sin), dmask, qdec, kdec, cdec


def _retention_mixer(h, gains, w_in, w_out, consts, layer, j):
    b, s, d = h.shape
    cos, sin, dmask, qdec, kdec, cdec = consts
    tile = lambda bi, t: (bi, t, 0)
    pos = lambda bi, t: (t, 0)
    gain = lambda k: _layer_resident(gains.shape, layer * N_NORMS + k)
    specs = (_weight(w_in, j), _weight(w_out, j))
    return pl.pallas_call(
        functools.partial(_retention_kernel, specs=specs),
        grid=(b, s // RET_TM),
        in_specs=[
            pl.BlockSpec((None, RET_TM, d), tile),
            pl.BlockSpec((RET_TM, RET_QK_DIM // 2), pos),
            pl.BlockSpec((RET_TM, RET_QK_DIM // 2), pos),
            gain(0), gain(1),
            HBM, HBM,
            _resident(dmask.shape), _resident(qdec.shape), _resident(kdec.shape), _resident(cdec.shape),
        ],
        out_specs=pl.BlockSpec((None, RET_TM, d), tile),
        out_shape=jax.ShapeDtypeStruct((b, s, d), F32),
        scratch_shapes=[
            pltpu.VMEM((RET_HEADS, RET_QK_DIM, RET_V_DIM), F32),
            pltpu.VMEM((RET_SUB, RET_HEADS * RET_V_DIM), F32),
        ] + _weight_scratch(specs),
        compiler_params=_params("arbitrary", "arbitrary"),
        name="retention_mixer",
    )(h, cos, sin, gains, gains, w_in, w_out, dmask, qdec, kdec, cdec)


GM_TM = 1024
GM_SUB = 512


def _gmlp_kernel(h_ref, g0_ref, g1_ref, win_hbm, lng_ref, lnb_ref, ws_ref, bs_ref, wout_hbm, o_ref, s_ref, *wscr,
                 specs):
    win_ref, wout_ref = _resident_weights(pl.program_id(0) == 0, specs, (win_hbm, wout_hbm), wscr)

    def gelu(t):
        return 0.5 * t * (1.0 + lax.erf(t * (2.0 ** -0.5)))

    row = lax.broadcasted_iota(jnp.int32, (GM_CHUNK, GM_CHUNK), 0)
    col = lax.broadcasted_iota(jnp.int32, (GM_CHUNK, GM_CHUNK), 1)
    causal = row >= col
    for part in range(GM_TM // GM_SUB):
        prow = slice(part * GM_SUB, (part + 1) * GM_SUB)
        x = h_ref[prow, :]
        a = _rms(x, g0_ref[...]).astype(BF16)
        v = gelu(_dot(a, win_ref[:, GM_WIDTH:]))
        u = gelu(_dot(a, win_ref[:, :GM_WIDTH]))
        mu = jnp.mean(v, axis=-1, keepdims=True)
        vc = v - mu
        var = jnp.mean(vc * vc, axis=-1, keepdims=True)
        vn = (vc * lax.rsqrt(var + EPS) * lng_ref[...] + lnb_ref[...]).astype(BF16)
        for g in range(GM_GROUPS):
            w = jnp.where(causal, ws_ref[g], 0.0).astype(BF16)
            cols = slice(g * GM_GROUP_DIM, (g + 1) * GM_GROUP_DIM)
            for c in range(GM_SUB // GM_CHUNK):
                rows = slice(c * GM_CHUNK, (c + 1) * GM_CHUNK)
                s_ref[rows, cols] = _dot(w, vn[rows, cols]) + bs_ref[g]
        z = (u * s_ref[...]).astype(BF16)
        o_ref[prow, :] = x + _rms(_dot(z, wout_ref[...]), g1_ref[...])


def _gmlp_mixer(h, gains, w_in, ln_g, ln_b, w_s, b_s, w_out, layer, j):
    n, d = h.shape
    row = lambda i: (i, 0)
    gain = lambda k: _layer_resident(gains.shape, layer * N_NORMS + k)
    specs = (_weight(w_in, j), _weight(w_out, j))
    return pl.pallas_call(
        functools.partial(_gmlp_kernel, specs=specs),
        grid=(n // GM_TM,),
        in_specs=[
            pl.BlockSpec((GM_TM, d), row),
            gain(0), gain(1),
            HBM, _layer_resident(ln_g.shape, j), _layer_resident(ln_b.shape, j),
            _layer_resident(w_s.shape, j), _layer_resident(b_s.shape, j), HBM,
        ],
        out_specs=pl.BlockSpec((GM_TM, d), row),
        out_shape=jax.ShapeDtypeStruct((n, d), F32),
        scratch_shapes=[pltpu.VMEM((GM_SUB, GM_WIDTH), F32)] + _weight_scratch(specs),
        compiler_params=_params("arbitrary"),
        name="gmlp_mixer",
    )(h, gains, gains, w_in, ln_g, ln_b, w_s, b_s, w_out)


ATT_D = ATT_HEADS * ATT_HEAD_DIM


class _GroupPlan(NamedTuple):
    dilation: int
    view_minor: int
    block_rows: int
    halves: int
    n_units: int
    n_slots: int
    pieces: tuple
    first_units: tuple
    positions: np.ndarray

    @property
    def tokens_per_step(self):
        return self.n_units * ATT_BLK


def _group_plan(dilation):
    if dilation == 1:
        return _GroupPlan(1, 0, 8 * ATT_BLK, 1, 8, 1,
                          tuple(((None, u * ATT_BLK, ATT_BLK),) for u in range(8)), (0,),
                          np.arange(ATT_BLK))
    if dilation == 4:
        half = ATT_BLK // 2
        return _GroupPlan(4, 8, ATT_BLK, 1, 8, 4,
                          tuple(((c, blk * half, half), (c + 4, blk * half, half))
                                for blk in range(2) for c in range(4)),
                          (0, 1, 2, 3),
                          np.concatenate([2 * np.arange(half), 2 * np.arange(half) + 1]))
    assert dilation == 16
    return _GroupPlan(16, 16, ATT_BLK, 2, 8, 8,
                      tuple(((c, 0, ATT_BLK),) for c in range(8)), tuple(range(8)),
                      np.arange(ATT_BLK))


def _t5_bucket(dist):
    d = jnp.maximum(dist, 1).astype(F32)
    large = REL_MAX_EXACT + (jnp.log(d / REL_MAX_EXACT) / math.log(REL_MAX_DIST / REL_MAX_EXACT)
                             * (REL_BUCKETS - REL_MAX_EXACT)).astype(jnp.int32)
    large = jnp.minimum(large, REL_BUCKETS - 1)
    return jnp.where(dist < REL_MAX_EXACT, dist, large)


def _bias_kernel(table_ref, bucket_ref, dist_ref, o_ref):
    g = pl.program_id(0)
    bucket = bucket_ref[...]
    dist = dist_ref[...]
    band = (dist >= 0) & (dist <= ATT_BLK)
    for hd in range(ATT_HEADS):
        acc = jnp.zeros(bucket.shape, F32)
        for b in range(REL_BUCKETS):
            acc = jnp.where(bucket == b, table_ref[b, g * ATT_HEADS + hd], acc)
        o_ref[hd] = jnp.where(band, acc * LOG2E, MASK_VALUE)


def _attention_bias(rel_bias, plans):
    dists = []
    for plan in plans:
        pos_k = np.concatenate([plan.positions - ATT_BLK, plan.positions])
        dists.append(plan.positions[:, None] - pos_k[None, :])
    dist = jnp.asarray(np.stack(dists), jnp.int32)
    scale = jnp.asarray([plan.dilation for plan in plans], jnp.int32)[:, None, None]
    buckets = _t5_bucket(jnp.maximum(dist, 0) * scale).astype(jnp.int32)
    tile = pl.BlockSpec((None, ATT_BLK, 2 * ATT_BLK), lambda g: (g, 0, 0))
    return pl.pallas_call(
        _bias_kernel,
        grid=(len(plans),),
        in_specs=[pl.BlockSpec(memory_space=pltpu.SMEM), tile, tile],
        out_specs=pl.BlockSpec((None, ATT_HEADS, ATT_BLK, 2 * ATT_BLK), lambda g: (g, 0, 0, 0)),
        out_shape=jax.ShapeDtypeStruct((len(plans), ATT_HEADS, ATT_BLK, 2 * ATT_BLK), F32),
        compiler_params=_params("arbitrary"),
        name="attention_bias",
    )(rel_bias, buckets, dist)


def _attn_units(plan, t, xa, g0_ref, w_ref, bias_ref, k_ref, v_ref, q_ref, put_o, put_lse):
    @pl.when(t == 0)
    def _():
        k_ref[:, :ATT_BLK, :] = jnp.zeros((plan.n_slots, ATT_BLK, ATT_D), BF16)
        v_ref[:, :ATT_BLK, :] = jnp.zeros((plan.n_slots, ATT_BLK, ATT_D), BF16)

    a = _rms(xa, g0_ref[...]).astype(BF16)
    q_ref[...] = (_dot(a, w_ref[:, :ATT_D]) * (ATT_HEAD_DIM ** -0.5 * LOG2E)).astype(BF16)
    kc = _dot(a, w_ref[:, ATT_D:2 * ATT_D]).astype(BF16)
    vc = _dot(a, w_ref[:, 2 * ATT_D:]).astype(BF16)

    lane = lax.broadcasted_iota(jnp.int32, (ATT_BLK, V7X_LANES), 1)
    low = lane < ATT_HEAD_DIM
    kcol = lax.broadcasted_iota(jnp.int32, (ATT_BLK, 2 * ATT_BLK), 1)
    first_pen = jnp.where((kcol < ATT_BLK) & (t == 0), MASK_VALUE, 0.0)
    for u in range(plan.n_units):
        slot = u % plan.n_slots
        rows = slice(u * ATT_BLK, (u + 1) * ATT_BLK)
        k_ref[slot, ATT_BLK:, :] = kc[rows]
        v_ref[slot, ATT_BLK:, :] = vc[rows]
        m_tile = jnp.zeros((ATT_BLK, V7X_LANES), F32)
        den_tile = jnp.ones((ATT_BLK, V7X_LANES), F32)
        for hp in range(ATT_HEADS // 2):
            cols = slice(hp * V7X_LANES, (hp + 1) * V7X_LANES)
            qp = q_ref[rows, cols]
            kk = k_ref[slot, :, cols]
            vv = v_ref[slot, :, cols]
            outs, lses = [], []
            for sub in range(2):
                qm = jnp.where(low if sub == 0 else ~low, qp, jnp.zeros_like(qp))
                s = _dot_nt(qm, kk) + bias_ref[2 * hp + sub]
                if u in plan.first_units:
                    s = s + first_pen
                m = jnp.max(s, axis=-1, keepdims=True)
                e = jnp.exp2(s - m)
                den = jnp.sum(e, axis=-1, keepdims=True)
                outs.append(_dot(e.astype(BF16), vv) / den)
                lses.append((m, den))
            put_o(rows, cols, jnp.where(low, outs[0], outs[1]))
            m_tile = jnp.where(lane == 2 * hp, lses[0][0], jnp.where(lane == 2 * hp + 1, lses[1][0], m_tile))
            den_tile = jnp.where(lane == 2 * hp, lses[0][1], jnp.where(lane == 2 * hp + 1, lses[1][1], den_tile))
        put_lse(rows, m_tile * LN2 + jnp.log(den_tile))
        k_ref[slot, :ATT_BLK, :] = k_ref[slot, ATT_BLK:, :]
        v_ref[slot, :ATT_BLK, :] = v_ref[slot, ATT_BLK:, :]


def _attn_dense_kernel(x_ref, g0_ref, w_hbm, bias_ref, o_ref, lse_ref, k_ref, v_ref, q_ref, *wscr, plan, specs):
    (w_ref,) = _resident_weights((pl.program_id(0) == 0) & (pl.program_id(2) == 0), specs, (w_hbm,), wscr)

    def put_o(rows, cols, tile):
        o_ref[rows, cols] = tile.astype(o_ref.dtype)

    def put_lse(rows, tile):
        lse_ref[rows, :] = tile

    _attn_units(plan, pl.program_id(2), x_ref[...], g0_ref, w_ref, bias_ref, k_ref, v_ref, q_ref, put_o, put_lse)


def _attn_strided_kernel(x_hbm, g0_ref, w_hbm, bias_ref, o_hbm, lse_hbm, k_ref, v_ref, q_ref,
                         xbuf, obuf, lbuf, xsem, osem, lsem, *wscr, plan, specs, n_batch, n_steps):
    b, hf, t = pl.program_id(0), pl.program_id(1), pl.program_id(2)
    step = (b * plan.halves + hf) * n_steps + t
    n_total = n_batch * plan.halves * n_steps
    buf = lax.rem(step, 2)

    def class_rows(ref, bb, hh, tt, sub, r0, n):
        return ref.at[bb, pl.ds(tt * plan.block_rows + r0, n), hh * V7X_SUBLANES + sub]

    def copies(kind, bb, hh, tt, which):
        out = []
        for u, pcs in enumerate(plan.pieces):
            at = u * ATT_BLK
            for sub, r0, n in pcs:
                dense = pl.ds(at, n)
                if kind == "gather":
                    out.append(pltpu.make_async_copy(class_rows(x_hbm, bb, hh, tt, sub, r0, n),
                                                     xbuf.at[which, dense], xsem.at[which]))
                else:
                    out.append(pltpu.make_async_copy(obuf.at[which, dense],
                                                     class_rows(o_hbm, bb, hh, tt, sub, r0, n), osem.at[which]))
                    out.append(pltpu.make_async_copy(lbuf.at[which, dense],
                                                     class_rows(lse_hbm, bb, hh, tt, sub, r0, n), lsem.at[which]))
                at += n
        return out

    @pl.when(step == 0)
    def _():
        for cp in copies("gather", b, hf, t, buf):
            cp.start()

    wrap_t = t + 1 == n_steps
    wrap_h = hf + 1 == plan.halves
    nt = jnp.where(wrap_t, 0, t + 1)
    nh = jnp.where(wrap_t, jnp.where(wrap_h, 0, hf + 1), hf)
    nb = jnp.where(wrap_t & wrap_h, b + 1, b)

    @pl.when(step + 1 < n_total)
    def _():
        for cp in copies("gather", nb, nh, nt, 1 - buf):
            cp.start()

    (w_ref,) = _resident_weights(step == 0, specs, (w_hbm,), wscr)

    for cp in copies("gather", b, hf, t, buf):
        cp.wait()

    @pl.when(step >= 2)
    def _():
        for cp in copies("scatter", b, hf, t, buf):
            cp.wait()

    def put_o(rows, cols, tile):
        obuf[buf, rows, cols] = tile

    def put_lse(rows, tile):
        lbuf[buf, rows, :] = tile

    _attn_units(plan, t, xbuf[buf], g0_ref, w_ref, bias_ref, k_ref, v_ref, q_ref, put_o, put_lse)

    for cp in copies("scatter", b, hf, t, buf):
        cp.start()

    @pl.when(step == n_total - 1)
    def _():
        for cp in copies("scatter", b, hf, t, 1 - buf):
            cp.wait()
        for cp in copies("scatter", b, hf, t, buf):
            cp.wait()


def _attention_group(h, gains, w_in, bias, plan, group, layer, j):
    b, s, d = h.shape
    tokens = plan.tokens_per_step
    specs = (_weight(w_in, j, col0=group * 3 * ATT_D, cols=3 * ATT_D),)
    resident = [
        _layer_resident(gains.shape, layer * N_NORMS),
        HBM,
        pl.BlockSpec((None, ATT_HEADS, ATT_BLK, 2 * ATT_BLK), lambda bi, hf, t: (group, 0, 0, 0),
                     pipeline_mode=pl.Buffered(1)),
    ]
    kv_q = [
        pltpu.VMEM((plan.n_slots, 2 * ATT_BLK, ATT_D), BF16),
        pltpu.VMEM((plan.n_slots, 2 * ATT_BLK, ATT_D), BF16),
        pltpu.VMEM((tokens, ATT_D), BF16),
    ]
    if not plan.view_minor:
        steps = s // plan.block_rows
        block = lambda width: pl.BlockSpec((None, plan.block_rows, width), lambda bi, hf, t: (bi, t, 0))
        o, lse = pl.pallas_call(
            functools.partial(_attn_dense_kernel, plan=plan, specs=specs),
            grid=(b, 1, steps),
            in_specs=[block(d)] + resident,
            out_specs=[block(ATT_D), block(V7X_LANES)],
            out_shape=[jax.ShapeDtypeStruct((b, s, ATT_D), BF16), jax.ShapeDtypeStruct((b, s, V7X_LANES), F32)],
            scratch_shapes=kv_q + _weight_scratch(specs),
            compiler_params=_params("arbitrary", "arbitrary", "arbitrary"),
            name=f"attention_group{group}",
        )(h, gains, w_in, bias)
    else:
        lead = s // plan.view_minor
        steps = lead // plan.block_rows
        assert b * plan.halves * steps >= 2
        view = lambda width: (b, lead, plan.view_minor, width)
        o, lse = pl.pallas_call(
            functools.partial(_attn_strided_kernel, plan=plan, specs=specs, n_batch=b, n_steps=steps),
            grid=(b, plan.halves, steps),
            in_specs=[HBM] + resident,
            out_specs=[HBM, HBM],
            out_shape=[jax.ShapeDtypeStruct(view(ATT_D), F32), jax.ShapeDtypeStruct(view(V7X_LANES), F32)],
            scratch_shapes=kv_q + [
                pltpu.VMEM((2, tokens, d), F32),
                pltpu.VMEM((2, tokens, ATT_D), F32),
                pltpu.VMEM((2, tokens, V7X_LANES), F32),
                pltpu.SemaphoreType.DMA((2,)), pltpu.SemaphoreType.DMA((2,)), pltpu.SemaphoreType.DMA((2,)),
            ] + _weight_scratch(specs),
            compiler_params=_params("arbitrary", "arbitrary", "arbitrary"),
            name=f"attention_group{group}",
        )(h.reshape(view(d)), gains, w_in, bias)
    return o.reshape(b * s, ATT_D), lse.reshape(b * s, V7X_LANES)


ATT_MERGE_TM = 512


def _attn_merge_kernel(h_ref, o0_ref, o1_ref, o2_ref, l0_ref, l1_ref, l2_ref, g1_ref, wout_hbm, expand_ref,
                       out_ref, *wscr, specs):
    (wout_ref,) = _resident_weights(pl.program_id(0) == 0, specs, (wout_hbm,), wscr)
    l0, l1, l2 = l0_ref[...], l1_ref[...], l2_ref[...]
    m = jnp.maximum(jnp.maximum(l0, l1), l2)
    e0, e1, e2 = jnp.exp(l0 - m), jnp.exp(l1 - m), jnp.exp(l2 - m)
    inv = 1.0 / (e0 + e1 + e2)
    expand = expand_ref[...]

    def per_channel(w):
        hi = w.astype(BF16)
        lo = (w - hi.astype(F32)).astype(BF16)
        return _dot(jnp.concatenate([hi, lo], axis=1), expand)

    o = (per_channel(e0 * inv) * o0_ref[...].astype(F32) + per_channel(e1 * inv) * o1_ref[...]
         + per_channel(e2 * inv) * o2_ref[...])
    out_ref[...] = h_ref[...] + _rms(_dot(o.astype(BF16), wout_ref[...]), g1_ref[...])


def _attention_merge(h, outs, lses, gains, w_out, layer, j):
    n, d = h.shape
    row = lambda i: (i, 0)
    tile = pl.BlockSpec((ATT_MERGE_TM, d), row)
    lse_tile = pl.BlockSpec((ATT_MERGE_TM, V7X_LANES), row)
    expand = np.zeros((V7X_LANES, ATT_D), np.float32)
    for hd in range(ATT_HEADS):
        expand[hd, hd * ATT_HEAD_DIM:(hd + 1) * ATT_HEAD_DIM] = 1.0
    expand = jnp.asarray(np.concatenate([expand, expand], axis=0), BF16)
    specs = (_weight(w_out, j),)
    return pl.pallas_call(
        functools.partial(_attn_merge_kernel, specs=specs),
        grid=(n // ATT_MERGE_TM,),
        in_specs=[tile] * 4 + [lse_tile] * 3 + [
            _layer_resident(gains.shape, layer * N_NORMS + 1), HBM, _resident(expand.shape)],
        out_specs=tile,
        out_shape=jax.ShapeDtypeStruct((n, d), F32),
        scratch_shapes=_weight_scratch(specs),
        compiler_params=_params("arbitrary"),
        name="attention_merge",
    )(h, *outs, *lses, gains, w_out, expand)


def kernel(x, p, norm_g, ret_w_in, ret_w_out, attn_w_in, attn_w_out, rel_bias, gm_w_in, gm_ln_g, gm_ln_b,
           gm_w_s, gm_b_s, gm_w_out, ffn_w_in, ffn_w_out, ple_w_proj, ple_w_gate):
    b, s, d = x.shape
    n = b * s
    gains = norm_g.reshape(DEPTH * N_NORMS, 1, d)
    p = p.reshape(DEPTH, n, PLE_DIM)
    gm_ln_g = gm_ln_g.reshape(-1, 1, GM_WIDTH)
    gm_ln_b = gm_ln_b.reshape(-1, 1, GM_WIDTH)
    gm_b_s = gm_b_s.reshape(-1, GM_GROUPS, GM_CHUNK, 1)
    ret_consts = _retention_consts(s)
    plans = tuple(_group_plan(dil) for _, dil in DILATION_PAIRS)

    h = x
    for i in range(DEPTH):
        kind, j = i % N_MIXERS, i // N_MIXERS
        if kind == 0:
            h = _retention_mixer(h.reshape(b, s, d), gains, ret_w_in, ret_w_out, ret_consts, i, j)
        elif kind == 1:
            bias = _attention_bias(rel_bias, plans)
            outs, lses = [], []
            for gi, plan in enumerate(plans):
                o, lse = _attention_group(h.reshape(b, s, d), gains, attn_w_in, bias, plan, gi, i, j)
                outs.append(o)
                lses.append(lse)
            h = _attention_merge(h.reshape(n, d), outs, lses, gains, attn_w_out, i, j)
        else:
            h = _gmlp_mixer(h.reshape(n, d), gains, gm_w_in, gm_ln_g, gm_ln_b, gm_w_s, gm_b_s, gm_w_out, i, j)
        h = _ffn_ple(h.reshape(n, d), p, gains, ffn_w_in, ffn_w_out, ple_w_proj, ple_w_gate, i)
    return h.reshape(b, s, d)
```

```python
import functools
import math
from typing import NamedTuple

import jax
import jax.numpy as jnp
import numpy as np
from jax import lax
from jax.experimental import pallas as pl
from jax.experimental.pallas import tpu as pltpu

D_MODEL = 1024
DEPTH = 4
N_MIXERS = 3
N_NORMS = 5
PLE_DIM = 256
EPS = 1e-6

RET_HEADS = 4
RET_QK_DIM = 256
RET_V_DIM = 512
RET_CHUNK = 256
ROPE_BASE = 10000.0

ATT_HEADS = 16
ATT_HEAD_DIM = 64
DILATION_PAIRS = ((128, 1), (512, 4), (2048, 16))
ATT_BLK = 128
REL_BUCKETS = 32
REL_MAX_EXACT = 16
REL_MAX_DIST = 2048

GM_CHUNK = 128
GM_WIDTH = 2 * D_MODEL
GM_GROUPS = 8
GM_GROUP_DIM = GM_WIDTH // GM_GROUPS

FFN_HIDDEN = 2816

V7X_LANES = 128
V7X_SUBLANES = 8
V7X_VMEM_BYTES = 64 * 1024 * 1024
VMEM_LIMIT_BYTES = V7X_VMEM_BYTES - 8 * 1024 * 1024

MASK_VALUE = -1e30
LOG2E = math.log2(math.e)
LN2 = math.log(2.0)

BF16 = jnp.bfloat16
F32 = jnp.float32


def _resident(shape):
    return pl.BlockSpec(shape, lambda *_: (0,) * len(shape), pipeline_mode=pl.Buffered(1))


def _layer_resident(shape, layer):
    rest = tuple(shape[1:])
    return pl.BlockSpec((None,) + rest, lambda *_: (layer,) + (0,) * len(rest), pipeline_mode=pl.Buffered(1))


def _rms(x, g):
    return x * lax.rsqrt(jnp.mean(x * x, axis=-1, keepdims=True) + EPS) * g


def _dot(a, b):
    return jnp.dot(a, b, preferred_element_type=F32)


def _dot_nt(a, b):
    return lax.dot_general(a, b, (((1,), (1,)), ((), ())), preferred_element_type=F32)


def _dot_tn(a, b):
    return lax.dot_general(a, b, (((0,), (0,)), ((), ())), preferred_element_type=F32)


def _params(*semantics):
    return pltpu.CompilerParams(dimension_semantics=semantics, vmem_limit_bytes=VMEM_LIMIT_BYTES)


WEIGHT_CHUNK = (256, 512)
WEIGHT_SLOTS = 12

HBM = pl.BlockSpec(memory_space=pl.ANY)


class _WeightSpec(NamedTuple):
    layer: int
    rows: int
    cols: int
    col0: int


def _weight(array, layer, col0=0, cols=None):
    spec = _WeightSpec(layer, array.shape[1], array.shape[2] if cols is None else cols, col0)
    assert spec.rows % WEIGHT_CHUNK[0] == 0 and spec.cols % WEIGHT_CHUNK[1] == 0
    return spec


def _weight_scratch(specs):
    return ([pltpu.VMEM((spec.rows, spec.cols), BF16) for spec in specs]
            + [pltpu.VMEM((WEIGHT_SLOTS,) + WEIGHT_CHUNK, F32), pltpu.SemaphoreType.DMA((WEIGHT_SLOTS,))])


def _resident_weights(first, specs, hbm_refs, scratch):
    *w_refs, stage, sem = scratch
    cr, cc = WEIGHT_CHUNK
    chunks = [(w_hbm, w_ref, spec, r, c)
              for spec, w_hbm, w_ref in zip(specs, hbm_refs, w_refs)
              for r in range(0, spec.rows, cr) for c in range(0, spec.cols, cc)]

    def copy(k):
        w_hbm, _, spec, r, c = chunks[k]
        src = w_hbm.at[spec.layer, pl.ds(r, cr), pl.ds(spec.col0 + c, cc)]
        return pltpu.make_async_copy(src, stage.at[k % WEIGHT_SLOTS], sem.at[k % WEIGHT_SLOTS])

    @pl.when(first)
    def _():
        for k in range(min(WEIGHT_SLOTS, len(chunks))):
            copy(k).start()
        for k, (_, w_ref, _, r, c) in enumerate(chunks):
            copy(k).wait()
            w_ref[r:r + cr, c:c + cc] = stage[k % WEIGHT_SLOTS].astype(BF16)
            if k + WEIGHT_SLOTS < len(chunks):
                copy(k + WEIGHT_SLOTS).start()

    return w_refs


FFN_TM = 1024
FFN_SUB = 512


def _ffn_rows(x, p_rows, g2_ref, g3_ref, g4_ref, win_ref, wout_ref, wp_ref, wg_ref):
    a = _rms(x, g2_ref[...]).astype(BF16)
    gate = _dot(a, win_ref[:, :FFN_HIDDEN])
    up = _dot(a, win_ref[:, FFN_HIDDEN:])
    hid = (gate * jax.nn.sigmoid(gate) * up).astype(BF16)
    h1 = x + _rms(_dot(hid, wout_ref[...]), g3_ref[...])
    emb = _dot(p_rows.astype(BF16), wp_ref[...])
    gate = jax.nn.sigmoid(_dot(_rms(h1, g4_ref[...]).astype(BF16), wg_ref[...]))
    return h1 + gate * emb


def _ffn_ple_kernel(h_ref, p_ref, g2_ref, g3_ref, g4_ref, win_hbm, wout_hbm, wp_hbm, wg_hbm, o_ref, *wscr, specs):
    weights = _resident_weights(pl.program_id(0) == 0, specs, (win_hbm, wout_hbm, wp_hbm, wg_hbm), wscr)
    for part in range(FFN_TM // FFN_SUB):
        rows = slice(part * FFN_SUB, (part + 1) * FFN_SUB)
        o_ref[rows, :] = _ffn_rows(h_ref[rows, :], p_ref[rows, :], g2_ref, g3_ref, g4_ref, *weights)


def _ffn_ple(h, p, gains, w_in, w_out, w_proj, w_gate, layer):
    n, d = h.shape
    row = lambda i: (i, 0)
    gain = lambda k: _layer_resident(gains.shape, layer * N_NORMS + k)
    specs = tuple(_weight(w, layer) for w in (w_in, w_out, w_proj, w_gate))
    return pl.pallas_call(
        functools.partial(_ffn_ple_kernel, specs=specs),
        grid=(n // FFN_TM,),
        in_specs=[
            pl.BlockSpec((FFN_TM, d), row),
            pl.BlockSpec((None, FFN_TM, PLE_DIM), lambda i: (layer, i, 0)),
            gain(2), gain(3), gain(4),
            HBM, HBM, HBM, HBM,
        ],
        out_specs=pl.BlockSpec((FFN_TM, d), row),
        out_shape=jax.ShapeDtypeStruct((n, d), F32),
        scratch_shapes=_weight_scratch(specs),
        compiler_params=_params("arbitrary"),
        name="ffn_ple",
    )(h, p, gains, gains, gains, w_in, w_out, w_proj, w_gate)


RET_TM = 512
RET_SUB = 512
RET_Q_OFF = 0
RET_K_OFF = RET_HEADS * RET_QK_DIM
RET_V_OFF = 2 * RET_HEADS * RET_QK_DIM
RET_G_OFF = RET_V_OFF + RET_HEADS * RET_V_DIM


def _rope(t, cos, sin):
    half = RET_QK_DIM // 2
    t1, t2 = t[:, :half], t[:, half:]
    return jnp.concatenate([t1 * cos - t2 * sin, t2 * cos + t1 * sin], axis=-1)


def _retention_kernel(h_ref, cos_ref, sin_ref, g0_ref, g1_ref, win_hbm, wout_hbm,
                      dmask_ref, qdec_ref, kdec_ref, cdec_ref, o_ref, state_ref, y_ref, *wscr, specs):
    win_ref, wout_ref = _resident_weights(
        (pl.program_id(0) == 0) & (pl.program_id(1) == 0), specs, (win_hbm, wout_hbm), wscr)

    @pl.when(pl.program_id(1) == 0)
    def _():
        state_ref[...] = jnp.zeros(state_ref.shape, F32)

    for part in range(RET_TM // RET_SUB):
        prow = slice(part * RET_SUB, (part + 1) * RET_SUB)
        x = h_ref[prow, :]
        a = _rms(x, g0_ref[...]).astype(BF16)
        cos, sin = cos_ref[prow, :], sin_ref[prow, :]
        for hd in range(RET_HEADS):
            qo, ko, vo = RET_Q_OFF + hd * RET_QK_DIM, RET_K_OFF + hd * RET_QK_DIM, RET_V_OFF + hd * RET_V_DIM
            q = _rope(_dot(a, win_ref[:, qo:qo + RET_QK_DIM]), cos, sin).astype(BF16)
            k = _rope(_dot(a, win_ref[:, ko:ko + RET_QK_DIM]), cos, sin) * (RET_QK_DIM ** -0.5)
            v = _dot(a, win_ref[:, vo:vo + RET_V_DIM]).astype(BF16)
            for c in range(RET_SUB // RET_CHUNK):
                rows = slice(c * RET_CHUNK, (c + 1) * RET_CHUNK)
                qc, kc, vc = q[rows], k[rows], v[rows]
                st = state_ref[hd]
                scores = _dot_nt(qc, kc.astype(BF16)) * dmask_ref[hd]
                inner = _dot(scores.astype(BF16), vc)
                cross = _dot(qc, st.astype(BF16)) * qdec_ref[hd]
                state_ref[hd] = st * cdec_ref[hd] + _dot_tn((kc * kdec_ref[hd]).astype(BF16), vc)
                y = inner + cross
                y_ref[rows, hd * RET_V_DIM:(hd + 1) * RET_V_DIM] = y * lax.rsqrt(
                    jnp.mean(y * y, axis=-1, keepdims=True) + EPS)
        gate = _dot(a, win_ref[:, RET_G_OFF:RET_G_OFF + RET_HEADS * RET_V_DIM])
        z = (gate * jax.nn.sigmoid(gate) * y_ref[...]).astype(BF16)
        o_ref[prow, :] = x + _rms(_dot(z, wout_ref[...]), g1_ref[...])


def _retention_consts(seq_len):
    hh, c, dk = RET_HEADS, RET_CHUNK, RET_QK_DIM
    inv_freq = np.float32(ROPE_BASE) ** (-np.arange(0, dk, 2, dtype=np.float32) / np.float32(dk))
    ang = np.arange(seq_len, dtype=np.float32)[:, None] * inv_freq[None, :]
    cos = np.cos(ang.astype(np.float64)).astype(np.float32)
    sin = np.sin(ang.astype(np.float64)).astype(np.float32)
    log_gamma = jnp.log1p(-jnp.exp2(-5.0 - jnp.arange(hh, dtype=F32)))
    idx = jnp.arange(c, dtype=F32)
    diff = idx[:, None] - idx[None, :]
    dmask = jnp.where(diff >= 0, jnp.exp(log_gamma[:, None, None] * jnp.maximum(diff, 0.0)), 0.0)
    qdec = jnp.exp(log_gamma[:, None] * (idx[None, :] + 1.0))[:, :, None]
    kdec = jnp.exp(log_gamma[:, None] * (c - 1.0 - idx[None, :]))[:, :, None]
    cdec = jnp.exp(log_gamma * c)[:, None, None]
    return jnp.asarray(cos), jnp.asarray(sin), dmask, qdec, kdec, cdec


def _retention_mixer(h, gains, w_in, w_out, consts, layer, j):
    b, s, d = h.shape
    cos, sin, dmask, qdec, kdec, cdec = consts
    tile = lambda bi, t: (bi, t, 0)
    pos = lambda bi, t: (t, 0)
    gain = lambda k: _layer_resident(gains.shape, layer * N_NORMS + k)
    specs = (_weight(w_in, j), _weight(w_out, j))
    return pl.pallas_call(
        functools.partial(_retention_kernel, specs=specs),
        grid=(b, s // RET_TM),
        in_specs=[
            pl.BlockSpec((None, RET_TM, d), tile),
            pl.BlockSpec((RET_TM, RET_QK_DIM // 2), pos),
            pl.BlockSpec((RET_TM, RET_QK_DIM // 2), pos),
            gain(0), gain(1),
            HBM, HBM,
            _resident(dmask.shape), _resident(qdec.shape), _resident(kdec.shape), _resident(cdec.shape),
        ],
        out_specs=pl.BlockSpec((None, RET_TM, d), tile),
        out_shape=jax.ShapeDtypeStruct((b, s, d), F32),
        scratch_shapes=[
            pltpu.VMEM((RET_HEADS, RET_QK_DIM, RET_V_DIM), F32),
            pltpu.VMEM((RET_SUB, RET_HEADS * RET_V_DIM), F32),
        ] + _weight_scratch(specs),
        compiler_params=_params("arbitrary", "arbitrary"),
        name="retention_mixer",
    )(h, cos, sin, gains, gains, w_in, w_out, dmask, qdec, kdec, cdec)


GM_TM = 1024
GM_SUB = 512


def _gmlp_kernel(h_ref, g0_ref, g1_ref, win_hbm, lng_ref, lnb_ref, ws_ref, bs_ref, wout_hbm, o_ref, s_ref, *wscr,
                 specs):
    win_ref, wout_ref = _resident_weights(pl.program_id(0) == 0, specs, (win_hbm, wout_hbm), wscr)

    def gelu(t):
        return 0.5 * t * (1.0 + lax.erf(t * (2.0 ** -0.5)))

    row = lax.broadcasted_iota(jnp.int32, (GM_CHUNK, GM_CHUNK), 0)
    col = lax.broadcasted_iota(jnp.int32, (GM_CHUNK, GM_CHUNK), 1)
    causal = row >= col
    for part in range(GM_TM // GM_SUB):
        prow = slice(part * GM_SUB, (part + 1) * GM_SUB)
        x = h_ref[prow, :]
        a = _rms(x, g0_ref[...]).astype(BF16)
        v = gelu(_dot(a, win_ref[:, GM_WIDTH:]))
        u = gelu(_dot(a, win_ref[:, :GM_WIDTH]))
        mu = jnp.mean(v, axis=-1, keepdims=True)
        vc = v - mu
        var = jnp.mean(vc * vc, axis=-1, keepdims=True)
        vn = (vc * lax.rsqrt(var + EPS) * lng_ref[...] + lnb_ref[...]).astype(BF16)
        for g in range(GM_GROUPS):
            w = jnp.where(causal, ws_ref[g], 0.0).astype(BF16)
            cols = slice(g * GM_GROUP_DIM, (g + 1) * GM_GROUP_DIM)
            for c in range(GM_SUB // GM_CHUNK):
                rows = slice(c * GM_CHUNK, (c + 1) * GM_CHUNK)
                s_ref[rows, cols] = _dot(w, vn[rows, cols]) + bs_ref[g]
        z = (u * s_ref[...]).astype(BF16)
        o_ref[prow, :] = x + _rms(_dot(z, wout_ref[...]), g1_ref[...])


def _gmlp_mixer(h, gains, w_in, ln_g, ln_b, w_s, b_s, w_out, layer, j):
    n, d = h.shape
    row = lambda i: (i, 0)
    gain = lambda k: _layer_resident(gains.shape, layer * N_NORMS + k)
    specs = (_weight(w_in, j), _weight(w_out, j))
    return pl.pallas_call(
        functools.partial(_gmlp_kernel, specs=specs),
        grid=(n // GM_TM,),
        in_specs=[
            pl.BlockSpec((GM_TM, d), row),
            gain(0), gain(1),
            HBM, _layer_resident(ln_g.shape, j), _layer_resident(ln_b.shape, j),
            _layer_resident(w_s.shape, j), _layer_resident(b_s.shape, j), HBM,
        ],
        out_specs=pl.BlockSpec((GM_TM, d), row),
        out_shape=jax.ShapeDtypeStruct((n, d), F32),
        scratch_shapes=[pltpu.VMEM((GM_SUB, GM_WIDTH), F32)] + _weight_scratch(specs),
        compiler_params=_params("arbitrary"),
        name="gmlp_mixer",
    )(h, gains, gains, w_in, ln_g, ln_b, w_s, b_s, w_out)


ATT_D = ATT_HEADS * ATT_HEAD_DIM


class _GroupPlan(NamedTuple):
    dilation: int
    view_minor: int
    block_rows: int
    halves: int
    n_units: int
    n_slots: int
    pieces: tuple
    first_units: tuple
    positions: np.ndarray

    @property
    def tokens_per_step(self):
        return self.n_units * ATT_BLK


def _group_plan(dilation):
    if dilation == 1:
        return _GroupPlan(1, 0, 8 * ATT_BLK, 1, 8, 1,
                          tuple(((None, u * ATT_BLK, ATT_BLK),) for u in range(8)), (0,),
                          np.arange(ATT_BLK))
    if dilation == 4:
        half = ATT_BLK // 2
        return _GroupPlan(4, 8, ATT_BLK, 1, 8, 4,
                          tuple(((c, blk * half, half), (c + 4, blk * half, half))
                                for blk in range(2) for c in range(4)),
                          (0, 1, 2, 3),
                          np.concatenate([2 * np.arange(half), 2 * np.arange(half) + 1]))
    assert dilation == 16
    return _GroupPlan(16, 16, ATT_BLK, 2, 8, 8,
                      tuple(((c, 0, ATT_BLK),) for c in range(8)), tuple(range(8)),
                      np.arange(ATT_BLK))


def _t5_bucket(dist):
    d = jnp.maximum(dist, 1).astype(F32)
    large = REL_MAX_EXACT + (jnp.log(d / REL_MAX_EXACT) / math.log(REL_MAX_DIST / REL_MAX_EXACT)
                             * (REL_BUCKETS - REL_MAX_EXACT)).astype(jnp.int32)
    large = jnp.minimum(large, REL_BUCKETS - 1)
    return jnp.where(dist < REL_MAX_EXACT, dist, large)


def _bias_kernel(table_ref, bucket_ref, dist_ref, o_ref):
    g = pl.program_id(0)
    bucket = bucket_ref[...]
    dist = dist_ref[...]
    band = (dist >= 0) & (dist <= ATT_BLK)
    for hd in range(ATT_HEADS):
        acc = jnp.zeros(bucket.shape, F32)
        for b in range(REL_BUCKETS):
            acc = jnp.where(bucket == b, table_ref[b, g * ATT_HEADS + hd], acc)
        o_ref[hd] = jnp.where(band, acc * LOG2E, MASK_VALUE)


def _attention_bias(rel_bias, plans):
    dists = []
    for plan in plans:
        pos_k = np.concatenate([plan.positions - ATT_BLK, plan.positions])
        dists.append(plan.positions[:, None] - pos_k[None, :])
    dist = jnp.asarray(np.stack(dists), jnp.int32)
    scale = jnp.asarray([plan.dilation for plan in plans], jnp.int32)[:, None, None]
    buckets = _t5_bucket(jnp.maximum(dist, 0) * scale).astype(jnp.int32)
    tile = pl.BlockSpec((None, ATT_BLK, 2 * ATT_BLK), lambda g: (g, 0, 0))
    return pl.pallas_call(
        _bias_kernel,
        grid=(len(plans),),
        in_specs=[pl.BlockSpec(memory_space=pltpu.SMEM), tile, tile],
        out_specs=pl.BlockSpec((None, ATT_HEADS, ATT_BLK, 2 * ATT_BLK), lambda g: (g, 0, 0, 0)),
        out_shape=jax.ShapeDtypeStruct((len(plans), ATT_HEADS, ATT_BLK, 2 * ATT_BLK), F32),
        compiler_params=_params("arbitrary"),
        name="attention_bias",
    )(rel_bias, buckets, dist)


def _attn_units(plan, t, xa, g0_ref, w_ref, bias_ref, k_ref, v_ref, q_ref, put_o, put_lse):
    @pl.when(t == 0)
    def _():
        k_ref[:, :ATT_BLK, :] = jnp.zeros((plan.n_slots, ATT_BLK, ATT_D), BF16)
        v_ref[:, :ATT_BLK, :] = jnp.zeros((plan.n_slots, ATT_BLK, ATT_D), BF16)

    a = _rms(xa, g0_ref[...]).astype(BF16)
    q_ref[...] = (_dot(a, w_ref[:, :ATT_D]) * (ATT_HEAD_DIM ** -0.5 * LOG2E)).astype(BF16)
    kc = _dot(a, w_ref[:, ATT_D:2 * ATT_D]).astype(BF16)
    vc = _dot(a, w_ref[:, 2 * ATT_D:]).astype(BF16)

    lane = lax.broadcasted_iota(jnp.int32, (ATT_BLK, V7X_LANES), 1)
    low = lane < ATT_HEAD_DIM
    kcol = lax.broadcasted_iota(jnp.int32, (ATT_BLK, 2 * ATT_BLK), 1)
    first_pen = jnp.where((kcol < ATT_BLK) & (t == 0), MASK_VALUE, 0.0)
    for u in range(plan.n_units):
        slot = u % plan.n_slots
        rows = slice(u * ATT_BLK, (u + 1) * ATT_BLK)
        k_ref[slot, ATT_BLK:, :] = kc[rows]
        v_ref[slot, ATT_BLK:, :] = vc[rows]
        m_tile = jnp.zeros((ATT_BLK, V7X_LANES), F32)
        den_tile = jnp.ones((ATT_BLK, V7X_LANES), F32)
        for hp in range(ATT_HEADS // 2):
            cols = slice(hp * V7X_LANES, (hp + 1) * V7X_LANES)
            qp = q_ref[rows, cols]
            kk = k_ref[slot, :, cols]
            vv = v_ref[slot, :, cols]
            outs, lses = [], []
            for sub in range(2):
                qm = jnp.where(low if sub == 0 else ~low, qp, jnp.zeros_like(qp))
                s = _dot_nt(qm, kk) + bias_ref[2 * hp + sub]
                if u in plan.first_units:
                    s = s + first_pen
                m = jnp.max(s, axis=-1, keepdims=True)
                e = jnp.exp2(s - m)
                den = jnp.sum(e, axis=-1, keepdims=True)
                outs.append(_dot(e.astype(BF16), vv) / den)
                lses.append((m, den))
            put_o(rows, cols, jnp.where(low, outs[0], outs[1]))
            m_tile = jnp.where(lane == 2 * hp, lses[0][0], jnp.where(lane == 2 * hp + 1, lses[1][0], m_tile))
            den_tile = jnp.where(lane == 2 * hp, lses[0][1], jnp.where(lane == 2 * hp + 1, lses[1][1], den_tile))
        put_lse(rows, m_tile * LN2 + jnp.log(den_tile))
        k_ref[slot, :ATT_BLK, :] = k_ref[slot, ATT_BLK:, :]
        v_ref[slot, :ATT_BLK, :] = v_ref[slot, ATT_BLK:, :]


def _attn_dense_kernel(x_ref, g0_ref, w_hbm, bias_ref, o_ref, lse_ref, k_ref, v_ref, q_ref, *wscr, plan, specs):
    (w_ref,) = _resident_weights((pl.program_id(0) == 0) & (pl.program_id(2) == 0), specs, (w_hbm,), wscr)

    def put_o(rows, cols, tile):
        o_ref[rows, cols] = tile.astype(o_ref.dtype)

    def put_lse(rows, tile):
        lse_ref[rows, :] = tile

    _attn_units(plan, pl.program_id(2), x_ref[...], g0_ref, w_ref, bias_ref, k_ref, v_ref, q_ref, put_o, put_lse)


def _attn_strided_kernel(x_hbm, g0_ref, w_hbm, bias_ref, o_hbm, lse_hbm, k_ref, v_ref, q_ref,
                         xbuf, obuf, lbuf, xsem, osem, lsem, *wscr, plan, specs, n_batch, n_steps):
    b, hf, t = pl.program_id(0), pl.program_id(1), pl.program_id(2)
    step = (b * plan.halves + hf) * n_steps + t
    n_total = n_batch * plan.halves * n_steps
    buf = lax.rem(step, 2)

    def class_rows(ref, bb, hh, tt, sub, r0, n):
        return ref.at[bb, pl.ds(tt * plan.block_rows + r0, n), hh * V7X_SUBLANES + sub]

    def copies(kind, bb, hh, tt, which):
        out = []
        for u, pcs in enumerate(plan.pieces):
            at = u * ATT_BLK
            for sub, r0, n in pcs:
                dense = pl.ds(at, n)
                if kind == "gather":
                    out.append(pltpu.make_async_copy(class_rows(x_hbm, bb, hh, tt, sub, r0, n),
                                                     xbuf.at[which, dense], xsem.at[which]))
                else:
                    out.append(pltpu.make_async_copy(obuf.at[which, dense],
                                                     class_rows(o_hbm, bb, hh, tt, sub, r0, n), osem.at[which]))
                    out.append(pltpu.make_async_copy(lbuf.at[which, dense],
                                                     class_rows(lse_hbm, bb, hh, tt, sub, r0, n), lsem.at[which]))
                at += n
        return out

    @pl.when(step == 0)
    def _():
        for cp in copies("gather", b, hf, t, buf):
            cp.start()

    wrap_t = t + 1 == n_steps
    wrap_h = hf + 1 == plan.halves
    nt = jnp.where(wrap_t, 0, t + 1)
    nh = jnp.where(wrap_t, jnp.where(wrap_h, 0, hf + 1), hf)
    nb = jnp.where(wrap_t & wrap_h, b + 1, b)

    @pl.when(step + 1 < n_total)
    def _():
        for cp in copies("gather", nb, nh, nt, 1 - buf):
            cp.start()

    (w_ref,) = _resident_weights(step == 0, specs, (w_hbm,), wscr)

    for cp in copies("gather", b, hf, t, buf):
        cp.wait()

    @pl.when(step >= 2)
    def _():
        for cp in copies("scatter", b, hf, t, buf):
            cp.wait()

    def put_o(rows, cols, tile):
        obuf[buf, rows, cols] = tile

    def put_lse(rows, tile):
        lbuf[buf, rows, :] = tile

    _attn_units(plan, t, xbuf[buf], g0_ref, w_ref, bias_ref, k_ref, v_ref, q_ref, put_o, put_lse)

    for cp in copies("scatter", b, hf, t, buf):
        cp.start()

    @pl.when(step == n_total - 1)
    def _():
        for cp in copies("scatter", b, hf, t, 1 - buf):
            cp.wait()
        for cp in copies("scatter", b, hf, t, buf):
            cp.wait()


def _attention_group(h, gains, w_in, bias, plan, group, layer, j):
    b, s, d = h.shape
    tokens = plan.tokens_per_step
    specs = (_weight(w_in, j, col0=group * 3 * ATT_D, cols=3 * ATT_D),)
    resident = [
        _layer_resident(gains.shape, layer * N_NORMS),
        HBM,
        pl.BlockSpec((None, ATT_HEADS, ATT_BLK, 2 * ATT_BLK), lambda bi, hf, t: (group, 0, 0, 0),
                     pipeline_mode=pl.Buffered(1)),
    ]
    kv_q = [
        pltpu.VMEM((plan.n_slots, 2 * ATT_BLK, ATT_D), BF16),
        pltpu.VMEM((plan.n_slots, 2 * ATT_BLK, ATT_D), BF16),
        pltpu.VMEM((tokens, ATT_D), BF16),
    ]
    if not plan.view_minor:
        steps = s // plan.block_rows
        block = lambda width: pl.BlockSpec((None, plan.block_rows, width), lambda bi, hf, t: (bi, t, 0))
        o, lse = pl.pallas_call(
            functools.partial(_attn_dense_kernel, plan=plan, specs=specs),
            grid=(b, 1, steps),
            in_specs=[block(d)] + resident,
            out_specs=[block(ATT_D), block(V7X_LANES)],
            out_shape=[jax.ShapeDtypeStruct((b, s, ATT_D), BF16), jax.ShapeDtypeStruct((b, s, V7X_LANES), F32)],
            scratch_shapes=kv_q + _weight_scratch(specs),
            compiler_params=_params("arbitrary", "arbitrary", "arbitrary"),
            name=f"attention_group{group}",
        )(h, gains, w_in, bias)
    else:
        lead = s // plan.view_minor
        steps = lead // plan.block_rows
        assert b * plan.halves * steps >= 2
        view = lambda width: (b, lead, plan.view_minor, width)
        o, lse = pl.pallas_call(
            functools.partial(_attn_strided_kernel, plan=plan, specs=specs, n_batch=b, n_steps=steps),
            grid=(b, plan.halves, steps),
            in_specs=[HBM] + resident,
            out_specs=[HBM, HBM],
            out_shape=[jax.ShapeDtypeStruct(view(ATT_D), F32), jax.ShapeDtypeStruct(view(V7X_LANES), F32)],
            scratch_shapes=kv_q + [
                pltpu.VMEM((2, tokens, d), F32),
                pltpu.VMEM((2, tokens, ATT_D), F32),
                pltpu.VMEM((2, tokens, V7X_LANES), F32),
                pltpu.SemaphoreType.DMA((2,)), pltpu.SemaphoreType.DMA((2,)), pltpu.SemaphoreType.DMA((2,)),
            ] + _weight_scratch(specs),
            compiler_params=_params("arbitrary", "arbitrary", "arbitrary"),
            name=f"attention_group{group}",
        )(h.reshape(view(d)), gains, w_in, bias)
    return o.reshape(b * s, ATT_D), lse.reshape(b * s, V7X_LANES)


ATT_MERGE_TM = 512
ATT_MERGE_VMEM_LIMIT_BYTES = V7X_VMEM_BYTES - 4 * 1024 * 1024


def _attn_merge_ffn_kernel(h_ref, o0_ref, o1_ref, o2_ref, l0_ref, l1_ref, l2_ref, p_ref, g1_ref, g2_ref, g3_ref,
                           g4_ref, expand_ref, wao_hbm, win_hbm, wout_hbm, wp_hbm, wg_hbm, out_ref, *wscr, specs):
    wout_ref, *ffn_weights = _resident_weights(
        pl.program_id(0) == 0, specs, (wao_hbm, win_hbm, wout_hbm, wp_hbm, wg_hbm), wscr)
    l0, l1, l2 = l0_ref[...], l1_ref[...], l2_ref[...]
    m = jnp.maximum(jnp.maximum(l0, l1), l2)
    e0, e1, e2 = jnp.exp(l0 - m), jnp.exp(l1 - m), jnp.exp(l2 - m)
    inv = 1.0 / (e0 + e1 + e2)
    expand = expand_ref[...]

    def per_channel(w):
        hi = w.astype(BF16)
        lo = (w - hi.astype(F32)).astype(BF16)
        return _dot(jnp.concatenate([hi, lo], axis=1), expand)

    o = (per_channel(e0 * inv) * o0_ref[...].astype(F32) + per_channel(e1 * inv) * o1_ref[...]
         + per_channel(e2 * inv) * o2_ref[...])
    x = h_ref[...] + _rms(_dot(o.astype(BF16), wout_ref[...]), g1_ref[...])
    out_ref[...] = _ffn_rows(x, p_ref[...], g2_ref, g3_ref, g4_ref, *ffn_weights)


def _attention_merge_ffn(h, outs, lses, p, gains, attn_w_out, w_in, w_out, w_proj, w_gate, layer, j):
    n, d = h.shape
    row = lambda i: (i, 0)
    tile = pl.BlockSpec((ATT_MERGE_TM, d), row)
    lse_tile = pl.BlockSpec((ATT_MERGE_TM, V7X_LANES), row)
    expand = np.zeros((V7X_LANES, ATT_D), np.float32)
    for hd in range(ATT_HEADS):
        expand[hd, hd * ATT_HEAD_DIM:(hd + 1) * ATT_HEAD_DIM] = 1.0
    expand = jnp.asarray(np.concatenate([expand, expand], axis=0), BF16)
    gain = lambda k: _layer_resident(gains.shape, layer * N_NORMS + k)
    specs = (_weight(attn_w_out, j),) + tuple(_weight(w, layer) for w in (w_in, w_out, w_proj, w_gate))
    return pl.pallas_call(
        functools.partial(_attn_merge_ffn_kernel, specs=specs),
        grid=(n // ATT_MERGE_TM,),
        in_specs=[tile] * 4 + [lse_tile] * 3 + [
            pl.BlockSpec((None, ATT_MERGE_TM, PLE_DIM), lambda i: (layer, i, 0)),
            gain(1), gain(2), gain(3), gain(4), _resident(expand.shape), HBM, HBM, HBM, HBM, HBM],
        out_specs=tile,
        out_shape=jax.ShapeDtypeStruct((n, d), F32),
        scratch_shapes=_weight_scratch(specs),
        compiler_params=pltpu.CompilerParams(dimension_semantics=("arbitrary",),
                                             vmem_limit_bytes=ATT_MERGE_VMEM_LIMIT_BYTES),
        name="attention_merge_ffn",
    )(h, *outs, *lses, p, gains, gains, gains, gains, expand, attn_w_out, w_in, w_out, w_proj, w_gate)


def kernel(x, p, norm_g, ret_w_in, ret_w_out, attn_w_in, attn_w_out, rel_bias, gm_w_in, gm_ln_g, gm_ln_b,
           gm_w_s, gm_b_s, gm_w_out, ffn_w_in, ffn_w_out, ple_w_proj, ple_w_gate):
    b, s, d = x.shape
    n = b * s
    gains = norm_g.reshape(DEPTH * N_NORMS, 1, d)
    p = p.reshape(DEPTH, n, PLE_DIM)
    gm_ln_g = gm_ln_g.reshape(-1, 1, GM_WIDTH)
    gm_ln_b = gm_ln_b.reshape(-1, 1, GM_WIDTH)
    gm_b_s = gm_b_s.reshape(-1, GM_GROUPS, GM_CHUNK, 1)
    ret_consts = _retention_consts(s)
    plans = tuple(_group_plan(dil) for _, dil in DILATION_PAIRS)

    h = x
    for i in range(DEPTH):
        kind, j = i % N_MIXERS, i // N_MIXERS
        if kind == 0:
            h = _retention_mixer(h.reshape(b, s, d), gains, ret_w_in, ret_w_out, ret_consts, i, j)
        elif kind == 1:
            bias = _attention_bias(rel_bias, plans)
            outs, lses = [], []
            for gi, plan in enumerate(plans):
                o, lse = _attention_group(h.reshape(b, s, d), gains, attn_w_in, bias, plan, gi, i, j)
                outs.append(o)
                lses.append(lse)
            h = _attention_merge_ffn(h.reshape(n, d), outs, lses, p, gains, attn_w_out,
                                     ffn_w_in, ffn_w_out, ple_w_proj, ple_w_gate, i, j)
            continue
        else:
            h = _gmlp_mixer(h.reshape(n, d), gains, gm_w_in, gm_ln_g, gm_ln_b, gm_w_s, gm_b_s, gm_w_out, i, j)
        h = _ffn_ple(h.reshape(n, d), p, gains, ffn_w_in, ffn_w_out, ple_w_proj, ple_w_gate, i)
    return h.reshape(b, s, d)
```

```python
import functools
import math
from typing import NamedTuple

import jax
import jax.numpy as jnp
import numpy as np
from jax import lax
from jax.experimental import pallas as pl
from jax.experimental.pallas import tpu as pltpu

D_MODEL = 1024
DEPTH = 4
N_MIXERS = 3
N_NORMS = 5
PLE_DIM = 256
EPS = 1e-6

RET_HEADS = 4
RET_QK_DIM = 256
RET_V_DIM = 512
RET_CHUNK = 256
ROPE_BASE = 10000.0

ATT_HEADS = 16
ATT_HEAD_DIM = 64
DILATION_PAIRS = ((128, 1), (512, 4), (2048, 16))
ATT_BLK = 128
REL_BUCKETS = 32
REL_MAX_EXACT = 16
REL_MAX_DIST = 2048

GM_CHUNK = 128
GM_WIDTH = 2 * D_MODEL
GM_GROUPS = 8
GM_GROUP_DIM = GM_WIDTH // GM_GROUPS

FFN_HIDDEN = 2816

V7X_LANES = 128
V7X_SUBLANES = 8
V7X_VMEM_BYTES = 64 * 1024 * 1024
VMEM_LIMIT_BYTES = V7X_VMEM_BYTES - 8 * 1024 * 1024

MASK_VALUE = -1e30
LOG2E = math.log2(math.e)
LN2 = math.log(2.0)

BF16 = jnp.bfloat16
F32 = jnp.float32


def _resident(shape):
    return pl.BlockSpec(shape, lambda *_: (0,) * len(shape), pipeline_mode=pl.Buffered(1))


def _layer_resident(shape, layer):
    rest = tuple(shape[1:])
    return pl.BlockSpec((None,) + rest, lambda *_: (layer,) + (0,) * len(rest), pipeline_mode=pl.Buffered(1))


def _rms(x, g):
    return x * lax.rsqrt(jnp.mean(x * x, axis=-1, keepdims=True) + EPS) * g


def _dot(a, b):
    return jnp.dot(a, b, preferred_element_type=F32)


def _dot_nt(a, b):
    return lax.dot_general(a, b, (((1,), (1,)), ((), ())), preferred_element_type=F32)


def _dot_tn(a, b):
    return lax.dot_general(a, b, (((0,), (0,)), ((), ())), preferred_element_type=F32)


def _params(*semantics):
    return pltpu.CompilerParams(dimension_semantics=semantics, vmem_limit_bytes=VMEM_LIMIT_BYTES)


WEIGHT_CHUNK = (256, 512)
WEIGHT_SLOTS = 12
DMA_PRIORITIES = 2

HBM = pl.BlockSpec(memory_space=pl.ANY)


class _WeightSpec(NamedTuple):
    layer: int
    rows: int
    cols: int
    col0: int


def _weight(array, layer, col0=0, cols=None):
    spec = _WeightSpec(layer, array.shape[1], array.shape[2] if cols is None else cols, col0)
    assert spec.rows % WEIGHT_CHUNK[0] == 0 and spec.cols % WEIGHT_CHUNK[1] == 0
    return spec


def _weight_scratch(specs):
    return ([pltpu.VMEM((spec.rows, spec.cols), BF16) for spec in specs]
            + [pltpu.VMEM((WEIGHT_SLOTS,) + WEIGHT_CHUNK, F32), pltpu.SemaphoreType.DMA((WEIGHT_SLOTS,))])


def _resident_weights(first, specs, hbm_refs, scratch):
    *w_refs, stage, sem = scratch
    cr, cc = WEIGHT_CHUNK
    chunks = [(w_hbm, w_ref, spec, r, c)
              for spec, w_hbm, w_ref in zip(specs, hbm_refs, w_refs)
              for r in range(0, spec.rows, cr) for c in range(0, spec.cols, cc)]

    def copy(k):
        w_hbm, _, spec, r, c = chunks[k]
        src = w_hbm.at[spec.layer, pl.ds(r, cr), pl.ds(spec.col0 + c, cc)]
        return pltpu.make_async_copy(src, stage.at[k % WEIGHT_SLOTS], sem.at[k % WEIGHT_SLOTS])

    @pl.when(first)
    def _():
        for k in range(min(WEIGHT_SLOTS, len(chunks))):
            copy(k).start(priority=k % DMA_PRIORITIES)
        for k, (_, w_ref, _, r, c) in enumerate(chunks):
            copy(k).wait()
            w_ref[r:r + cr, c:c + cc] = stage[k % WEIGHT_SLOTS].astype(BF16)
            if k + WEIGHT_SLOTS < len(chunks):
                copy(k + WEIGHT_SLOTS).start(priority=k % DMA_PRIORITIES)

    return w_refs


FFN_TM = 1024
FFN_SUB = 512


def _ffn_rows(x, p_rows, g2_ref, g3_ref, g4_ref, win_ref, wout_ref, wp_ref, wg_ref):
    a = _rms(x, g2_ref[...]).astype(BF16)
    gate = _dot(a, win_ref[:, :FFN_HIDDEN])
    up = _dot(a, win_ref[:, FFN_HIDDEN:])
    hid = (gate * jax.nn.sigmoid(gate) * up).astype(BF16)
    h1 = x + _rms(_dot(hid, wout_ref[...]), g3_ref[...])
    emb = _dot(p_rows.astype(BF16), wp_ref[...])
    gate = jax.nn.sigmoid(_dot(_rms(h1, g4_ref[...]).astype(BF16), wg_ref[...]))
    return h1 + gate * emb


def _ffn_ple_kernel(h_ref, p_ref, g2_ref, g3_ref, g4_ref, win_hbm, wout_hbm, wp_hbm, wg_hbm, o_ref, *wscr, specs):
    weights = _resident_weights(pl.program_id(0) == 0, specs, (win_hbm, wout_hbm, wp_hbm, wg_hbm), wscr)
    for part in range(FFN_TM // FFN_SUB):
        rows = slice(part * FFN_SUB, (part + 1) * FFN_SUB)
        o_ref[rows, :] = _ffn_rows(h_ref[rows, :], p_ref[rows, :], g2_ref, g3_ref, g4_ref, *weights)


def _ffn_ple(h, p, gains, w_in, w_out, w_proj, w_gate, layer):
    n, d = h.shape
    row = lambda i: (i, 0)
    gain = lambda k: _layer_resident(gains.shape, layer * N_NORMS + k)
    specs = tuple(_weight(w, layer) for w in (w_in, w_out, w_proj, w_gate))
    return pl.pallas_call(
        functools.partial(_ffn_ple_kernel, specs=specs),
        grid=(n // FFN_TM,),
        in_specs=[
            pl.BlockSpec((FFN_TM, d), row),
            pl.BlockSpec((None, FFN_TM, PLE_DIM), lambda i: (layer, i, 0)),
            gain(2), gain(3), gain(4),
            HBM, HBM, HBM, HBM,
        ],
        out_specs=pl.BlockSpec((FFN_TM, d), row),
        out_shape=jax.ShapeDtypeStruct((n, d), F32),
        scratch_shapes=_weight_scratch(specs),
        compiler_params=_params("arbitrary"),
        name="ffn_ple",
    )(h, p, gains, gains, gains, w_in, w_out, w_proj, w_gate)


RET_TM = 512
RET_SUB = 512
RET_Q_OFF = 0
RET_K_OFF = RET_HEADS * RET_QK_DIM
RET_V_OFF = 2 * RET_HEADS * RET_QK_DIM
RET_G_OFF = RET_V_OFF + RET_HEADS * RET_V_DIM


def _rope(t, cos, sin):
    half = RET_QK_DIM // 2
    t1, t2 = t[:, :half], t[:, half:]
    return jnp.concatenate([t1 * cos - t2 * sin, t2 * cos + t1 * sin], axis=-1)


def _retention_kernel(h_ref, cos_ref, sin_ref, g0_ref, g1_ref, win_hbm, wout_hbm,
                      dmask_ref, qdec_ref, kdec_ref, cdec_ref, o_ref, state_ref, y_ref, *wscr, specs):
    win_ref, wout_ref = _resident_weights(
        (pl.program_id(0) == 0) & (pl.program_id(1) == 0), specs, (win_hbm, wout_hbm), wscr)

    @pl.when(pl.program_id(1) == 0)
    def _():
        state_ref[...] = jnp.zeros(state_ref.shape, F32)

    for part in range(RET_TM // RET_SUB):
        prow = slice(part * RET_SUB, (part + 1) * RET_SUB)
        x = h_ref[prow, :]
        a = _rms(x, g0_ref[...]).astype(BF16)
        cos, sin = cos_ref[prow, :], sin_ref[prow, :]
        for hd in range(RET_HEADS):
            qo, ko, vo = RET_Q_OFF + hd * RET_QK_DIM, RET_K_OFF + hd * RET_QK_DIM, RET_V_OFF + hd * RET_V_DIM
            q = _rope(_dot(a, win_ref[:, qo:qo + RET_QK_DIM]), cos, sin).astype(BF16)
            k = _rope(_dot(a, win_ref[:, ko:ko + RET_QK_DIM]), cos, sin) * (RET_QK_DIM ** -0.5)
            v = _dot(a, win_ref[:, vo:vo + RET_V_DIM]).astype(BF16)
            for c in range(RET_SUB // RET_CHUNK):
                rows = slice(c * RET_CHUNK, (c + 1) * RET_CHUNK)
                qc, kc, vc = q[rows], k[rows], v[rows]
                st = state_ref[hd]
                scores = _dot_nt(qc, kc.astype(BF16)) * dmask_ref[hd]
                inner = _dot(scores.astype(BF16), vc)
                cross = _dot(qc, st.astype(BF16)) * qdec_ref[hd]
                state_ref[hd] = st * cdec_ref[hd] + _dot_tn((kc * kdec_ref[hd]).astype(BF16), vc)
                y = inner + cross
                y_ref[rows, hd * RET_V_DIM:(hd + 1) * RET_V_DIM] = y * lax.rsqrt(
                    jnp.mean(y * y, axis=-1, keepdims=True) + EPS)
        gate = _dot(a, win_ref[:, RET_G_OFF:RET_G_OFF + RET_HEADS * RET_V_DIM])
        z = (gate * jax.nn.sigmoid(gate) * y_ref[...]).astype(BF16)
        o_ref[prow, :] = x + _rms(_dot(z, wout_ref[...]), g1_ref[...])


def _retention_consts(seq_len):
    hh, c, dk = RET_HEADS, RET_CHUNK, RET_QK_DIM
    inv_freq = np.float32(ROPE_BASE) ** (-np.arange(0, dk, 2, dtype=np.float32) / np.float32(dk))
    ang = np.arange(seq_len, dtype=np.float32)[:, None] * inv_freq[None, :]
    cos = np.cos(ang.astype(np.float64)).astype(np.float32)
    sin = np.sin(ang.astype(np.float64)).astype(np.float32)
    log_gamma = jnp.log1p(-jnp.exp2(-5.0 - jnp.arange(hh, dtype=F32)))
    idx = jnp.arange(c, dtype=F32)
    diff = idx[:, None] - idx[None, :]
    dmask = jnp.where(diff >= 0, jnp.exp(log_gamma[:, None, None] * jnp.maximum(diff, 0.0)), 0.0)
    qdec = jnp.exp(log_gamma[:, None] * (idx[None, :] + 1.0))[:, :, None]
    kdec = jnp.exp(log_gamma[:, None] * (c - 1.0 - idx[None, :]))[:, :, None]
    cdec = jnp.exp(log_gamma * c)[:, None, None]
    return jnp.asarray(cos), jnp.asarray(sin), dmask, qdec, kdec, cdec


def _retention_mixer(h, gains, w_in, w_out, consts, layer, j):
    b, s, d = h.shape
    cos, sin, dmask, qdec, kdec, cdec = consts
    tile = lambda bi, t: (bi, t, 0)
    pos = lambda bi, t: (t, 0)
    gain = lambda k: _layer_resident(gains.shape, layer * N_NORMS + k)
    specs = (_weight(w_in, j), _weight(w_out, j))
    return pl.pallas_call(
        functools.partial(_retention_kernel, specs=specs),
        grid=(b, s // RET_TM),
        in_specs=[
            pl.BlockSpec((None, RET_TM, d), tile),
            pl.BlockSpec((RET_TM, RET_QK_DIM // 2), pos),
            pl.BlockSpec((RET_TM, RET_QK_DIM // 2), pos),
            gain(0), gain(1),
            HBM, HBM,
            _resident(dmask.shape), _resident(qdec.shape), _resident(kdec.shape), _resident(cdec.shape),
        ],
        out_specs=pl.BlockSpec((None, RET_TM, d), tile),
        out_shape=jax.ShapeDtypeStruct((b, s, d), F32),
        scratch_shapes=[
            pltpu.VMEM((RET_HEADS, RET_QK_DIM, RET_V_DIM), F32),
            pltpu.VMEM((RET_SUB, RET_HEADS * RET_V_DIM), F32),
        ] + _weight_scratch(specs),
        compiler_params=_params("arbitrary", "arbitrary"),
        name="retention_mixer",
    )(h, cos, sin, gains, gains, w_in, w_out, dmask, qdec, kdec, cdec)


GM_TM = 1024
GM_SUB = 512


def _gmlp_kernel(h_ref, g0_ref, g1_ref, win_hbm, lng_ref, lnb_ref, ws_ref, bs_ref, wout_hbm, o_ref, s_ref, *wscr,
                 specs):
    win_ref, wout_ref = _resident_weights(pl.program_id(0) == 0, specs, (win_hbm, wout_hbm), wscr)

    def gelu(t):
        return 0.5 * t * (1.0 + lax.erf(t * (2.0 ** -0.5)))

    row = lax.broadcasted_iota(jnp.int32, (GM_CHUNK, GM_CHUNK), 0)
    col = lax.broadcasted_iota(jnp.int32, (GM_CHUNK, GM_CHUNK), 1)
    causal = row >= col
    for part in range(GM_TM // GM_SUB):
        prow = slice(part * GM_SUB, (part + 1) * GM_SUB)
        x = h_ref[prow, :]
        a = _rms(x, g0_ref[...]).astype(BF16)
        v = gelu(_dot(a, win_ref[:, GM_WIDTH:]))
        u = gelu(_dot(a, win_ref[:, :GM_WIDTH]))
        mu = jnp.mean(v, axis=-1, keepdims=True)
        vc = v - mu
        var = jnp.mean(vc * vc, axis=-1, keepdims=True)
        vn = (vc * lax.rsqrt(var + EPS) * lng_ref[...] + lnb_ref[...]).astype(BF16)
        for g in range(GM_GROUPS):
            w = jnp.where(causal, ws_ref[g], 0.0).astype(BF16)
            cols = slice(g * GM_GROUP_DIM, (g + 1) * GM_GROUP_DIM)
            for c in range(GM_SUB // GM_CHUNK):
                rows = slice(c * GM_CHUNK, (c + 1) * GM_CHUNK)
                s_ref[rows, cols] = _dot(w, vn[rows, cols]) + bs_ref[g]
        z = (u * s_ref[...]).astype(BF16)
        o_ref[prow, :] = x + _rms(_dot(z, wout_ref[...]), g1_ref[...])


def _gmlp_mixer(h, gains, w_in, ln_g, ln_b, w_s, b_s, w_out, layer, j):
    n, d = h.shape
    row = lambda i: (i, 0)
    gain = lambda k: _layer_resident(gains.shape, layer * N_NORMS + k)
    specs = (_weight(w_in, j), _weight(w_out, j))
    return pl.pallas_call(
        functools.partial(_gmlp_kernel, specs=specs),
        grid=(n // GM_TM,),
        in_specs=[
            pl.BlockSpec((GM_TM, d), row),
            gain(0), gain(1),
            HBM, _layer_resident(ln_g.shape, j), _layer_resident(ln_b.shape, j),
            _layer_resident(w_s.shape, j), _layer_resident(b_s.shape, j), HBM,
        ],
        out_specs=pl.BlockSpec((GM_TM, d), row),
        out_shape=jax.ShapeDtypeStruct((n, d), F32),
        scratch_shapes=[pltpu.VMEM((GM_SUB, GM_WIDTH), F32)] + _weight_scratch(specs),
        compiler_params=_params("arbitrary"),
        name="gmlp_mixer",
    )(h, gains, gains, w_in, ln_g, ln_b, w_s, b_s, w_out)


ATT_D = ATT_HEADS * ATT_HEAD_DIM


class _GroupPlan(NamedTuple):
    dilation: int
    view_minor: int
    block_rows: int
    halves: int
    n_units: int
    n_slots: int
    pieces: tuple
    first_units: tuple
    positions: np.ndarray

    @property
    def tokens_per_step(self):
        return self.n_units * ATT_BLK


def _group_plan(dilation):
    if dilation == 1:
        return _GroupPlan(1, 0, 8 * ATT_BLK, 1, 8, 1,
                          tuple(((None, u * ATT_BLK, ATT_BLK),) for u in range(8)), (0,),
                          np.arange(ATT_BLK))
    if dilation == 4:
        half = ATT_BLK // 2
        return _GroupPlan(4, 8, ATT_BLK, 1, 8, 4,
                          tuple(((c, blk * half, half), (c + 4, blk * half, half))
                                for blk in range(2) for c in range(4)),
                          (0, 1, 2, 3),
                          np.concatenate([2 * np.arange(half), 2 * np.arange(half) + 1]))
    assert dilation == 16
    return _GroupPlan(16, 16, ATT_BLK, 2, 8, 8,
                      tuple(((c, 0, ATT_BLK),) for c in range(8)), tuple(range(8)),
                      np.arange(ATT_BLK))


def _t5_bucket(dist):
    d = jnp.maximum(dist, 1).astype(F32)
    large = REL_MAX_EXACT + (jnp.log(d / REL_MAX_EXACT) / math.log(REL_MAX_DIST / REL_MAX_EXACT)
                             * (REL_BUCKETS - REL_MAX_EXACT)).astype(jnp.int32)
    large = jnp.minimum(large, REL_BUCKETS - 1)
    return jnp.where(dist < REL_MAX_EXACT, dist, large)


def _bias_kernel(table_ref, bucket_ref, dist_ref, o_ref):
    g = pl.program_id(0)
    bucket = bucket_ref[...]
    dist = dist_ref[...]
    band = (dist >= 0) & (dist <= ATT_BLK)
    for hd in range(ATT_HEADS):
        acc = jnp.zeros(bucket.shape, F32)
        for b in range(REL_BUCKETS):
            acc = jnp.where(bucket == b, table_ref[b, g * ATT_HEADS + hd], acc)
        o_ref[hd] = jnp.where(band, acc * LOG2E, MASK_VALUE)


def _attention_bias(rel_bias, plans):
    dists = []
    for plan in plans:
        pos_k = np.concatenate([plan.positions - ATT_BLK, plan.positions])
        dists.append(plan.positions[:, None] - pos_k[None, :])
    dist = jnp.asarray(np.stack(dists), jnp.int32)
    scale = jnp.asarray([plan.dilation for plan in plans], jnp.int32)[:, None, None]
    buckets = _t5_bucket(jnp.maximum(dist, 0) * scale).astype(jnp.int32)
    tile = pl.BlockSpec((None, ATT_BLK, 2 * ATT_BLK), lambda g: (g, 0, 0))
    return pl.pallas_call(
        _bias_kernel,
        grid=(len(plans),),
        in_specs=[pl.BlockSpec(memory_space=pltpu.SMEM), tile, tile],
        out_specs=pl.BlockSpec((None, ATT_HEADS, ATT_BLK, 2 * ATT_BLK), lambda g: (g, 0, 0, 0)),
        out_shape=jax.ShapeDtypeStruct((len(plans), ATT_HEADS, ATT_BLK, 2 * ATT_BLK), F32),
        compiler_params=_params("arbitrary"),
        name="attention_bias",
    )(rel_bias, buckets, dist)


def _attn_units(plan, t, xa, g0_ref, w_ref, bias_ref, k_ref, v_ref, q_ref, put_o, put_lse):
    @pl.when(t == 0)
    def _():
        k_ref[:, :ATT_BLK, :] = jnp.zeros((plan.n_slots, ATT_BLK, ATT_D), BF16)
        v_ref[:, :ATT_BLK, :] = jnp.zeros((plan.n_slots, ATT_BLK, ATT_D), BF16)

    a = _rms(xa, g0_ref[...]).astype(BF16)
    q_ref[...] = (_dot(a, w_ref[:, :ATT_D]) * (ATT_HEAD_DIM ** -0.5 * LOG2E)).astype(BF16)
    kc = _dot(a, w_ref[:, ATT_D:2 * ATT_D]).astype(BF16)
    vc = _dot(a, w_ref[:, 2 * ATT_D:]).astype(BF16)

    lane = lax.broadcasted_iota(jnp.int32, (ATT_BLK, V7X_LANES), 1)
    low = lane < ATT_HEAD_DIM
    kcol = lax.broadcasted_iota(jnp.int32, (ATT_BLK, 2 * ATT_BLK), 1)
    first_pen = jnp.where((kcol < ATT_BLK) & (t == 0), MASK_VALUE, 0.0)
    for u in range(plan.n_units):
        slot = u % plan.n_slots
        rows = slice(u * ATT_BLK, (u + 1) * ATT_BLK)
        k_ref[slot, ATT_BLK:, :] = kc[rows]
        v_ref[slot, ATT_BLK:, :] = vc[rows]
        m_tile = jnp.zeros((ATT_BLK, V7X_LANES), F32)
        den_tile = jnp.ones((ATT_BLK, V7X_LANES), F32)
        for hp in range(ATT_HEADS // 2):
            cols = slice(hp * V7X_LANES, (hp + 1) * V7X_LANES)
            qp = q_ref[rows, cols]
            kk = k_ref[slot, :, cols]
            vv = v_ref[slot, :, cols]
            outs, lses = [], []
            for sub in range(2):
                qm = jnp.where(low if sub == 0 else ~low, qp, jnp.zeros_like(qp))
                s = _dot_nt(qm, kk) + bias_ref[2 * hp + sub]
                if u in plan.first_units:
                    s = s + first_pen
                m = jnp.max(s, axis=-1, keepdims=True)
                e = jnp.exp2(s - m)
                den = jnp.sum(e, axis=-1, keepdims=True)
                outs.append(_dot(e.astype(BF16), vv) / den)
                lses.append((m, den))
            put_o(rows, cols, jnp.where(low, outs[0], outs[1]))
            m_tile = jnp.where(lane == 2 * hp, lses[0][0], jnp.where(lane == 2 * hp + 1, lses[1][0], m_tile))
            den_tile = jnp.where(lane == 2 * hp, lses[0][1], jnp.where(lane == 2 * hp + 1, lses[1][1], den_tile))
        put_lse(rows, m_tile * LN2 + jnp.log(den_tile))
        k_ref[slot, :ATT_BLK, :] = k_ref[slot, ATT_BLK:, :]
        v_ref[slot, :ATT_BLK, :] = v_ref[slot, ATT_BLK:, :]


def _attn_dense_kernel(x_ref, g0_ref, w_hbm, bias_ref, o_ref, lse_ref, k_ref, v_ref, q_ref, *wscr, plan, specs):
    (w_ref,) = _resident_weights((pl.program_id(0) == 0) & (pl.program_id(2) == 0), specs, (w_hbm,), wscr)

    def put_o(rows, cols, tile):
        o_ref[rows, cols] = tile.astype(o_ref.dtype)

    def put_lse(rows, tile):
        lse_ref[rows, :] = tile

    _attn_units(plan, pl.program_id(2), x_ref[...], g0_ref, w_ref, bias_ref, k_ref, v_ref, q_ref, put_o, put_lse)


def _attn_strided_kernel(x_hbm, g0_ref, w_hbm, bias_ref, o_hbm, lse_hbm, k_ref, v_ref, q_ref,
                         xbuf, obuf, lbuf, xsem, osem, lsem, *wscr, plan, specs, n_batch, n_steps):
    b, hf, t = pl.program_id(0), pl.program_id(1), pl.program_id(2)
    step = (b * plan.halves + hf) * n_steps + t
    n_total = n_batch * plan.halves * n_steps
    buf = lax.rem(step, 2)

    def class_rows(ref, bb, hh, tt, sub, r0, n):
        return ref.at[bb, pl.ds(tt * plan.block_rows + r0, n), hh * V7X_SUBLANES + sub]

    def copies(kind, bb, hh, tt, which):
        out = []
        for u, pcs in enumerate(plan.pieces):
            at = u * ATT_BLK
            for sub, r0, n in pcs:
                dense = pl.ds(at, n)
                if kind == "gather":
                    out.append(pltpu.make_async_copy(class_rows(x_hbm, bb, hh, tt, sub, r0, n),
                                                     xbuf.at[which, dense], xsem.at[which]))
                else:
                    out.append(pltpu.make_async_copy(obuf.at[which, dense],
                                                     class_rows(o_hbm, bb, hh, tt, sub, r0, n), osem.at[which]))
                    out.append(pltpu.make_async_copy(lbuf.at[which, dense],
                                                     class_rows(lse_hbm, bb, hh, tt, sub, r0, n), lsem.at[which]))
                at += n
        return out

    @pl.when(step == 0)
    def _():
        for cp in copies("gather", b, hf, t, buf):
            cp.start()

    wrap_t = t + 1 == n_steps
    wrap_h = hf + 1 == plan.halves
    nt = jnp.where(wrap_t, 0, t + 1)
    nh = jnp.where(wrap_t, jnp.where(wrap_h, 0, hf + 1), hf)
    nb = jnp.where(wrap_t & wrap_h, b + 1, b)

    @pl.when(step + 1 < n_total)
    def _():
        for cp in copies("gather", nb, nh, nt, 1 - buf):
            cp.start()

    (w_ref,) = _resident_weights(step == 0, specs, (w_hbm,), wscr)

    for cp in copies("gather", b, hf, t, buf):
        cp.wait()

    @pl.when(step >= 2)
    def _():
        for cp in copies("scatter", b, hf, t, buf):
            cp.wait()

    def put_o(rows, cols, tile):
        obuf[buf, rows, cols] = tile

    def put_lse(rows, tile):
        lbuf[buf, rows, :] = tile

    _attn_units(plan, t, xbuf[buf], g0_ref, w_ref, bias_ref, k_ref, v_ref, q_ref, put_o, put_lse)

    for cp in copies("scatter", b, hf, t, buf):
        cp.start()

    @pl.when(step == n_total - 1)
    def _():
        for cp in copies("scatter", b, hf, t, 1 - buf):
            cp.wait()
        for cp in copies("scatter", b, hf, t, buf):
            cp.wait()


def _attention_group(h, gains, w_in, bias, plan, group, layer, j):
    b, s, d = h.shape
    tokens = plan.tokens_per_step
    specs = (_weight(w_in, j, col0=group * 3 * ATT_D, cols=3 * ATT_D),)
    resident = [
        _layer_resident(gains.shape, layer * N_NORMS),
        HBM,
        pl.BlockSpec((None, ATT_HEADS, ATT_BLK, 2 * ATT_BLK), lambda bi, hf, t: (group, 0, 0, 0),
                     pipeline_mode=pl.Buffered(1)),
    ]
    kv_q = [
        pltpu.VMEM((plan.n_slots, 2 * ATT_BLK, ATT_D), BF16),
        pltpu.VMEM((plan.n_slots, 2 * ATT_BLK, ATT_D), BF16),
        pltpu.VMEM((tokens, ATT_D), BF16),
    ]
    if not plan.view_minor:
        steps = s // plan.block_rows
        block = lambda width: pl.BlockSpec((None, plan.block_rows, width), lambda bi, hf, t: (bi, t, 0))
        o, lse = pl.pallas_call(
            functools.partial(_attn_dense_kernel, plan=plan, specs=specs),
            grid=(b, 1, steps),
            in_specs=[block(d)] + resident,
            out_specs=[block(ATT_D), block(V7X_LANES)],
            out_shape=[jax.ShapeDtypeStruct((b, s, ATT_D), BF16), jax.ShapeDtypeStruct((b, s, V7X_LANES), F32)],
            scratch_shapes=kv_q + _weight_scratch(specs),
            compiler_params=_params("arbitrary", "arbitrary", "arbitrary"),
            name=f"attention_group{group}",
        )(h, gains, w_in, bias)
    else:
        lead = s // plan.view_minor
        steps = lead // plan.block_rows
        assert b * plan.halves * steps >= 2
        view = lambda width: (b, lead, plan.view_minor, width)
        o, lse = pl.pallas_call(
            functools.partial(_attn_strided_kernel, plan=plan, specs=specs, n_batch=b, n_steps=steps),
            grid=(b, plan.halves, steps),
            in_specs=[HBM] + resident,
            out_specs=[HBM, HBM],
            out_shape=[jax.ShapeDtypeStruct(view(ATT_D), F32), jax.ShapeDtypeStruct(view(V7X_LANES), F32)],
            scratch_shapes=kv_q + [
                pltpu.VMEM((2, tokens, d), F32),
                pltpu.VMEM((2, tokens, ATT_D), F32),
                pltpu.VMEM((2, tokens, V7X_LANES), F32),
                pltpu.SemaphoreType.DMA((2,)), pltpu.SemaphoreType.DMA((2,)), pltpu.SemaphoreType.DMA((2,)),
            ] + _weight_scratch(specs),
            compiler_params=_params("arbitrary", "arbitrary", "arbitrary"),
            name=f"attention_group{group}",
        )(h.reshape(view(d)), gains, w_in, bias)
    return o.reshape(b * s, ATT_D), lse.reshape(b * s, V7X_LANES)


ATT_MERGE_TM = 512
ATT_MERGE_VMEM_LIMIT_BYTES = V7X_VMEM_BYTES - 4 * 1024 * 1024


def _attn_merge_ffn_kernel(h_ref, o0_ref, o1_ref, o2_ref, l0_ref, l1_ref, l2_ref, p_ref, g1_ref, g2_ref, g3_ref,
                           g4_ref, expand_ref, wao_hbm, win_hbm, wout_hbm, wp_hbm, wg_hbm, out_ref, *wscr, specs):
    wout_ref, *ffn_weights = _resident_weights(
        pl.program_id(0) == 0, specs, (wao_hbm, win_hbm, wout_hbm, wp_hbm, wg_hbm), wscr)
    l0, l1, l2 = l0_ref[...], l1_ref[...], l2_ref[...]
    m = jnp.maximum(jnp.maximum(l0, l1), l2)
    e0, e1, e2 = jnp.exp(l0 - m), jnp.exp(l1 - m), jnp.exp(l2 - m)
    inv = 1.0 / (e0 + e1 + e2)
    expand = expand_ref[...]

    def per_channel(w):
        hi = w.astype(BF16)
        lo = (w - hi.astype(F32)).astype(BF16)
        return _dot(jnp.concatenate([hi, lo], axis=1), expand)

    o = (per_channel(e0 * inv) * o0_ref[...].astype(F32) + per_channel(e1 * inv) * o1_ref[...]
         + per_channel(e2 * inv) * o2_ref[...])
    x = h_ref[...] + _rms(_dot(o.astype(BF16), wout_ref[...]), g1_ref[...])
    out_ref[...] = _ffn_rows(x, p_ref[...], g2_ref, g3_ref, g4_ref, *ffn_weights)


def _attention_merge_ffn(h, outs, lses, p, gains, attn_w_out, w_in, w_out, w_proj, w_gate, layer, j):
    n, d = h.shape
    row = lambda i: (i, 0)
    tile = pl.BlockSpec((ATT_MERGE_TM, d), row)
    lse_tile = pl.BlockSpec((ATT_MERGE_TM, V7X_LANES), row)
    expand = np.zeros((V7X_LANES, ATT_D), np.float32)
    for hd in range(ATT_HEADS):
        expand[hd, hd * ATT_HEAD_DIM:(hd + 1) * ATT_HEAD_DIM] = 1.0
    expand = jnp.asarray(np.concatenate([expand, expand], axis=0), BF16)
    gain = lambda k: _layer_resident(gains.shape, layer * N_NORMS + k)
    specs = (_weight(attn_w_out, j),) + tuple(_weight(w, layer) for w in (w_in, w_out, w_proj, w_gate))
    return pl.pallas_call(
        functools.partial(_attn_merge_ffn_kernel, specs=specs),
        grid=(n // ATT_MERGE_TM,),
        in_specs=[tile] * 4 + [lse_tile] * 3 + [
            pl.BlockSpec((None, ATT_MERGE_TM, PLE_DIM), lambda i: (layer, i, 0)),
            gain(1), gain(2), gain(3), gain(4), _resident(expand.shape), HBM, HBM, HBM, HBM, HBM],
        out_specs=tile,
        out_shape=jax.ShapeDtypeStruct((n, d), F32),
        scratch_shapes=_weight_scratch(specs),
        compiler_params=pltpu.CompilerParams(dimension_semantics=("arbitrary",),
                                             vmem_limit_bytes=ATT_MERGE_VMEM_LIMIT_BYTES),
        name="attention_merge_ffn",
    )(h, *outs, *lses, p, gains, gains, gains, gains, expand, attn_w_out, w_in, w_out, w_proj, w_gate)


def kernel(x, p, norm_g, ret_w_in, ret_w_out, attn_w_in, attn_w_out, rel_bias, gm_w_in, gm_ln_g, gm_ln_b,
           gm_w_s, gm_b_s, gm_w_out, ffn_w_in, ffn_w_out, ple_w_proj, ple_w_gate):
    b, s, d = x.shape
    n = b * s
    gains = norm_g.reshape(DEPTH * N_NORMS, 1, d)
    p = p.reshape(DEPTH, n, PLE_DIM)
    gm_ln_g = gm_ln_g.reshape(-1, 1, GM_WIDTH)
    gm_ln_b = gm_ln_b.reshape(-1, 1, GM_WIDTH)
    gm_b_s = gm_b_s.reshape(-1, GM_GROUPS, GM_CHUNK, 1)
    ret_consts = _retention_consts(s)
    plans = tuple(_group_plan(dil) for _, dil in DILATION_PAIRS)

    h = x
    for i in range(DEPTH):
        kind, j = i % N_MIXERS, i // N_MIXERS
        if kind == 0:
            h = _retention_mixer(h.reshape(b, s, d), gains, ret_w_in, ret_w_out, ret_consts, i, j)
        elif kind == 1:
            bias = _attention_bias(rel_bias, plans)
            outs, lses = [], []
            for gi, plan in enumerate(plans):
                o, lse = _attention_group(h.reshape(b, s, d), gains, attn_w_in, bias, plan, gi, i, j)
                outs.append(o)
                lses.append(lse)
            h = _attention_merge_ffn(h.reshape(n, d), outs, lses, p, gains, attn_w_out,
                                     ffn_w_in, ffn_w_out, ple_w_proj, ple_w_gate, i, j)
            continue
        else:
            h = _gmlp_mixer(h.reshape(n, d), gains, gm_w_in, gm_ln_g, gm_ln_b, gm_w_s, gm_b_s, gm_w_out, i, j)
        h = _ffn_ple(h.reshape(n, d), p, gains, ffn_w_in, ffn_w_out, ple_w_proj, ple_w_gate, i)
    return h.reshape(b, s, d)
```

```python
import functools
import math
from typing import NamedTuple

import jax
import jax.numpy as jnp
import numpy as np
from jax import lax
from jax.experimental import pallas as pl
from jax.experimental.pallas import tpu as pltpu

D_MODEL = 1024
DEPTH = 4
N_MIXERS = 3
N_NORMS = 5
PLE_DIM = 256
EPS = 1e-6

RET_HEADS = 4
RET_QK_DIM = 256
RET_V_DIM = 512
RET_CHUNK = 256
ROPE_BASE = 10000.0

ATT_HEADS = 16
ATT_HEAD_DIM = 64
DILATION_PAIRS = ((128, 1), (512, 4), (2048, 16))
ATT_BLK = 128
REL_BUCKETS = 32
REL_MAX_EXACT = 16
REL_MAX_DIST = 2048

GM_CHUNK = 128
GM_WIDTH = 2 * D_MODEL
GM_GROUPS = 8
GM_GROUP_DIM = GM_WIDTH // GM_GROUPS

FFN_HIDDEN = 2816

V7X_LANES = 128
V7X_SUBLANES = 8
V7X_VMEM_BYTES = 64 * 1024 * 1024
VMEM_LIMIT_BYTES = V7X_VMEM_BYTES - 8 * 1024 * 1024

MASK_VALUE = -1e30
LOG2E = math.log2(math.e)
LN2 = math.log(2.0)

BF16 = jnp.bfloat16
F32 = jnp.float32


def _resident(shape):
    return pl.BlockSpec(shape, lambda *_: (0,) * len(shape), pipeline_mode=pl.Buffered(1))


def _layer_resident(shape, layer):
    rest = tuple(shape[1:])
    return pl.BlockSpec((None,) + rest, lambda *_: (layer,) + (0,) * len(rest), pipeline_mode=pl.Buffered(1))


def _rms(x, g):
    return x * lax.rsqrt(jnp.mean(x * x, axis=-1, keepdims=True) + EPS) * g


def _dot(a, b):
    return jnp.dot(a, b, preferred_element_type=F32)


def _dot_nt(a, b):
    return lax.dot_general(a, b, (((1,), (1,)), ((), ())), preferred_element_type=F32)


def _dot_tn(a, b):
    return lax.dot_general(a, b, (((0,), (0,)), ((), ())), preferred_element_type=F32)


def _params(*semantics):
    return pltpu.CompilerParams(dimension_semantics=semantics, vmem_limit_bytes=VMEM_LIMIT_BYTES)


WEIGHT_CHUNK = (256, 512)
WEIGHT_SLOTS = 12

HBM = pl.BlockSpec(memory_space=pl.ANY)


class _WeightSpec(NamedTuple):
    layer: int
    rows: int
    cols: int
    col0: int


def _weight(array, layer, col0=0, cols=None):
    spec = _WeightSpec(layer, array.shape[1], array.shape[2] if cols is None else cols, col0)
    assert spec.rows % WEIGHT_CHUNK[0] == 0 and spec.cols % WEIGHT_CHUNK[1] == 0
    return spec


def _weight_scratch(specs):
    return ([pltpu.VMEM((spec.rows, spec.cols), BF16) for spec in specs]
            + [pltpu.VMEM((WEIGHT_SLOTS,) + WEIGHT_CHUNK, F32), pltpu.SemaphoreType.DMA((WEIGHT_SLOTS,))])


def _resident_weights(first, specs, hbm_refs, scratch):
    *w_refs, stage, sem = scratch
    cr, cc = WEIGHT_CHUNK
    chunks = [(w_hbm, w_ref, spec, r, c)
              for spec, w_hbm, w_ref in zip(specs, hbm_refs, w_refs)
              for r in range(0, spec.rows, cr) for c in range(0, spec.cols, cc)]

    def copy(k):
        w_hbm, _, spec, r, c = chunks[k]
        src = w_hbm.at[spec.layer, pl.ds(r, cr), pl.ds(spec.col0 + c, cc)]
        return pltpu.make_async_copy(src, stage.at[k % WEIGHT_SLOTS], sem.at[k % WEIGHT_SLOTS])

    @pl.when(first)
    def _():
        for k in range(min(WEIGHT_SLOTS, len(chunks))):
            copy(k).start()
        for k, (_, w_ref, _, r, c) in enumerate(chunks):
            copy(k).wait()
            w_ref[r:r + cr, c:c + cc] = stage[k % WEIGHT_SLOTS].astype(BF16)
            if k + WEIGHT_SLOTS < len(chunks):
                copy(k + WEIGHT_SLOTS).start()

    return w_refs


FFN_TM = 1024
FFN_SUB = 512


def _ffn_rows(x, p_rows, g2_ref, g3_ref, g4_ref, win_ref, wout_ref, wp_ref, wg_ref):
    a = _rms(x, g2_ref[...]).astype(BF16)
    gate = _dot(a, win_ref[:, :FFN_HIDDEN])
    up = _dot(a, win_ref[:, FFN_HIDDEN:])
    hid = (gate * jax.nn.sigmoid(gate) * up).astype(BF16)
    h1 = x + _rms(_dot(hid, wout_ref[...]), g3_ref[...])
    emb = _dot(p_rows.astype(BF16), wp_ref[...])
    gate = jax.nn.sigmoid(_dot(_rms(h1, g4_ref[...]).astype(BF16), wg_ref[...]))
    return h1 + gate * emb


def _ffn_ple_kernel(h_ref, p_ref, g2_ref, g3_ref, g4_ref, win_hbm, wout_hbm, wp_hbm, wg_hbm, o_ref, *wscr, specs):
    weights = _resident_weights(pl.program_id(0) == 0, specs, (win_hbm, wout_hbm, wp_hbm, wg_hbm), wscr)
    for part in range(FFN_TM // FFN_SUB):
        rows = slice(part * FFN_SUB, (part + 1) * FFN_SUB)
        o_ref[rows, :] = _ffn_rows(h_ref[rows, :], p_ref[rows, :], g2_ref, g3_ref, g4_ref, *weights)


def _ffn_ple(h, p, gains, w_in, w_out, w_proj, w_gate, layer):
    n, d = h.shape
    row = lambda i: (i, 0)
    gain = lambda k: _layer_resident(gains.shape, layer * N_NORMS + k)
    specs = tuple(_weight(w, layer) for w in (w_in, w_out, w_proj, w_gate))
    return pl.pallas_call(
        functools.partial(_ffn_ple_kernel, specs=specs),
        grid=(n // FFN_TM,),
        in_specs=[
            pl.BlockSpec((FFN_TM, d), row),
            pl.BlockSpec((None, FFN_TM, PLE_DIM), lambda i: (layer, i, 0)),
            gain(2), gain(3), gain(4),
            HBM, HBM, HBM, HBM,
        ],
        out_specs=pl.BlockSpec((FFN_TM, d), row),
        out_shape=jax.ShapeDtypeStruct((n, d), F32),
        scratch_shapes=_weight_scratch(specs),
        compiler_params=_params("arbitrary"),
        name="ffn_ple",
    )(h, p, gains, gains, gains, w_in, w_out, w_proj, w_gate)


RET_TM = 512
RET_SUB = 512
RET_Q_OFF = 0
RET_K_OFF = RET_HEADS * RET_QK_DIM
RET_V_OFF = 2 * RET_HEADS * RET_QK_DIM
RET_G_OFF = RET_V_OFF + RET_HEADS * RET_V_DIM


def _rope(t, cos, sin):
    half = RET_QK_DIM // 2
    t1, t2 = t[:, :half], t[:, half:]
    return jnp.concatenate([t1 * cos - t2 * sin, t2 * cos + t1 * sin], axis=-1)


def _retention_kernel(h_ref, cos_ref, sin_ref, g0_ref, g1_ref, win_hbm, wout_hbm,
                      dmask_ref, qdec_ref, kdec_ref, cdec_ref, o_ref, state_ref, y_ref, *wscr, specs):
    win_ref, wout_ref = _resident_weights(
        (pl.program_id(0) == 0) & (pl.program_id(1) == 0), specs, (win_hbm, wout_hbm), wscr)

    @pl.when(pl.program_id(1) == 0)
    def _():
        state_ref[...] = jnp.zeros(state_ref.shape, F32)

    for part in range(RET_TM // RET_SUB):
        prow = slice(part * RET_SUB, (part + 1) * RET_SUB)
        x = h_ref[prow, :]
        a = _rms(x, g0_ref[...]).astype(BF16)
        cos, sin = cos_ref[prow, :], sin_ref[prow, :]
        for hd in range(RET_HEADS):
            qo, ko, vo = RET_Q_OFF + hd * RET_QK_DIM, RET_K_OFF + hd * RET_QK_DIM, RET_V_OFF + hd * RET_V_DIM
            q = _rope(_dot(a, win_ref[:, qo:qo + RET_QK_DIM]), cos, sin).astype(BF16)
            k = _rope(_dot(a, win_ref[:, ko:ko + RET_QK_DIM]), cos, sin) * (RET_QK_DIM ** -0.5)
            v = _dot(a, win_ref[:, vo:vo + RET_V_DIM]).astype(BF16)
            for c in range(RET_SUB // RET_CHUNK):
                rows = slice(c * RET_CHUNK, (c + 1) * RET_CHUNK)
                qc, kc, vc = q[rows], k[rows], v[rows]
                st = state_ref[hd]
                scores = _dot_nt(qc, kc.astype(BF16)) * dmask_ref[hd]
                inner = _dot(scores.astype(BF16), vc)
                cross = _dot(qc, st.astype(BF16)) * qdec_ref[hd]
                state_ref[hd] = st * cdec_ref[hd] + _dot_tn((kc * kdec_ref[hd]).astype(BF16), vc)
                y = inner + cross
                y_ref[rows, hd * RET_V_DIM:(hd + 1) * RET_V_DIM] = y * lax.rsqrt(
                    jnp.mean(y * y, axis=-1, keepdims=True) + EPS)
        gate = _dot(a, win_ref[:, RET_G_OFF:RET_G_OFF + RET_HEADS * RET_V_DIM])
        z = (gate * jax.nn.sigmoid(gate) * y_ref[...]).astype(BF16)
        o_ref[prow, :] = x + _rms(_dot(z, wout_ref[...]), g1_ref[...])


def _retention_consts(seq_len):
    hh, c, dk = RET_HEADS, RET_CHUNK, RET_QK_DIM
    inv_freq = np.float32(ROPE_BASE) ** (-np.arange(0, dk, 2, dtype=np.float32) / np.float32(dk))
    ang = np.arange(seq_len, dtype=np.float32)[:, None] * inv_freq[None, :]
    cos = np.cos(ang.astype(np.float64)).astype(np.float32)
    sin = np.sin(ang.astype(np.float64)).astype(np.float32)
    log_gamma = np.log1p(-np.exp2(-5.0 - np.arange(hh, dtype=np.float64)))
    idx = np.arange(c, dtype=np.float64)
    diff = idx[:, None] - idx[None, :]
    dmask = np.where(diff >= 0, np.exp(log_gamma[:, None, None] * np.maximum(diff, 0.0)), 0.0)
    qdec = np.exp(log_gamma[:, None] * (idx[None, :] + 1.0))[:, :, None]
    kdec = np.exp(log_gamma[:, None] * (c - 1.0 - idx[None, :]))[:, :, None]
    cdec = np.exp(log_gamma * c)[:, None, None]
    return tuple(jnp.asarray(t, F32) for t in (cos, sin, dmask, qdec, kdec, cdec))


def _retention_mixer(h, gains, w_in, w_out, consts, layer, j):
    b, s, d = h.shape
    cos, sin, dmask, qdec, kdec, cdec = consts
    tile = lambda bi, t: (bi, t, 0)
    pos = lambda bi, t: (t, 0)
    gain = lambda k: _layer_resident(gains.shape, layer * N_NORMS + k)
    specs = (_weight(w_in, j), _weight(w_out, j))
    return pl.pallas_call(
        functools.partial(_retention_kernel, specs=specs),
        grid=(b, s // RET_TM),
        in_specs=[
            pl.BlockSpec((None, RET_TM, d), tile),
            pl.BlockSpec((RET_TM, RET_QK_DIM // 2), pos),
            pl.BlockSpec((RET_TM, RET_QK_DIM // 2), pos),
            gain(0), gain(1),
            HBM, HBM,
            _resident(dmask.shape), _resident(qdec.shape), _resident(kdec.shape), _resident(cdec.shape),
        ],
        out_specs=pl.BlockSpec((None, RET_TM, d), tile),
        out_shape=jax.ShapeDtypeStruct((b, s, d), F32),
        scratch_shapes=[
            pltpu.VMEM((RET_HEADS, RET_QK_DIM, RET_V_DIM), F32),
            pltpu.VMEM((RET_SUB, RET_HEADS * RET_V_DIM), F32),
        ] + _weight_scratch(specs),
        compiler_params=_params("arbitrary", "arbitrary"),
        name="retention_mixer",
    )(h, cos, sin, gains, gains, w_in, w_out, dmask, qdec, kdec, cdec)


GM_TM = 1024
GM_SUB = 512


def _gmlp_kernel(h_ref, g0_ref, g1_ref, win_hbm, lng_ref, lnb_ref, ws_ref, bs_ref, wout_hbm, o_ref, s_ref, *wscr,
                 specs):
    win_ref, wout_ref = _resident_weights(pl.program_id(0) == 0, specs, (win_hbm, wout_hbm), wscr)

    def gelu(t):
        return 0.5 * t * (1.0 + lax.erf(t * (2.0 ** -0.5)))

    row = lax.broadcasted_iota(jnp.int32, (GM_CHUNK, GM_CHUNK), 0)
    col = lax.broadcasted_iota(jnp.int32, (GM_CHUNK, GM_CHUNK), 1)
    causal = row >= col
    for part in range(GM_TM // GM_SUB):
        prow = slice(part * GM_SUB, (part + 1) * GM_SUB)
        x = h_ref[prow, :]
        a = _rms(x, g0_ref[...]).astype(BF16)
        v = gelu(_dot(a, win_ref[:, GM_WIDTH:]))
        u = gelu(_dot(a, win_ref[:, :GM_WIDTH]))
        mu = jnp.mean(v, axis=-1, keepdims=True)
        vc = v - mu
        var = jnp.mean(vc * vc, axis=-1, keepdims=True)
        vn = (vc * lax.rsqrt(var + EPS) * lng_ref[...] + lnb_ref[...]).astype(BF16)
        for g in range(GM_GROUPS):
            w = jnp.where(causal, ws_ref[g], 0.0).astype(BF16)
            cols = slice(g * GM_GROUP_DIM, (g + 1) * GM_GROUP_DIM)
            for c in range(GM_SUB // GM_CHUNK):
                rows = slice(c * GM_CHUNK, (c + 1) * GM_CHUNK)
                s_ref[rows, cols] = _dot(w, vn[rows, cols]) + bs_ref[g]
        z = (u * s_ref[...]).astype(BF16)
        o_ref[prow, :] = x + _rms(_dot(z, wout_ref[...]), g1_ref[...])


def _gmlp_mixer(h, gains, w_in, ln_g, ln_b, w_s, b_s, w_out, layer, j):
    n, d = h.shape
    row = lambda i: (i, 0)
    gain = lambda k: _layer_resident(gains.shape, layer * N_NORMS + k)
    specs = (_weight(w_in, j), _weight(w_out, j))
    return pl.pallas_call(
        functools.partial(_gmlp_kernel, specs=specs),
        grid=(n // GM_TM,),
        in_specs=[
            pl.BlockSpec((GM_TM, d), row),
            gain(0), gain(1),
            HBM, _layer_resident(ln_g.shape, j), _layer_resident(ln_b.shape, j),
            _layer_resident(w_s.shape, j), _layer_resident(b_s.shape, j), HBM,
        ],
        out_specs=pl.BlockSpec((GM_TM, d), row),
        out_shape=jax.ShapeDtypeStruct((n, d), F32),
        scratch_shapes=[pltpu.VMEM((GM_SUB, GM_WIDTH), F32)] + _weight_scratch(specs),
        compiler_params=_params("arbitrary"),
        name="gmlp_mixer",
    )(h, gains, gains, w_in, ln_g, ln_b, w_s, b_s, w_out)


ATT_D = ATT_HEADS * ATT_HEAD_DIM


class _GroupPlan(NamedTuple):
    dilation: int
    view_minor: int
    block_rows: int
    halves: int
    n_units: int
    n_slots: int
    pieces: tuple
    first_units: tuple
    positions: np.ndarray

    @property
    def tokens_per_step(self):
        return self.n_units * ATT_BLK


def _group_plan(dilation):
    if dilation == 1:
        return _GroupPlan(1, 0, 8 * ATT_BLK, 1, 8, 1,
                          tuple(((None, u * ATT_BLK, ATT_BLK),) for u in range(8)), (0,),
                          np.arange(ATT_BLK))
    if dilation == 4:
        half = ATT_BLK // 2
        return _GroupPlan(4, 8, ATT_BLK, 1, 8, 4,
                          tuple(((c, blk * half, half), (c + 4, blk * half, half))
                                for blk in range(2) for c in range(4)),
                          (0, 1, 2, 3),
                          np.concatenate([2 * np.arange(half), 2 * np.arange(half) + 1]))
    assert dilation == 16
    return _GroupPlan(16, 16, ATT_BLK, 2, 8, 8,
                      tuple(((c, 0, ATT_BLK),) for c in range(8)), tuple(range(8)),
                      np.arange(ATT_BLK))


def _t5_bucket(dist):
    d = jnp.maximum(dist, 1).astype(F32)
    large = REL_MAX_EXACT + (jnp.log(d / REL_MAX_EXACT) / math.log(REL_MAX_DIST / REL_MAX_EXACT)
                             * (REL_BUCKETS - REL_MAX_EXACT)).astype(jnp.int32)
    large = jnp.minimum(large, REL_BUCKETS - 1)
    return jnp.where(dist < REL_MAX_EXACT, dist, large)


def _bias_kernel(table_ref, bucket_ref, dist_ref, o_ref):
    g = pl.program_id(0)
    bucket = bucket_ref[...]
    dist = dist_ref[...]
    band = (dist >= 0) & (dist <= ATT_BLK)
    for hd in range(ATT_HEADS):
        acc = jnp.zeros(bucket.shape, F32)
        for b in range(REL_BUCKETS):
            acc = jnp.where(bucket == b, table_ref[b, g * ATT_HEADS + hd], acc)
        o_ref[hd] = jnp.where(band, acc * LOG2E, MASK_VALUE)


def _attention_bias(rel_bias, plans):
    dists = []
    for plan in plans:
        pos_k = np.concatenate([plan.positions - ATT_BLK, plan.positions])
        dists.append(plan.positions[:, None] - pos_k[None, :])
    dist = jnp.asarray(np.stack(dists), jnp.int32)
    scale = jnp.asarray([plan.dilation for plan in plans], jnp.int32)[:, None, None]
    buckets = _t5_bucket(jnp.maximum(dist, 0) * scale).astype(jnp.int32)
    tile = pl.BlockSpec((None, ATT_BLK, 2 * ATT_BLK), lambda g: (g, 0, 0))
    return pl.pallas_call(
        _bias_kernel,
        grid=(len(plans),),
        in_specs=[pl.BlockSpec(memory_space=pltpu.SMEM), tile, tile],
        out_specs=pl.BlockSpec((None, ATT_HEADS, ATT_BLK, 2 * ATT_BLK), lambda g: (g, 0, 0, 0)),
        out_shape=jax.ShapeDtypeStruct((len(plans), ATT_HEADS, ATT_BLK, 2 * ATT_BLK), F32),
        compiler_params=_params("arbitrary"),
        name="attention_bias",
    )(rel_bias, buckets, dist)


def _attn_units(plan, t, xa, g0_ref, w_ref, bias_ref, k_ref, v_ref, q_ref, put_o, put_lse):
    @pl.when(t == 0)
    def _():
        k_ref[:, :ATT_BLK, :] = jnp.zeros((plan.n_slots, ATT_BLK, ATT_D), BF16)
        v_ref[:, :ATT_BLK, :] = jnp.zeros((plan.n_slots, ATT_BLK, ATT_D), BF16)

    a = _rms(xa, g0_ref[...]).astype(BF16)
    q_ref[...] = (_dot(a, w_ref[:, :ATT_D]) * (ATT_HEAD_DIM ** -0.5 * LOG2E)).astype(BF16)
    kc = _dot(a, w_ref[:, ATT_D:2 * ATT_D]).astype(BF16)
    vc = _dot(a, w_ref[:, 2 * ATT_D:]).astype(BF16)

    lane = lax.broadcasted_iota(jnp.int32, (ATT_BLK, V7X_LANES), 1)
    low = lane < ATT_HEAD_DIM
    kcol = lax.broadcasted_iota(jnp.int32, (ATT_BLK, 2 * ATT_BLK), 1)
    first_pen = jnp.where((kcol < ATT_BLK) & (t == 0), MASK_VALUE, 0.0)
    for u in range(plan.n_units):
        slot = u % plan.n_slots
        rows = slice(u * ATT_BLK, (u + 1) * ATT_BLK)
        k_ref[slot, ATT_BLK:, :] = kc[rows]
        v_ref[slot, ATT_BLK:, :] = vc[rows]
        m_tile = jnp.zeros((ATT_BLK, V7X_LANES), F32)
        den_tile = jnp.ones((ATT_BLK, V7X_LANES), F32)
        for hp in range(ATT_HEADS // 2):
            cols = slice(hp * V7X_LANES, (hp + 1) * V7X_LANES)
            qp = q_ref[rows, cols]
            kk = k_ref[slot, :, cols]
            vv = v_ref[slot, :, cols]
            outs, lses = [], []
            for sub in range(2):
                qm = jnp.where(low if sub == 0 else ~low, qp, jnp.zeros_like(qp))
                s = _dot_nt(qm, kk) + bias_ref[2 * hp + sub]
                if u in plan.first_units:
                    s = s + first_pen
                m = jnp.max(s, axis=-1, keepdims=True)
                e = jnp.exp2(s - m)
                den = jnp.sum(e, axis=-1, keepdims=True)
                outs.append(_dot(e.astype(BF16), vv) / den)
                lses.append((m, den))
            put_o(rows, cols, jnp.where(low, outs[0], outs[1]))
            m_tile = jnp.where(lane == 2 * hp, lses[0][0], jnp.where(lane == 2 * hp + 1, lses[1][0], m_tile))
            den_tile = jnp.where(lane == 2 * hp, lses[0][1], jnp.where(lane == 2 * hp + 1, lses[1][1], den_tile))
        put_lse(rows, m_tile * LN2 + jnp.log(den_tile))
        k_ref[slot, :ATT_BLK, :] = k_ref[slot, ATT_BLK:, :]
        v_ref[slot, :ATT_BLK, :] = v_ref[slot, ATT_BLK:, :]


def _attn_dense_kernel(x_ref, g0_ref, w_hbm, bias_ref, o_ref, lse_ref, k_ref, v_ref, q_ref, *wscr, plan, specs):
    (w_ref,) = _resident_weights((pl.program_id(0) == 0) & (pl.program_id(2) == 0), specs, (w_hbm,), wscr)

    def put_o(rows, cols, tile):
        o_ref[rows, cols] = tile.astype(o_ref.dtype)

    def put_lse(rows, tile):
        lse_ref[rows, :] = tile

    _attn_units(plan, pl.program_id(2), x_ref[...], g0_ref, w_ref, bias_ref, k_ref, v_ref, q_ref, put_o, put_lse)


def _attn_strided_kernel(x_hbm, g0_ref, w_hbm, bias_ref, o_hbm, lse_hbm, k_ref, v_ref, q_ref,
                         xbuf, obuf, lbuf, xsem, osem, lsem, *wscr, plan, specs, n_batch, n_steps):
    b, hf, t = pl.program_id(0), pl.program_id(1), pl.program_id(2)
    step = (b * plan.halves + hf) * n_steps + t
    n_total = n_batch * plan.halves * n_steps
    buf = lax.rem(step, 2)

    def class_rows(ref, bb, hh, tt, sub, r0, n):
        return ref.at[bb, pl.ds(tt * plan.block_rows + r0, n), hh * V7X_SUBLANES + sub]

    def copies(kind, bb, hh, tt, which):
        out = []
        for u, pcs in enumerate(plan.pieces):
            at = u * ATT_BLK
            for sub, r0, n in pcs:
                dense = pl.ds(at, n)
                if kind == "gather":
                    out.append(pltpu.make_async_copy(class_rows(x_hbm, bb, hh, tt, sub, r0, n),
                                                     xbuf.at[which, dense], xsem.at[which]))
                else:
                    out.append(pltpu.make_async_copy(obuf.at[which, dense],
                                                     class_rows(o_hbm, bb, hh, tt, sub, r0, n), osem.at[which]))
                    out.append(pltpu.make_async_copy(lbuf.at[which, dense],
                                                     class_rows(lse_hbm, bb, hh, tt, sub, r0, n), lsem.at[which]))
                at += n
        return out

    @pl.when(step == 0)
    def _():
        for cp in copies("gather", b, hf, t, buf):
            cp.start()

    wrap_t = t + 1 == n_steps
    wrap_h = hf + 1 == plan.halves
    nt = jnp.where(wrap_t, 0, t + 1)
    nh = jnp.where(wrap_t, jnp.where(wrap_h, 0, hf + 1), hf)
    nb = jnp.where(wrap_t & wrap_h, b + 1, b)

    @pl.when(step + 1 < n_total)
    def _():
        for cp in copies("gather", nb, nh, nt, 1 - buf):
            cp.start()

    (w_ref,) = _resident_weights(step == 0, specs, (w_hbm,), wscr)

    for cp in copies("gather", b, hf, t, buf):
        cp.wait()

    @pl.when(step >= 2)
    def _():
        for cp in copies("scatter", b, hf, t, buf):
            cp.wait()

    def put_o(rows, cols, tile):
        obuf[buf, rows, cols] = tile

    def put_lse(rows, tile):
        lbuf[buf, rows, :] = tile

    _attn_units(plan, t, xbuf[buf], g0_ref, w_ref, bias_ref, k_ref, v_ref, q_ref, put_o, put_lse)

    for cp in copies("scatter", b, hf, t, buf):
        cp.start()

    @pl.when(step == n_total - 1)
    def _():
        for cp in copies("scatter", b, hf, t, 1 - buf):
            cp.wait()
        for cp in copies("scatter", b, hf, t, buf):
            cp.wait()


def _attention_group(h, gains, w_in, bias, plan, group, layer, j):
    b, s, d = h.shape
    tokens = plan.tokens_per_step
    specs = (_weight(w_in, j, col0=group * 3 * ATT_D, cols=3 * ATT_D),)
    resident = [
        _layer_resident(gains.shape, layer * N_NORMS),
        HBM,
        pl.BlockSpec((None, ATT_HEADS, ATT_BLK, 2 * ATT_BLK), lambda bi, hf, t: (group, 0, 0, 0),
                     pipeline_mode=pl.Buffered(1)),
    ]
    kv_q = [
        pltpu.VMEM((plan.n_slots, 2 * ATT_BLK, ATT_D), BF16),
        pltpu.VMEM((plan.n_slots, 2 * ATT_BLK, ATT_D), BF16),
        pltpu.VMEM((tokens, ATT_D), BF16),
    ]
    if not plan.view_minor:
        steps = s // plan.block_rows
        block = lambda width: pl.BlockSpec((None, plan.block_rows, width), lambda bi, hf, t: (bi, t, 0))
        o, lse = pl.pallas_call(
            functools.partial(_attn_dense_kernel, plan=plan, specs=specs),
            grid=(b, 1, steps),
            in_specs=[block(d)] + resident,
            out_specs=[block(ATT_D), block(V7X_LANES)],
            out_shape=[jax.ShapeDtypeStruct((b, s, ATT_D), BF16), jax.ShapeDtypeStruct((b, s, V7X_LANES), F32)],
            scratch_shapes=kv_q + _weight_scratch(specs),
            compiler_params=_params("arbitrary", "arbitrary", "arbitrary"),
            name=f"attention_group{group}",
        )(h, gains, w_in, bias)
    else:
        lead = s // plan.view_minor
        steps = lead // plan.block_rows
        assert b * plan.halves * steps >= 2
        view = lambda width: (b, lead, plan.view_minor, width)
        o, lse = pl.pallas_call(
            functools.partial(_attn_strided_kernel, plan=plan, specs=specs, n_batch=b, n_steps=steps),
            grid=(b, plan.halves, steps),
            in_specs=[HBM] + resident,
            out_specs=[HBM, HBM],
            out_shape=[jax.ShapeDtypeStruct(view(ATT_D), F32), jax.ShapeDtypeStruct(view(V7X_LANES), F32)],
            scratch_shapes=kv_q + [
                pltpu.VMEM((2, tokens, d), F32),
                pltpu.VMEM((2, tokens, ATT_D), F32),
                pltpu.VMEM((2, tokens, V7X_LANES), F32),
                pltpu.SemaphoreType.DMA((2,)), pltpu.SemaphoreType.DMA((2,)), pltpu.SemaphoreType.DMA((2,)),
            ] + _weight_scratch(specs),
            compiler_params=_params("arbitrary", "arbitrary", "arbitrary"),
            name=f"attention_group{group}",
        )(h.reshape(view(d)), gains, w_in, bias)
    return o.reshape(b * s, ATT_D), lse.reshape(b * s, V7X_LANES)


ATT_MERGE_TM = 512
ATT_MERGE_VMEM_LIMIT_BYTES = V7X_VMEM_BYTES - 4 * 1024 * 1024


def _attn_merge_ffn_kernel(h_ref, o0_ref, o1_ref, o2_ref, l0_ref, l1_ref, l2_ref, p_ref, g1_ref, g2_ref, g3_ref,
                           g4_ref, expand_ref, wao_hbm, win_hbm, wout_hbm, wp_hbm, wg_hbm, out_ref, *wscr, specs):
    wout_ref, *ffn_weights = _resident_weights(
        pl.program_id(0) == 0, specs, (wao_hbm, win_hbm, wout_hbm, wp_hbm, wg_hbm), wscr)
    l0, l1, l2 = l0_ref[...], l1_ref[...], l2_ref[...]
    m = jnp.maximum(jnp.maximum(l0, l1), l2)
    e0, e1, e2 = jnp.exp(l0 - m), jnp.exp(l1 - m), jnp.exp(l2 - m)
    inv = 1.0 / (e0 + e1 + e2)
    expand = expand_ref[...]

    def per_channel(w):
        hi = w.astype(BF16)
        lo = (w - hi.astype(F32)).astype(BF16)
        return _dot(jnp.concatenate([hi, lo], axis=1), expand)

    o = (per_channel(e0 * inv) * o0_ref[...].astype(F32) + per_channel(e1 * inv) * o1_ref[...]
         + per_channel(e2 * inv) * o2_ref[...])
    x = h_ref[...] + _rms(_dot(o.astype(BF16), wout_ref[...]), g1_ref[...])
    out_ref[...] = _ffn_rows(x, p_ref[...], g2_ref, g3_ref, g4_ref, *ffn_weights)


def _attention_merge_ffn(h, outs, lses, p, gains, attn_w_out, w_in, w_out, w_proj, w_gate, layer, j):
    n, d = h.shape
    row = lambda i: (i, 0)
    tile = pl.BlockSpec((ATT_MERGE_TM, d), row)
    lse_tile = pl.BlockSpec((ATT_MERGE_TM, V7X_LANES), row)
    expand = np.zeros((V7X_LANES, ATT_D), np.float32)
    for hd in range(ATT_HEADS):
        expand[hd, hd * ATT_HEAD_DIM:(hd + 1) * ATT_HEAD_DIM] = 1.0
    expand = jnp.asarray(np.concatenate([expand, expand], axis=0), BF16)
    gain = lambda k: _layer_resident(gains.shape, layer * N_NORMS + k)
    specs = (_weight(attn_w_out, j),) + tuple(_weight(w, layer) for w in (w_in, w_out, w_proj, w_gate))
    return pl.pallas_call(
        functools.partial(_attn_merge_ffn_kernel, specs=specs),
        grid=(n // ATT_MERGE_TM,),
        in_specs=[tile] * 4 + [lse_tile] * 3 + [
            pl.BlockSpec((None, ATT_MERGE_TM, PLE_DIM), lambda i: (layer, i, 0)),
            gain(1), gain(2), gain(3), gain(4), _resident(expand.shape), HBM, HBM, HBM, HBM, HBM],
        out_specs=tile,
        out_shape=jax.ShapeDtypeStruct((n, d), F32),
        scratch_shapes=_weight_scratch(specs),
        compiler_params=pltpu.CompilerParams(dimension_semantics=("arbitrary",),
                                             vmem_limit_bytes=ATT_MERGE_VMEM_LIMIT_BYTES),
        name="attention_merge_ffn",
    )(h, *outs, *lses, p, gains, gains, gains, gains, expand, attn_w_out, w_in, w_out, w_proj, w_gate)


def kernel(x, p, norm_g, ret_w_in, ret_w_out, attn_w_in, attn_w_out, rel_bias, gm_w_in, gm_ln_g, gm_ln_b,
           gm_w_s, gm_b_s, gm_w_out, ffn_w_in, ffn_w_out, ple_w_proj, ple_w_gate):
    b, s, d = x.shape
    n = b * s
    gains = norm_g.reshape(DEPTH * N_NORMS, 1, d)
    p = p.reshape(DEPTH, n, PLE_DIM)
    gm_ln_g = gm_ln_g.reshape(-1, 1, GM_WIDTH)
    gm_ln_b = gm_ln_b.reshape(-1, 1, GM_WIDTH)
    gm_b_s = gm_b_s.reshape(-1, GM_GROUPS, GM_CHUNK, 1)
    ret_consts = _retention_consts(s)
    plans = tuple(_group_plan(dil) for _, dil in DILATION_PAIRS)

    h = x
    for i in range(DEPTH):
        kind, j = i % N_MIXERS, i // N_MIXERS
        if kind == 0:
            h = _retention_mixer(h.reshape(b, s, d), gains, ret_w_in, ret_w_out, ret_consts, i, j)
        elif kind == 1:
            bias = _attention_bias(rel_bias, plans)
            outs, lses = [], []
            for gi, plan in enumerate(plans):
                o, lse = _attention_group(h.reshape(b, s, d), gains, attn_w_in, bias, plan, gi, i, j)
                outs.append(o)
                lses.append(lse)
            h = _attention_merge_ffn(h.reshape(n, d), outs, lses, p, gains, attn_w_out,
                                     ffn_w_in, ffn_w_out, ple_w_proj, ple_w_gate, i, j)
            continue
        else:
            h = _gmlp_mixer(h.reshape(n, d), gains, gm_w_in, gm_ln_g, gm_ln_b, gm_w_s, gm_b_s, gm_w_out, i, j)
        h = _ffn_ple(h.reshape(n, d), p, gains, ffn_w_in, ffn_w_out, ple_w_proj, ple_w_gate, i)
    return h.reshape(b, s, d)
```
